```python
import math
import jax, jax.numpy as jnp
from jax import lax
import numpy as np

D_MODEL = 1024
BATCH = 4
SEQ = 4096
DEPTH = 2

EPS = 1e-6
D_MIX = D_MODEL
LRU_WIDTH = D_MIX // 2
LRU_BLOCKS = 8
LRU_BLOCK_W = LRU_WIDTH // LRU_BLOCKS
LRU_C = 8.0
CONV_W = 4
N_HEADS = 8
QK_NOPE = 64
QK_ROPE = 32
V_HEAD = 64
Q_LORA = 256
KV_LORA = 128
ROPE_THETA = 10000.0
Q_BLOCK = 128
IN_SPLITS = (LRU_WIDTH, 2 * LRU_WIDTH, 2 * LRU_WIDTH + Q_LORA,
             2 * LRU_WIDTH + Q_LORA + KV_LORA)
IN_COLS = 2 * LRU_WIDTH + Q_LORA + KV_LORA + QK_ROPE
D_FF_DENSE = 2816
N_EXPERTS = 8
TOP_K = 2
D_FF_EXPERT = 1792
N_DENSE = (DEPTH + 1) // 2
N_MOE = DEPTH // 2

kernel_name = "hymba_rglru_mla_moe_trunk"


def rmsnorm(x, g):
    xf = x.astype(jnp.float32)
    y = xf * lax.rsqrt(jnp.mean(xf * xf, axis=-1, keepdims=True) + EPS)
    return (y * g.astype(jnp.float32)).astype(x.dtype)


def rope_tables(positions):
    half = QK_ROPE // 2
    inv = 1.0 / (ROPE_THETA ** (jnp.arange(half, dtype=jnp.float32) / half))
    ang = positions.astype(jnp.float32)[..., None] * inv
    return jnp.cos(ang), jnp.sin(ang)


def apply_rope(x, cos, sin):
    xf = x.astype(jnp.float32)
    x1, x2 = xf[..., : QK_ROPE // 2], xf[..., QK_ROPE // 2:]
    out = jnp.concatenate([x1 * cos - x2 * sin, x2 * cos + x1 * sin], axis=-1)
    return out.astype(x.dtype)


def causal_depthwise_conv(x, w, b):
    y = lax.conv_general_dilated(
        x, w[:, None, :].astype(x.dtype), window_strides=(1,),
        padding=((CONV_W - 1, 0),), dimension_numbers=("NWC", "WIO", "NWC"),
        feature_group_count=x.shape[-1])
    return y + b


def rg_lru(x, wa, ba, wx, bx, lam):
    B, S, C = x.shape
    xb = x.reshape(B, S, LRU_BLOCKS, LRU_BLOCK_W)
    r = jax.nn.sigmoid(jnp.einsum("bsnc,ncd->bsnd", xb, wa) + ba).reshape(B, S, C)
    i = jax.nn.sigmoid(jnp.einsum("bsnc,ncd->bsnd", xb, wx) + bx).reshape(B, S, C)
    log_a = -LRU_C * r.astype(jnp.float32) * jax.nn.softplus(-lam.astype(jnp.float32))
    a = jnp.exp(log_a)
    mult = jnp.sqrt(-jnp.expm1(2.0 * log_a))
    mult = jnp.where((jnp.arange(S) == 0)[None, :, None], jnp.float32(1.0), mult)
    u = mult * (i * x).astype(jnp.float32)

    def combine(left, right):
        a_l, b_l = left
        a_r, b_r = right
        return a_l * a_r, a_r * b_l + b_r

    _, h = lax.associative_scan(combine, (a, u), axis=1)
    return h.astype(x.dtype)


def mla_causal_attention(q_nope, q_rope, k_nope, k_rope, v):
    B, S, H, _ = q_nope.shape
    nb = S // Q_BLOCK
    scale = 1.0 / math.sqrt(QK_NOPE + QK_ROPE)
    qn = q_nope.reshape(B, nb, Q_BLOCK, H, QK_NOPE).transpose(1, 0, 2, 3, 4)
    qr = q_rope.reshape(B, nb, Q_BLOCK, H, QK_ROPE).transpose(1, 0, 2, 3, 4)
    k_pos = jnp.arange(S)

    def one_block(args):
        qn_b, qr_b, bi = args
        s = (jnp.einsum("bqhd,bkhd->bhqk", qn_b, k_nope, preferred_element_type=jnp.float32)
             + jnp.einsum("bqhd,bkd->bhqk", qr_b, k_rope, preferred_element_type=jnp.float32)) * scale
        q_pos = bi * Q_BLOCK + jnp.arange(Q_BLOCK)
        s = jnp.where((k_pos[None, :] <= q_pos[:, None])[None, None], s, -jnp.inf)
        p = jax.nn.softmax(s, axis=-1)
        return jnp.einsum("bhqk,bkhd->bqhd", p.astype(v.dtype), v)

    o = lax.map(one_block, (qn, qr, jnp.arange(nb)))
    return o.transpose(1, 0, 2, 3, 4).reshape(B, S, H * V_HEAD)


def swiglu(t, wg, wu, wd):
    return (jax.nn.silu(t @ wg) * (t @ wu)) @ wd


def moe_swiglu(h, router_w, wg, wu, wd):
    B, S, D = h.shape
    t = h.reshape(B * S, D)
    logits = (t @ router_w).astype(jnp.float32)
    top_v, top_i = lax.top_k(logits, TOP_K)
    top_w = jax.nn.softmax(top_v, axis=-1)
    gates = jnp.sum(jax.nn.one_hot(top_i, N_EXPERTS, dtype=jnp.float32) * top_w[..., None], axis=1)
    out = jnp.zeros((B * S, D), jnp.float32)
    for e in range(N_EXPERTS):
        out = out + gates[:, e:e + 1] * swiglu(t, wg[e], wu[e], wd[e]).astype(jnp.float32)
    return out.astype(h.dtype).reshape(B, S, D)


def setup_inputs(seed: int = 0) -> dict:
    key = jax.random.key(seed)
    ks = jax.random.split(key, 32)
    f32 = jnp.float32

    def nrm(k, shape, fan_in):
        return jax.random.normal(k, shape, f32) * (fan_in ** -0.5)

    def gain(k, shape):
        return 1.0 + 0.02 * jax.random.normal(k, shape, f32)

    def bias(k, shape):
        return 0.02 * jax.random.normal(k, shape, f32)

    u = jax.random.uniform(ks[10], (DEPTH, LRU_WIDTH), f32, minval=0.9, maxval=0.999)
    a0 = u ** (1.0 / LRU_C)
    lru_lambda = jnp.log(a0) - jnp.log1p(-a0)

    return {
        "x": jax.random.normal(ks[0], (BATCH, SEQ, D_MODEL), f32),
        "positions": jnp.broadcast_to(jnp.arange(SEQ, dtype=jnp.int32), (BATCH, SEQ)),
        "norm_mix": gain(ks[1], (DEPTH, D_MODEL)),
        "w_in": nrm(ks[2], (DEPTH, D_MODEL, IN_COLS), D_MODEL),
        "conv_w": nrm(ks[3], (DEPTH, CONV_W, LRU_WIDTH), CONV_W),
        "conv_b": bias(ks[4], (DEPTH, LRU_WIDTH)),
        "lru_wa": nrm(ks[5], (DEPTH, LRU_BLOCKS, LRU_BLOCK_W, LRU_BLOCK_W), LRU_BLOCK_W),
        "lru_ba": bias(ks[6], (DEPTH, LRU_BLOCKS, LRU_BLOCK_W)),
        "lru_wx": nrm(ks[7], (DEPTH, LRU_BLOCKS, LRU_BLOCK_W, LRU_BLOCK_W), LRU_BLOCK_W),
        "lru_bx": bias(ks[8], (DEPTH, LRU_BLOCKS, LRU_BLOCK_W)),
        "lru_lambda": lru_lambda,
        "q_norm": gain(ks[11], (DEPTH, Q_LORA)),
        "w_uq": nrm(ks[12], (DEPTH, Q_LORA, N_HEADS * (QK_NOPE + QK_ROPE)), Q_LORA),
        "kv_norm": gain(ks[13], (DEPTH, KV_LORA)),
        "w_ukv": nrm(ks[14], (DEPTH, KV_LORA, N_HEADS * (QK_NOPE + V_HEAD)), KV_LORA),
        "w_out": nrm(ks[15], (DEPTH, D_MIX, D_MODEL), D_MIX),
        "norm_ffn": gain(ks[16], (DEPTH, D_MODEL)),
        "dense_w_gate": nrm(ks[17], (N_DENSE, D_MODEL, D_FF_DENSE), D_MODEL),
        "dense_w_up": nrm(ks[18], (N_DENSE, D_MODEL, D_FF_DENSE), D_MODEL),
        "dense_w_down": nrm(ks[19], (N_DENSE, D_FF_DENSE, D_MODEL), D_FF_DENSE),
        "router_w": nrm(ks[20], (N_MOE, D_MODEL, N_EXPERTS), D_MODEL),
        "expert_w_gate": nrm(ks[21], (N_MOE, N_EXPERTS, D_MODEL, D_FF_EXPERT), D_MODEL),
        "expert_w_up": nrm(ks[22], (N_MOE, N_EXPERTS, D_MODEL, D_FF_EXPERT), D_MODEL),
        "expert_w_down": nrm(ks[23], (N_MOE, N_EXPERTS, D_FF_EXPERT, D_MODEL), D_FF_EXPERT),
        "norm_final": gain(ks[24], (D_MODEL,)),
    }


def reference(x, positions, norm_mix, w_in, conv_w, conv_b, lru_wa, lru_ba, lru_wx, lru_bx,
              lru_lambda, q_norm, w_uq, kv_norm, w_ukv, w_out, norm_ffn, dense_w_gate,
              dense_w_up, dense_w_down, router_w, expert_w_gate, expert_w_up, expert_w_down,
              norm_final):
    B, S, _ = x.shape
    cos, sin = rope_tables(positions)
    for l in range(DEPTH):
        h = rmsnorm(x, norm_mix[l])
        z = h @ w_in[l]
        x_lru, y_gate, c_q, c_kv, k_rope = jnp.split(z, IN_SPLITS, axis=-1)

        xc = causal_depthwise_conv(x_lru, conv_w[l], conv_b[l])
        hl = rg_lru(xc, lru_wa[l], lru_ba[l], lru_wx[l], lru_bx[l], lru_lambda[l])
        lru_out = hl * jax.nn.gelu(y_gate)

        q = (rmsnorm(c_q, q_norm[l]) @ w_uq[l]).reshape(B, S, N_HEADS, QK_NOPE + QK_ROPE)
        kv = (rmsnorm(c_kv, kv_norm[l]) @ w_ukv[l]).reshape(B, S, N_HEADS, QK_NOPE + V_HEAD)
        q_nope = q[..., :QK_NOPE]
        q_rope = apply_rope(q[..., QK_NOPE:], cos[:, :, None, :], sin[:, :, None, :])
        k_nope, v = kv[..., :QK_NOPE], kv[..., QK_NOPE:]
        k_rope = apply_rope(k_rope, cos, sin)
        attn_out = mla_causal_attention(q_nope, q_rope, k_nope, k_rope, v)

        x = x + jnp.concatenate([lru_out, attn_out], axis=-1) @ w_out[l]

        h2 = rmsnorm(x, norm_ffn[l])
        if l % 2 == 0:
            j = l // 2
            ffn = swiglu(h2, dense_w_gate[j], dense_w_up[j], dense_w_down[j])
        else:
            j = l // 2
            ffn = moe_swiglu(h2, router_w[j], expert_w_gate[j], expert_w_up[j], expert_w_down[j])
        x = x + ffn
    return rmsnorm(x, norm_final)
```

```python
import functools
import math

import jax
import jax.numpy as jnp
from jax import lax
from jax.experimental import pallas as pl
from jax.experimental.pallas import tpu as pltpu

D_MODEL = 1024
EPS = 1e-6
LRU_WIDTH = 512
LRU_BLOCKS = 8
LRU_BLOCK_W = 64
LRU_C = 8.0
CONV_W = 4
N_HEADS = 8
QK_NOPE = 64
QK_ROPE = 32
V_HEAD = 64
Q_LORA = 256
KV_LORA = 128
ROPE_THETA = 10000.0
N_EXPERTS = 8
D_FF_DENSE = 2816
D_FF_EXPERT = 1792

LANES = 128
HEAD_PAD = 128
C_XLRU = 0
C_GATE = LRU_WIDTH
C_CQ = 2 * LRU_WIDTH
C_CKV = C_CQ + Q_LORA
C_KR = C_CKV + KV_LORA
C_KRS = C_KR + HEAD_PAD
IN_COLS_PAD = C_KRS + HEAD_PAD

VMEM_LIMIT = 56 * 1024 * 1024

F32 = jnp.float32
BF16 = jnp.bfloat16


def _cparams(*sem):
    return pltpu.CompilerParams(dimension_semantics=sem, vmem_limit_bytes=VMEM_LIMIT)


def _rms(x, g):
    return x * lax.rsqrt(jnp.mean(x * x, axis=-1, keepdims=True) + EPS) * g


def _full(shape):
    nd = len(shape)
    return pl.BlockSpec(shape, lambda *_: (0,) * nd)


def _rope_kernel(pos_ref, inv_ref, sgn_ref, c_ref, s_ref):
    ang = pos_ref[...].astype(F32) * inv_ref[...]
    lane = lax.broadcasted_iota(jnp.int32, ang.shape, 1)
    c_ref[...] = jnp.where(lane < QK_NOPE, 1.0,
                           jnp.where(lane < QK_NOPE + QK_ROPE, jnp.cos(ang), 0.0))
    s_ref[...] = jnp.sin(ang) * sgn_ref[...]


def _rope_tables(positions):
    T = positions.size
    tt = min(T, 2048)
    half = QK_ROPE // 2
    inv = 1.0 / (ROPE_THETA ** (jnp.arange(half, dtype=F32) / half))
    zeros = jnp.zeros((QK_NOPE,), F32)
    inv128 = jnp.concatenate([zeros, inv, inv, jnp.zeros((32,), F32)])[None, :]
    sgn128 = jnp.concatenate([zeros, -jnp.ones((half,), F32), jnp.ones((half,), F32),
                              jnp.zeros((32,), F32)])[None, :]
    return pl.pallas_call(
        _rope_kernel,
        grid=(T // tt,),
        in_specs=[pl.BlockSpec((tt, 1), lambda i: (i, 0)), _full((1, LANES)), _full((1, LANES))],
        out_specs=[pl.BlockSpec((tt, LANES), lambda i: (i, 0))] * 2,
        out_shape=[jax.ShapeDtypeStruct((T, LANES), F32)] * 2,
        compiler_params=_cparams("arbitrary"),
        name="rope_tables",
    )(positions.reshape(T, 1), inv128, sgn128)


def _mix_in_kernel(x_ref, gmix_ref, win_ref, convw_ref, convb_ref, wgate_ref, bgate_ref,
                   lam_ref, gq_ref, wq_ref, gkv_ref, wkv_ref, c_ref, s_ref,
                   lru_ref, q_ref, k_ref, v_ref, halo_ref, hcarry_ref, *, ts):
    si = pl.program_id(1)

    @pl.when(si == 0)
    def _():
        halo_ref[0:8, :] = jnp.zeros((8, LRU_WIDTH), F32)
        hcarry_ref[...] = jnp.zeros_like(hcarry_ref)

    h = _rms(x_ref[...], gmix_ref[...]).astype(BF16)
    z = jnp.dot(h, win_ref[...], preferred_element_type=F32)

    halo_ref[8:, :] = z[:, C_XLRU:C_XLRU + LRU_WIDTH]
    xc = convb_ref[...] + jnp.zeros((ts, LRU_WIDTH), F32)
    for kk in range(CONV_W):
        off = 8 - (CONV_W - 1) + kk
        xc = xc + convw_ref[kk:kk + 1, :] * halo_ref[off:off + ts, :]
    halo_ref[0:8, :] = halo_ref[ts:ts + 8, :]

    gates = jnp.dot(xc.astype(BF16), wgate_ref[...], preferred_element_type=F32) + bgate_ref[...]
    r = jax.nn.sigmoid(gates[:, :LRU_WIDTH])
    ig = jax.nn.sigmoid(gates[:, LRU_WIDTH:])
    nlam = -lam_ref[...]
    softplus = jnp.maximum(nlam, 0.0) + jnp.log1p(jnp.exp(-jnp.abs(nlam)))
    log_a = -LRU_C * r * softplus
    a = jnp.exp(log_a)
    mult = jnp.sqrt(-jnp.tanh(log_a) * (a * a + 1.0))
    row = lax.broadcasted_iota(jnp.int32, (ts, LRU_WIDTH), 0)
    mult = jnp.where(row + si * ts == 0, 1.0, mult)
    b = mult * (ig * xc)

    d = 1
    while d < ts:
        keep = row >= d
        a_sh = jnp.where(keep, pltpu.roll(a, d, 0), 1.0)
        b_sh = jnp.where(keep, pltpu.roll(b, d, 0), 0.0)
        b = a * b_sh + b
        a = a * a_sh
        d *= 2
    hseq = b + a * hcarry_ref[...]
    hcarry_ref[...] = hseq[ts - 1:ts, :]
    lru_ref[...] = (hseq * jax.nn.gelu(z[:, C_GATE:C_GATE + LRU_WIDTH])).astype(BF16)

    cmul = c_ref[...]
    smul = s_ref[...]
    hq = _rms(z[:, C_CQ:C_CQ + Q_LORA], gq_ref[...]).astype(BF16)
    qq = jnp.dot(hq, wq_ref[...], preferred_element_type=F32)
    hkv = _rms(z[:, C_CKV:C_CKV + KV_LORA], gkv_ref[...]).astype(BF16)
    kv = jnp.dot(hkv, wkv_ref[...], preferred_element_type=F32)
    kr = z[:, C_KR:C_KR + HEAD_PAD] * cmul + z[:, C_KRS:C_KRS + HEAD_PAD] * smul
    nq = N_HEADS * HEAD_PAD
    for hh in range(N_HEADS):
        lo = hh * HEAD_PAD
        q_ref[:, lo:lo + HEAD_PAD] = (qq[:, lo:lo + HEAD_PAD] * cmul
                                      + qq[:, nq + lo:nq + lo + HEAD_PAD] * smul).astype(BF16)
        k_ref[:, lo:lo + HEAD_PAD] = (kv[:, lo:lo + HEAD_PAD] + kr).astype(BF16)
    v_ref[...] = kv[:, nq:].astype(BF16)


def _mix_in(x2, B, S, p, ctab, stab, ts):
    T = B * S
    ns = S // ts
    row = lambda b, s: (b * ns + s, 0)
    kern = functools.partial(_mix_in_kernel, ts=ts)
    nq = N_HEADS * HEAD_PAD
    return pl.pallas_call(
        kern,
        grid=(B, ns),
        in_specs=[
            pl.BlockSpec((ts, D_MODEL), row),
            _full((1, D_MODEL)), _full((D_MODEL, IN_COLS_PAD)),
            _full((CONV_W, LRU_WIDTH)), _full((1, LRU_WIDTH)),
            _full((LRU_WIDTH, 2 * LRU_WIDTH)), _full((1, 2 * LRU_WIDTH)),
            _full((1, LRU_WIDTH)),
            _full((1, Q_LORA)), _full((Q_LORA, 2 * nq)),
            _full((1, KV_LORA)), _full((KV_LORA, nq + N_HEADS * V_HEAD)),
            pl.BlockSpec((ts, LANES), row), pl.BlockSpec((ts, LANES), row),
        ],
        out_specs=[
            pl.BlockSpec((ts, LRU_WIDTH), row),
            pl.BlockSpec((ts, nq), row),
            pl.BlockSpec((ts, nq), row),
            pl.BlockSpec((ts, N_HEADS * V_HEAD), row),
        ],
        out_shape=[
            jax.ShapeDtypeStruct((T, LRU_WIDTH), BF16),
            jax.ShapeDtypeStruct((T, nq), BF16),
            jax.ShapeDtypeStruct((T, nq), BF16),
            jax.ShapeDtypeStruct((T, N_HEADS * V_HEAD), BF16),
        ],
        scratch_shapes=[pltpu.VMEM((ts + 8, LRU_WIDTH), F32), pltpu.VMEM((1, LRU_WIDTH), F32)],
        compiler_params=_cparams("arbitrary", "arbitrary"),
        name="mix_in",
    )(x2, p["g_mix"], p["w_in"], p["conv_w"], p["conv_b"], p["w_gate"], p["b_gate"], p["lam"],
      p["g_q"], p["w_q"], p["g_kv"], p["w_kv"], ctab, stab)


def _attn_kernel(q_ref, k_ref, v_ref, o_ref, *, tq):
    i = pl.program_id(2)
    row = lax.broadcasted_iota(jnp.int32, (tq, tq), 0)
    col = lax.broadcasted_iota(jnp.int32, (tq, tq), 1)
    outs = []
    for hh in range(2):
        lo = hh * HEAD_PAD
        q = q_ref[:, lo:lo + HEAD_PAD]

        def step(j, carry, masked, lo=lo, q=q):
            m, l, acc = carry
            start = pl.multiple_of(j * tq, tq)
            kj = k_ref[pl.ds(start, tq), lo:lo + HEAD_PAD]
            vj = v_ref[pl.ds(start, tq), :]
            s = lax.dot_general(q, kj, (((1,), (1,)), ((), ())), preferred_element_type=F32)
            if masked:
                s = jnp.where(col <= row, s, -jnp.inf)
            m_new = jnp.maximum(m, jnp.max(s, axis=1, keepdims=True))
            alpha = jnp.exp(m - m_new)
            pexp = jnp.exp(s - m_new)
            l = alpha * l + jnp.sum(pexp, axis=1, keepdims=True)
            acc = alpha * acc + jnp.dot(pexp.astype(BF16), vj, preferred_element_type=F32)
            return m_new, l, acc

        init = (jnp.full((tq, 1), -jnp.inf, F32), jnp.zeros((tq, 1), F32),
                jnp.zeros((tq, 2 * V_HEAD), F32))
        carry = lax.fori_loop(0, i, functools.partial(step, masked=False), init)
        m, l, acc = step(i, carry, True)
        outs.append(acc / l)
    lane = lax.broadcasted_iota(jnp.int32, (tq, 2 * V_HEAD), 1)
    o_ref[...] = jnp.where(lane < V_HEAD, outs[0], outs[1]).astype(BF16)


def _attention(q, k, v, B, S, tq):
    T = B * S
    nq = S // tq
    npair = N_HEADS // 2
    return pl.pallas_call(
        functools.partial(_attn_kernel, tq=tq),
        grid=(B, npair, nq),
        in_specs=[
            pl.BlockSpec((tq, 2 * HEAD_PAD), lambda b, h, i: (b * nq + i, h)),
            pl.BlockSpec((S, 2 * HEAD_PAD), lambda b, h, i: (b, h)),
            pl.BlockSpec((S, 2 * V_HEAD), lambda b, h, i: (b, h)),
        ],
        out_specs=pl.BlockSpec((tq, 2 * V_HEAD), lambda b, h, i: (b * nq + i, h)),
        out_shape=jax.ShapeDtypeStruct((T, N_HEADS * V_HEAD), BF16),
        compiler_params=_cparams("arbitrary", "arbitrary", "arbitrary"),
        name="mla_attention",
    )(q, k, v)


def _mix_out_kernel(x_ref, lru_ref, att_ref, wo_ref, gffn_ref, xo_ref, h2_ref):
    mixed = jnp.concatenate([lru_ref[...], att_ref[...]], axis=1)
    xn = x_ref[...] + jnp.dot(mixed, wo_ref[...], preferred_element_type=F32)
    xo_ref[...] = xn
    h2_ref[...] = _rms(xn, gffn_ref[...]).astype(BF16)


def _mix_out_router_kernel(x_ref, lru_ref, att_ref, wo_ref, gffn_ref, wr_ref,
                           xo_ref, h2_ref, gates_ref):
    mixed = jnp.concatenate([lru_ref[...], att_ref[...]], axis=1)
    xn = x_ref[...] + jnp.dot(mixed, wo_ref[...], preferred_element_type=F32)
    xo_ref[...] = xn
    h2 = _rms(xn, gffn_ref[...])
    h2_ref[...] = h2.astype(BF16)
    logits = jnp.dot(h2, wr_ref[...], preferred_element_type=F32,
                     precision=lax.Precision.HIGHEST)
    lane = lax.broadcasted_iota(jnp.int32, logits.shape, 1)
    logits = jnp.where(lane < N_EXPERTS, logits, -jnp.inf)
    m1 = jnp.max(logits, axis=1, keepdims=True)
    i1 = jnp.min(jnp.where(logits == m1, lane, LANES), axis=1, keepdims=True)
    rest = jnp.where(lane == i1, -jnp.inf, logits)
    m2 = jnp.max(rest, axis=1, keepdims=True)
    i2 = jnp.min(jnp.where(rest == m2, lane, LANES), axis=1, keepdims=True)
    e2 = jnp.exp(m2 - m1)
    den = 1.0 + e2
    gates_ref[...] = jnp.where(lane == i1, 1.0 / den, 0.0) + jnp.where(lane == i2, e2 / den, 0.0)


def _mix_out(x2, lru, att, p, tm, with_router):
    T = x2.shape[0]
    row = lambda i: (i, 0)
    in_specs = [
        pl.BlockSpec((tm, D_MODEL), row),
        pl.BlockSpec((tm, LRU_WIDTH), row),
        pl.BlockSpec((tm, N_HEADS * V_HEAD), row),
        _full((D_MODEL, D_MODEL)), _full((1, D_MODEL)),
    ]
    out_specs = [pl.BlockSpec((tm, D_MODEL), row), pl.BlockSpec((tm, D_MODEL), row)]
    out_shape = [jax.ShapeDtypeStruct((T, D_MODEL), F32), jax.ShapeDtypeStruct((T, D_MODEL), BF16)]
    args = [x2, lru, att, p["w_out"], p["g_ffn"]]
    if with_router:
        in_specs.append(_full((D_MODEL, LANES)))
        out_specs.append(pl.BlockSpec((tm, LANES), row))
        out_shape.append(jax.ShapeDtypeStruct((T, LANES), F32))
        args.append(p["w_router"])
    return pl.pallas_call(
        _mix_out_router_kernel if with_router else _mix_out_kernel,
        grid=(T // tm,),
        in_specs=in_specs, out_specs=out_specs, out_shape=out_shape,
        compiler_params=_cparams("arbitrary"),
        name="mix_out_router" if with_router else "mix_out",
    )(*args)


def _ffn_dense_kernel(x_ref, h2_ref, wg_ref, wu_ref, wd_ref, o_ref, *, fc):
    h2 = h2_ref[...]
    acc = x_ref[...]
    for c in range(D_FF_DENSE // fc):
        g = jnp.dot(h2, wg_ref[:, c * fc:(c + 1) * fc], preferred_element_type=F32)
        u = jnp.dot(h2, wu_ref[:, c * fc:(c + 1) * fc], preferred_element_type=F32)
        act = (g * jax.nn.sigmoid(g) * u).astype(BF16)
        acc = acc + jnp.dot(act, wd_ref[c * fc:(c + 1) * fc, :], preferred_element_type=F32)
    o_ref[...] = acc


def _ffn_dense(xn, h2, p, tm):
    T = xn.shape[0]
    row = lambda i: (i, 0)
    once = pl.Buffered(1)
    return pl.pallas_call(
        functools.partial(_ffn_dense_kernel, fc=256),
        grid=(T // tm,),
        in_specs=[
            pl.BlockSpec((tm, D_MODEL), row), pl.BlockSpec((tm, D_MODEL), row),
            pl.BlockSpec((D_MODEL, D_FF_DENSE), lambda i: (0, 0), pipeline_mode=once),
            pl.BlockSpec((D_MODEL, D_FF_DENSE), lambda i: (0, 0), pipeline_mode=once),
            pl.BlockSpec((D_FF_DENSE, D_MODEL), lambda i: (0, 0), pipeline_mode=once),
        ],
        out_specs=pl.BlockSpec((tm, D_MODEL), row),
        out_shape=jax.ShapeDtypeStruct((T, D_MODEL), F32),
        compiler_params=_cparams("arbitrary"),
        name="ffn_dense",
    )(xn, h2, p["w_gate_d"], p["w_up_d"], p["w_down_d"])


def _ffn_moe_kernel(x_ref, h2_ref, gates_ref, wg_ref, wu_ref, wd_ref, gfin_ref, o_ref, acc_ref,
                    *, fc, final_norm):
    e = pl.program_id(1)

    @pl.when(e == 0)
    def _():
        acc_ref[...] = x_ref[...]

    h2 = h2_ref[...]
    lane = lax.broadcasted_iota(jnp.int32, gates_ref.shape, 1)
    gate = jnp.sum(jnp.where(lane == e, gates_ref[...], 0.0), axis=1, keepdims=True)
    y = jnp.zeros(acc_ref.shape, F32)
    for c in range(D_FF_EXPERT // fc):
        g = jnp.dot(h2, wg_ref[0, :, c * fc:(c + 1) * fc], preferred_element_type=F32)
        u = jnp.dot(h2, wu_ref[0, :, c * fc:(c + 1) * fc], preferred_element_type=F32)
        act = (g * jax.nn.sigmoid(g) * u).astype(BF16)
        y = y + jnp.dot(act, wd_ref[0, c * fc:(c + 1) * fc, :], preferred_element_type=F32)
    acc_ref[...] += gate * y

    @pl.when(e == N_EXPERTS - 1)
    def _():
        out = acc_ref[...]
        if final_norm:
            out = _rms(out, gfin_ref[...])
        o_ref[...] = out


def _ffn_moe(xn, h2, gates, p, g_final, tm, final_norm):
    T = xn.shape[0]
    row = lambda i, e: (i, 0)
    return pl.pallas_call(
        functools.partial(_ffn_moe_kernel, fc=256, final_norm=final_norm),
        grid=(T // tm, N_EXPERTS),
        in_specs=[
            pl.BlockSpec((tm, D_MODEL), row), pl.BlockSpec((tm, D_MODEL), row),
            pl.BlockSpec((tm, LANES), row),
            pl.BlockSpec((1, D_MODEL, D_FF_EXPERT), lambda i, e: (e, 0, 0)),
            pl.BlockSpec((1, D_MODEL, D_FF_EXPERT), lambda i, e: (e, 0, 0)),
            pl.BlockSpec((1, D_FF_EXPERT, D_MODEL), lambda i, e: (e, 0, 0)),
            pl.BlockSpec((1, D_MODEL), lambda i, e: (0, 0)),
        ],
        out_specs=pl.BlockSpec((tm, D_MODEL), row),
        out_shape=jax.ShapeDtypeStruct((T, D_MODEL), F32),
        scratch_shapes=[pltpu.VMEM((tm, D_MODEL), F32)],
        compiler_params=_cparams("arbitrary", "arbitrary"),
        name="ffn_moe",
    )(xn, h2, gates, p["w_gate_e"], p["w_up_e"], p["w_down_e"], g_final)


def _rope_block(w_rope, swap):
    half = QK_ROPE // 2
    if swap:
        w_rope = jnp.concatenate([w_rope[:, half:], w_rope[:, :half]], axis=1)
    k = w_rope.shape[0]
    return jnp.concatenate([jnp.zeros((k, QK_NOPE), w_rope.dtype), w_rope,
                            jnp.zeros((k, HEAD_PAD - QK_NOPE - QK_ROPE), w_rope.dtype)], axis=1)


def _block_diag(w):
    eye = jnp.eye(LRU_BLOCKS, dtype=w.dtype)
    return jnp.einsum("ncd,nm->ncmd", w, eye).reshape(LRU_WIDTH, LRU_WIDTH)


def _layer_params(l, norm_mix, w_in, conv_w, conv_b, lru_wa, lru_ba, lru_wx, lru_bx, lru_lambda,
                  q_norm, w_uq, kv_norm, w_ukv, w_out, norm_ffn):
    scale = 1.0 / math.sqrt(QK_NOPE + QK_ROPE)
    wi = w_in[l]
    w_kr = wi[:, C_CKV + KV_LORA:]
    w_in_pad = jnp.concatenate([wi[:, :C_KR], _rope_block(w_kr, False), _rope_block(w_kr, True)],
                               axis=1).astype(BF16)
    wq = (w_uq[l] * scale).reshape(Q_LORA, N_HEADS, QK_NOPE + QK_ROPE)
    zpad = jnp.zeros((Q_LORA, N_HEADS, HEAD_PAD - QK_NOPE - QK_ROPE), F32)
    zn = jnp.zeros((Q_LORA, N_HEADS, QK_NOPE), F32)
    half = QK_ROPE // 2
    wq_a = jnp.concatenate([wq, zpad], axis=2).reshape(Q_LORA, N_HEADS * HEAD_PAD)
    wq_sw = jnp.concatenate([wq[:, :, QK_NOPE + half:], wq[:, :, QK_NOPE:QK_NOPE + half]], axis=2)
    wq_b = jnp.concatenate([zn, wq_sw, zpad], axis=2).reshape(Q_LORA, N_HEADS * HEAD_PAD)
    wkv = w_ukv[l].reshape(KV_LORA, N_HEADS, QK_NOPE + V_HEAD)
    wk = jnp.concatenate([wkv[:, :, :QK_NOPE], jnp.zeros((KV_LORA, N_HEADS, HEAD_PAD - QK_NOPE), F32)],
                         axis=2).reshape(KV_LORA, N_HEADS * HEAD_PAD)
    wv = wkv[:, :, QK_NOPE:].reshape(KV_LORA, N_HEADS * V_HEAD)
    return {
        "g_mix": norm_mix[l][None, :],
        "w_in": w_in_pad,
        "conv_w": conv_w[l],
        "conv_b": conv_b[l][None, :],
        "w_gate": jnp.concatenate([_block_diag(lru_wa[l]), _block_diag(lru_wx[l])], axis=1).astype(BF16),
        "b_gate": jnp.concatenate([lru_ba[l].reshape(1, -1), lru_bx[l].reshape(1, -1)], axis=1),
        "lam": lru_lambda[l][None, :],
        "g_q": q_norm[l][None, :],
        "w_q": jnp.concatenate([wq_a, wq_b], axis=1).astype(BF16),
        "g_kv": kv_norm[l][None, :],
        "w_kv": jnp.concatenate([wk, wv], axis=1).astype(BF16),
        "w_out": w_out[l].astype(BF16),
        "g_ffn": norm_ffn[l][None, :],
    }


def kernel(x, positions, norm_mix, w_in, conv_w, conv_b, lru_wa, lru_ba, lru_wx, lru_bx, lru_lambda,
           q_norm, w_uq, kv_norm, w_ukv, w_out, norm_ffn, dense_w_gate, dense_w_up, dense_w_down,
           router_w, expert_w_gate, expert_w_up, expert_w_down, norm_final):
    B, S, _ = x.shape
    T = B * S
    depth = norm_mix.shape[0]
    ts = min(S, 256)
    tq = min(S, 512)
    tm = min(T, 512)
    ctab, stab = _rope_tables(positions)
    x2 = x.reshape(T, D_MODEL)
    for l in range(depth):
        p = _layer_params(l, norm_mix, w_in, conv_w, conv_b, lru_wa, lru_ba, lru_wx, lru_bx,
                          lru_lambda, q_norm, w_uq, kv_norm, w_ukv, w_out, norm_ffn)
        lru, q, k, v = _mix_in(x2, B, S, p, ctab, stab, ts)
        att = _attention(q, k, v, B, S, tq)
        j = l // 2
        last = l == depth - 1
        if l % 2 == 0:
            xn, h2 = _mix_out(x2, lru, att, p, tm, with_router=False)
            p["w_gate_d"] = dense_w_gate[j].astype(BF16)
            p["w_up_d"] = dense_w_up[j].astype(BF16)
            p["w_down_d"] = dense_w_down[j].astype(BF16)
            x2 = _ffn_dense(xn, h2, p, tm)
        else:
            p["w_router"] = jnp.pad(router_w[j], ((0, 0), (0, LANES - N_EXPERTS)))
            xn, h2, gates = _mix_out(x2, lru, att, p, tm, with_router=True)
            p["w_gate_e"] = expert_w_gate[j].astype(BF16)
            p["w_up_e"] = expert_w_up[j].astype(BF16)
            p["w_down_e"] = expert_w_down[j].astype(BF16)
            x2 = _ffn_moe(xn, h2, gates, p, norm_final[None, :], min(T, 1024), final_norm=last)
    return x2.reshape(B, S, D_MODEL)
```

```python
import functools
import math

import jax
import jax.numpy as jnp
from jax import lax
from jax.experimental import pallas as pl
from jax.experimental.pallas import tpu as pltpu

D_MODEL = 1024
EPS = 1e-6
LRU_WIDTH = 512
LRU_BLOCKS = 8
LRU_BLOCK_W = 64
LRU_C = 8.0
CONV_W = 4
N_HEADS = 8
QK_NOPE = 64
QK_ROPE = 32
V_HEAD = 64
Q_LORA = 256
KV_LORA = 128
ROPE_THETA = 10000.0
N_EXPERTS = 8
D_FF_DENSE = 2816
D_FF_EXPERT = 1792

LANES = 128
HEAD_PAD = 128
C_XLRU = 0
C_GATE = LRU_WIDTH
C_CQ = 2 * LRU_WIDTH
C_CKV = C_CQ + Q_LORA
C_KR = C_CKV + KV_LORA
C_KRS = C_KR + HEAD_PAD
IN_COLS_PAD = C_KRS + HEAD_PAD

VMEM_LIMIT = 56 * 1024 * 1024

F32 = jnp.float32
BF16 = jnp.bfloat16


def _cparams(*sem):
    return pltpu.CompilerParams(dimension_semantics=sem, vmem_limit_bytes=VMEM_LIMIT)


def _rms(x, g):
    return x * lax.rsqrt(jnp.mean(x * x, axis=-1, keepdims=True) + EPS) * g


def _full(shape):
    nd = len(shape)
    return pl.BlockSpec(shape, lambda *_: (0,) * nd)


def _rope_kernel(pos_ref, inv_ref, sgn_ref, c_ref, s_ref):
    ang = pos_ref[...].astype(F32) * inv_ref[...]
    lane = lax.broadcasted_iota(jnp.int32, ang.shape, 1)
    c_ref[...] = jnp.where(lane < QK_NOPE, 1.0,
                           jnp.where(lane < QK_NOPE + QK_ROPE, jnp.cos(ang), 0.0))
    s_ref[...] = jnp.sin(ang) * sgn_ref[...]


def _rope_tables(positions):
    T = positions.size
    tt = min(T, 2048)
    half = QK_ROPE // 2
    inv = 1.0 / (ROPE_THETA ** (jnp.arange(half, dtype=F32) / half))
    zeros = jnp.zeros((QK_NOPE,), F32)
    inv128 = jnp.concatenate([zeros, inv, inv, jnp.zeros((32,), F32)])[None, :]
    sgn128 = jnp.concatenate([zeros, -jnp.ones((half,), F32), jnp.ones((half,), F32),
                              jnp.zeros((32,), F32)])[None, :]
    return pl.pallas_call(
        _rope_kernel,
        grid=(T // tt,),
        in_specs=[pl.BlockSpec((tt, 1), lambda i: (i, 0)), _full((1, LANES)), _full((1, LANES))],
        out_specs=[pl.BlockSpec((tt, LANES), lambda i: (i, 0))] * 2,
        out_shape=[jax.ShapeDtypeStruct((T, LANES), F32)] * 2,
        compiler_params=_cparams("arbitrary"),
        name="rope_tables",
    )(positions.reshape(T, 1), inv128, sgn128)


def _mix_in_kernel(x_ref, gmix_ref, win_ref, convw_ref, convb_ref, wgate_ref, bgate_ref,
                   lam_ref, gq_ref, wq_ref, gkv_ref, wkv_ref, vones_ref, c_ref, s_ref,
                   lru_ref, q_ref, k_ref, v_ref, halo_ref, hcarry_ref, *, ts):
    si = pl.program_id(1)

    @pl.when(si == 0)
    def _():
        halo_ref[0:8, :] = jnp.zeros((8, LRU_WIDTH), F32)
        hcarry_ref[...] = jnp.zeros_like(hcarry_ref)

    h = _rms(x_ref[...], gmix_ref[...]).astype(BF16)
    z = jnp.dot(h, win_ref[...], preferred_element_type=F32)

    halo_ref[8:, :] = z[:, C_XLRU:C_XLRU + LRU_WIDTH]
    xc = convb_ref[...] + jnp.zeros((ts, LRU_WIDTH), F32)
    for kk in range(CONV_W):
        off = 8 - (CONV_W - 1) + kk
        xc = xc + convw_ref[kk:kk + 1, :] * halo_ref[off:off + ts, :]
    halo_ref[0:8, :] = halo_ref[ts:ts + 8, :]

    gates = jnp.dot(xc.astype(BF16), wgate_ref[...], preferred_element_type=F32) + bgate_ref[...]
    r = jax.nn.sigmoid(gates[:, :LRU_WIDTH])
    ig = jax.nn.sigmoid(gates[:, LRU_WIDTH:])
    nlam = -lam_ref[...]
    softplus = jnp.maximum(nlam, 0.0) + jnp.log1p(jnp.exp(-jnp.abs(nlam)))
    log_a = -LRU_C * r * softplus
    a = jnp.exp(log_a)
    mult = jnp.sqrt(-jnp.tanh(log_a) * (a * a + 1.0))
    row = lax.broadcasted_iota(jnp.int32, (ts, LRU_WIDTH), 0)
    mult = jnp.where(row + si * ts == 0, 1.0, mult)
    b = mult * (ig * xc)

    d = 1
    while d < ts:
        keep = row >= d
        a_sh = jnp.where(keep, pltpu.roll(a, d, 0), 1.0)
        b_sh = jnp.where(keep, pltpu.roll(b, d, 0), 0.0)
        b = a * b_sh + b
        a = a * a_sh
        d *= 2
    hseq = b + a * hcarry_ref[...]
    hcarry_ref[...] = hseq[ts - 1:ts, :]
    lru_ref[...] = (hseq * jax.nn.gelu(z[:, C_GATE:C_GATE + LRU_WIDTH])).astype(BF16)

    cmul = c_ref[...]
    smul = s_ref[...]
    hq = _rms(z[:, C_CQ:C_CQ + Q_LORA], gq_ref[...]).astype(BF16)
    qq = jnp.dot(hq, wq_ref[...], preferred_element_type=F32)
    hkv = _rms(z[:, C_CKV:C_CKV + KV_LORA], gkv_ref[...]).astype(BF16)
    kv = jnp.dot(hkv, wkv_ref[...], preferred_element_type=F32)
    kr = z[:, C_KR:C_KR + HEAD_PAD] * cmul + z[:, C_KRS:C_KRS + HEAD_PAD] * smul
    nq = N_HEADS * HEAD_PAD
    for hh in range(N_HEADS):
        lo = hh * HEAD_PAD
        q_ref[:, lo:lo + HEAD_PAD] = (qq[:, lo:lo + HEAD_PAD] * cmul
                                      + qq[:, nq + lo:nq + lo + HEAD_PAD] * smul).astype(BF16)
        k_ref[:, lo:lo + HEAD_PAD] = (kv[:, lo:lo + HEAD_PAD] + kr).astype(BF16)
    v_ref[...] = (kv[:, nq:] + vones_ref[...]).astype(BF16)


def _mix_in(x2, B, S, p, ctab, stab, ts):
    T = B * S
    ns = S // ts
    row = lambda b, s: (b * ns + s, 0)
    kern = functools.partial(_mix_in_kernel, ts=ts)
    nq = N_HEADS * HEAD_PAD
    return pl.pallas_call(
        kern,
        grid=(B, ns),
        in_specs=[
            pl.BlockSpec((ts, D_MODEL), row),
            _full((1, D_MODEL)), _full((D_MODEL, IN_COLS_PAD)),
            _full((CONV_W, LRU_WIDTH)), _full((1, LRU_WIDTH)),
            _full((LRU_WIDTH, 2 * LRU_WIDTH)), _full((1, 2 * LRU_WIDTH)),
            _full((1, LRU_WIDTH)),
            _full((1, Q_LORA)), _full((Q_LORA, 2 * nq)),
            _full((1, KV_LORA)), _full((KV_LORA, 2 * nq)), _full((1, nq)),
            pl.BlockSpec((ts, LANES), row), pl.BlockSpec((ts, LANES), row),
        ],
        out_specs=[
            pl.BlockSpec((ts, LRU_WIDTH), row),
            pl.BlockSpec((ts, nq), row),
            pl.BlockSpec((ts, nq), row),
            pl.BlockSpec((ts, nq), row),
        ],
        out_shape=[
            jax.ShapeDtypeStruct((T, LRU_WIDTH), BF16),
            jax.ShapeDtypeStruct((T, nq), BF16),
            jax.ShapeDtypeStruct((T, nq), BF16),
            jax.ShapeDtypeStruct((T, nq), BF16),
        ],
        scratch_shapes=[pltpu.VMEM((ts + 8, LRU_WIDTH), F32), pltpu.VMEM((1, LRU_WIDTH), F32)],
        compiler_params=_cparams("arbitrary", "arbitrary"),
        name="mix_in",
    )(x2, p["g_mix"], p["w_in"], p["conv_w"], p["conv_b"], p["w_gate"], p["b_gate"], p["lam"],
      p["g_q"], p["w_q"], p["g_kv"], p["w_kv"], p["v_ones"], ctab, stab)


def _attn_kernel(q_ref, k_ref, v_ref, o_ref, *, tq, hpb):
    i = pl.program_id(2)
    row = lax.broadcasted_iota(jnp.int32, (tq, tq), 0)
    col = lax.broadcasted_iota(jnp.int32, (tq, tq), 1)

    def step(j, carry, masked):
        start = pl.multiple_of(j * tq, tq)
        new = []
        for hh in range(hpb):
            lo = hh * HEAD_PAD
            m, acc = carry[hh]
            kj = k_ref[pl.ds(start, tq), lo:lo + HEAD_PAD]
            vj = v_ref[pl.ds(start, tq), lo:lo + HEAD_PAD]
            s = lax.dot_general(q_ref[:, lo:lo + HEAD_PAD], kj, (((1,), (1,)), ((), ())),
                                preferred_element_type=F32)
            if masked:
                s = jnp.where(col <= row, s, -jnp.inf)
            m_new = jnp.maximum(m, jnp.max(s, axis=1, keepdims=True))
            alpha = jnp.exp2(m - m_new)
            pexp = jnp.exp2((s - m_new).astype(BF16))
            acc = alpha * acc + jnp.dot(pexp, vj, preferred_element_type=F32)
            new.append((m_new, acc))
        return tuple(new)

    init = tuple((jnp.full((tq, 1), -jnp.inf, F32), jnp.zeros((tq, HEAD_PAD), F32))
                 for _ in range(hpb))
    carry = lax.fori_loop(0, i, functools.partial(step, masked=False), init)
    accs = [acc for _, acc in step(i, carry, True)]
    lane = lax.broadcasted_iota(jnp.int32, (tq, HEAD_PAD), 1)
    for pr in range(hpb // 2):
        even, odd = accs[2 * pr], accs[2 * pr + 1]
        out = jnp.where(lane < V_HEAD, even / pltpu.roll(even, V_HEAD, 1),
                        odd / pltpu.roll(odd, V_HEAD, 1))
        o_ref[:, pr * HEAD_PAD:(pr + 1) * HEAD_PAD] = out.astype(BF16)


def _attention(q, k, v, B, S, tq, hpb):
    T = B * S
    nq = S // tq
    return pl.pallas_call(
        functools.partial(_attn_kernel, tq=tq, hpb=hpb),
        grid=(B, N_HEADS // hpb, nq),
        in_specs=[
            pl.BlockSpec((tq, hpb * HEAD_PAD), lambda b, h, i: (b * nq + i, h)),
            pl.BlockSpec((S, hpb * HEAD_PAD), lambda b, h, i: (b, h)),
            pl.BlockSpec((S, hpb * HEAD_PAD), lambda b, h, i: (b, h)),
        ],
        out_specs=pl.BlockSpec((tq, hpb * V_HEAD), lambda b, h, i: (b * nq + i, h)),
        out_shape=jax.ShapeDtypeStruct((T, N_HEADS * V_HEAD), BF16),
        compiler_params=_cparams("arbitrary", "arbitrary", "arbitrary"),
        name="mla_attention",
    )(q, k, v)


def _mix_out_kernel(x_ref, lru_ref, att_ref, wo_ref, gffn_ref, xo_ref, h2_ref):
    mixed = jnp.concatenate([lru_ref[...], att_ref[...]], axis=1)
    xn = x_ref[...] + jnp.dot(mixed, wo_ref[...], preferred_element_type=F32)
    xo_ref[...] = xn
    h2_ref[...] = _rms(xn, gffn_ref[...]).astype(BF16)


def _mix_out_router_kernel(x_ref, lru_ref, att_ref, wo_ref, gffn_ref, wr_ref,
                           xo_ref, h2_ref, gates_ref):
    mixed = jnp.concatenate([lru_ref[...], att_ref[...]], axis=1)
    xn = x_ref[...] + jnp.dot(mixed, wo_ref[...], preferred_element_type=F32)
    xo_ref[...] = xn
    h2 = _rms(xn, gffn_ref[...])
    h2_ref[...] = h2.astype(BF16)
    logits = jnp.dot(h2, wr_ref[...], preferred_element_type=F32,
                     precision=lax.Precision.HIGHEST)
    lane = lax.broadcasted_iota(jnp.int32, logits.shape, 1)
    logits = jnp.where(lane < N_EXPERTS, logits, -jnp.inf)
    m1 = jnp.max(logits, axis=1, keepdims=True)
    i1 = jnp.min(jnp.where(logits == m1, lane, LANES), axis=1, keepdims=True)
    rest = jnp.where(lane == i1, -jnp.inf, logits)
    m2 = jnp.max(rest, axis=1, keepdims=True)
    i2 = jnp.min(jnp.where(rest == m2, lane, LANES), axis=1, keepdims=True)
    e2 = jnp.exp(m2 - m1)
    den = 1.0 + e2
    gates_ref[...] = jnp.where(lane == i1, 1.0 / den, 0.0) + jnp.where(lane == i2, e2 / den, 0.0)


def _mix_out(x2, lru, att, p, tm, with_router):
    T = x2.shape[0]
    row = lambda i: (i, 0)
    in_specs = [
        pl.BlockSpec((tm, D_MODEL), row),
        pl.BlockSpec((tm, LRU_WIDTH), row),
        pl.BlockSpec((tm, N_HEADS * V_HEAD), row),
        _full((D_MODEL, D_MODEL)), _full((1, D_MODEL)),
    ]
    out_specs = [pl.BlockSpec((tm, D_MODEL), row), pl.BlockSpec((tm, D_MODEL), row)]
    out_shape = [jax.ShapeDtypeStruct((T, D_MODEL), F32), jax.ShapeDtypeStruct((T, D_MODEL), BF16)]
    args = [x2, lru, att, p["w_out"], p["g_ffn"]]
    if with_router:
        in_specs.append(_full((D_MODEL, LANES)))
        out_specs.append(pl.BlockSpec((tm, LANES), row))
        out_shape.append(jax.ShapeDtypeStruct((T, LANES), F32))
        args.append(p["w_router"])
    return pl.pallas_call(
        _mix_out_router_kernel if with_router else _mix_out_kernel,
        grid=(T // tm,),
        in_specs=in_specs, out_specs=out_specs, out_shape=out_shape,
        compiler_params=_cparams("arbitrary"),
        name="mix_out_router" if with_router else "mix_out",
    )(*args)


def _ffn_dense_kernel(x_ref, h2_ref, wg_ref, wu_ref, wd_ref, o_ref, *, fc):
    h2 = h2_ref[...]
    acc = x_ref[...]
    for c in range(D_FF_DENSE // fc):
        g = jnp.dot(h2, wg_ref[:, c * fc:(c + 1) * fc], preferred_element_type=F32)
        u = jnp.dot(h2, wu_ref[:, c * fc:(c + 1) * fc], preferred_element_type=F32)
        act = (g * jax.nn.sigmoid(g) * u).astype(BF16)
        acc = acc + jnp.dot(act, wd_ref[c * fc:(c + 1) * fc, :], preferred_element_type=F32)
    o_ref[...] = acc


def _ffn_dense(xn, h2, p, tm):
    T = xn.shape[0]
    row = lambda i: (i, 0)
    once = pl.Buffered(1)
    return pl.pallas_call(
        functools.partial(_ffn_dense_kernel, fc=256),
        grid=(T // tm,),
        in_specs=[
            pl.BlockSpec((tm, D_MODEL), row), pl.BlockSpec((tm, D_MODEL), row),
            pl.BlockSpec((D_MODEL, D_FF_DENSE), lambda i: (0, 0), pipeline_mode=once),
            pl.BlockSpec((D_MODEL, D_FF_DENSE), lambda i: (0, 0), pipeline_mode=once),
            pl.BlockSpec((D_FF_DENSE, D_MODEL), lambda i: (0, 0), pipeline_mode=once),
        ],
        out_specs=pl.BlockSpec((tm, D_MODEL), row),
        out_shape=jax.ShapeDtypeStruct((T, D_MODEL), F32),
        compiler_params=_cparams("arbitrary"),
        name="ffn_dense",
    )(xn, h2, p["w_gate_d"], p["w_up_d"], p["w_down_d"])


def _ffn_moe_kernel(x_ref, h2_ref, gates_ref, wg_ref, wu_ref, wd_ref, gfin_ref, o_ref, acc_ref,
                    *, fc, final_norm):
    e = pl.program_id(1)

    @pl.when(e == 0)
    def _():
        acc_ref[...] = x_ref[...]

    h2 = h2_ref[...]
    lane = lax.broadcasted_iota(jnp.int32, gates_ref.shape, 1)
    gate = jnp.sum(jnp.where(lane == e, gates_ref[...], 0.0), axis=1, keepdims=True)
    y = jnp.zeros(acc_ref.shape, F32)
    for c in range(D_FF_EXPERT // fc):
        g = jnp.dot(h2, wg_ref[0, :, c * fc:(c + 1) * fc], preferred_element_type=F32)
        u = jnp.dot(h2, wu_ref[0, :, c * fc:(c + 1) * fc], preferred_element_type=F32)
        act = (g * jax.nn.sigmoid(g) * u).astype(BF16)
        y = y + jnp.dot(act, wd_ref[0, c * fc:(c + 1) * fc, :], preferred_element_type=F32)
    acc_ref[...] += gate * y

    @pl.when(e == N_EXPERTS - 1)
    def _():
        out = acc_ref[...]
        if final_norm:
            out = _rms(out, gfin_ref[...])
        o_ref[...] = out


def _ffn_moe(xn, h2, gates, p, g_final, tm, final_norm):
    T = xn.shape[0]
    row = lambda i, e: (i, 0)
    return pl.pallas_call(
        functools.partial(_ffn_moe_kernel, fc=256, final_norm=final_norm),
        grid=(T // tm, N_EXPERTS),
        in_specs=[
            pl.BlockSpec((tm, D_MODEL), row), pl.BlockSpec((tm, D_MODEL), row),
            pl.BlockSpec((tm, LANES), row),
            pl.BlockSpec((1, D_MODEL, D_FF_EXPERT), lambda i, e: (e, 0, 0)),
            pl.BlockSpec((1, D_MODEL, D_FF_EXPERT), lambda i, e: (e, 0, 0)),
            pl.BlockSpec((1, D_FF_EXPERT, D_MODEL), lambda i, e: (e, 0, 0)),
            pl.BlockSpec((1, D_MODEL), lambda i, e: (0, 0)),
        ],
        out_specs=pl.BlockSpec((tm, D_MODEL), row),
        out_shape=jax.ShapeDtypeStruct((T, D_MODEL), F32),
        scratch_shapes=[pltpu.VMEM((tm, D_MODEL), F32)],
        compiler_params=_cparams("arbitrary", "arbitrary"),
        name="ffn_moe",
    )(xn, h2, gates, p["w_gate_e"], p["w_up_e"], p["w_down_e"], g_final)


def _rope_block(w_rope, swap):
    half = QK_ROPE // 2
    if swap:
        w_rope = jnp.concatenate([w_rope[:, half:], w_rope[:, :half]], axis=1)
    k = w_rope.shape[0]
    return jnp.concatenate([jnp.zeros((k, QK_NOPE), w_rope.dtype), w_rope,
                            jnp.zeros((k, HEAD_PAD - QK_NOPE - QK_ROPE), w_rope.dtype)], axis=1)


def _block_diag(w):
    eye = jnp.eye(LRU_BLOCKS, dtype=w.dtype)
    return jnp.einsum("ncd,nm->ncmd", w, eye).reshape(LRU_WIDTH, LRU_WIDTH)


def _layer_params(l, norm_mix, w_in, conv_w, conv_b, lru_wa, lru_ba, lru_wx, lru_bx, lru_lambda,
                  q_norm, w_uq, kv_norm, w_ukv, w_out, norm_ffn):
    scale = math.log2(math.e) / math.sqrt(QK_NOPE + QK_ROPE)
    wi = w_in[l]
    w_kr = wi[:, C_CKV + KV_LORA:]
    w_in_pad = jnp.concatenate([wi[:, :C_KR], _rope_block(w_kr, False), _rope_block(w_kr, True)],
                               axis=1).astype(BF16)
    wq = (w_uq[l] * scale).reshape(Q_LORA, N_HEADS, QK_NOPE + QK_ROPE)
    zpad = jnp.zeros((Q_LORA, N_HEADS, HEAD_PAD - QK_NOPE - QK_ROPE), F32)
    zn = jnp.zeros((Q_LORA, N_HEADS, QK_NOPE), F32)
    half = QK_ROPE // 2
    wq_a = jnp.concatenate([wq, zpad], axis=2).reshape(Q_LORA, N_HEADS * HEAD_PAD)
    wq_sw = jnp.concatenate([wq[:, :, QK_NOPE + half:], wq[:, :, QK_NOPE:QK_NOPE + half]], axis=2)
    wq_b = jnp.concatenate([zn, wq_sw, zpad], axis=2).reshape(Q_LORA, N_HEADS * HEAD_PAD)
    wkv = w_ukv[l].reshape(KV_LORA, N_HEADS, QK_NOPE + V_HEAD)
    wk = jnp.concatenate([wkv[:, :, :QK_NOPE], jnp.zeros((KV_LORA, N_HEADS, HEAD_PAD - QK_NOPE), F32)],
                         axis=2).reshape(KV_LORA, N_HEADS * HEAD_PAD)
    zv = jnp.zeros((KV_LORA, N_HEADS // 2, V_HEAD), F32)
    wv4 = wkv[:, :, QK_NOPE:].reshape(KV_LORA, N_HEADS // 2, 2, V_HEAD)
    wv = jnp.stack([wv4[:, :, 0], zv, zv, wv4[:, :, 1]], axis=2).reshape(KV_LORA, N_HEADS * HEAD_PAD)
    ones_pat = jnp.tile(jnp.concatenate([jnp.zeros((V_HEAD,), F32), jnp.ones((2 * V_HEAD,), F32),
                                         jnp.zeros((V_HEAD,), F32)]), N_HEADS // 2)[None, :]
    return {
        "g_mix": norm_mix[l][None, :],
        "w_in": w_in_pad,
        "conv_w": conv_w[l],
        "conv_b": conv_b[l][None, :],
        "w_gate": jnp.concatenate([_block_diag(lru_wa[l]), _block_diag(lru_wx[l])], axis=1).astype(BF16),
        "b_gate": jnp.concatenate([lru_ba[l].reshape(1, -1), lru_bx[l].reshape(1, -1)], axis=1),
        "lam": lru_lambda[l][None, :],
        "g_q": q_norm[l][None, :],
        "w_q": jnp.concatenate([wq_a, wq_b], axis=1).astype(BF16),
        "g_kv": kv_norm[l][None, :],
        "w_kv": jnp.concatenate([wk, wv], axis=1).astype(BF16),
        "v_ones": ones_pat,
        "w_out": w_out[l].astype(BF16),
        "g_ffn": norm_ffn[l][None, :],
    }


def kernel(x, positions, norm_mix, w_in, conv_w, conv_b, lru_wa, lru_ba, lru_wx, lru_bx, lru_lambda,
           q_norm, w_uq, kv_norm, w_ukv, w_out, norm_ffn, dense_w_gate, dense_w_up, dense_w_down,
           router_w, expert_w_gate, expert_w_up, expert_w_down, norm_final):
    B, S, _ = x.shape
    T = B * S
    depth = norm_mix.shape[0]
    ts = min(S, 256)
    tq = min(S, 512)
    tm = min(T, 512)
    ctab, stab = _rope_tables(positions)
    x2 = x.reshape(T, D_MODEL)
    for l in range(depth):
        p = _layer_params(l, norm_mix, w_in, conv_w, conv_b, lru_wa, lru_ba, lru_wx, lru_bx,
                          lru_lambda, q_norm, w_uq, kv_norm, w_ukv, w_out, norm_ffn)
        lru, q, k, v = _mix_in(x2, B, S, p, ctab, stab, ts)
        att = _attention(q, k, v, B, S, tq, hpb=4)
        j = l // 2
        last = l == depth - 1
        if l % 2 == 0:
            xn, h2 = _mix_out(x2, lru, att, p, tm, with_router=False)
            p["w_gate_d"] = dense_w_gate[j].astype(BF16)
            p["w_up_d"] = dense_w_up[j].astype(BF16)
            p["w_down_d"] = dense_w_down[j].astype(BF16)
            x2 = _ffn_dense(xn, h2, p, tm)
        else:
            p["w_router"] = jnp.pad(router_w[j], ((0, 0), (0, LANES - N_EXPERTS)))
            xn, h2, gates = _mix_out(x2, lru, att, p, tm, with_router=True)
            p["w_gate_e"] = expert_w_gate[j].astype(BF16)
            p["w_up_e"] = expert_w_up[j].astype(BF16)
            p["w_down_e"] = expert_w_down[j].astype(BF16)
            x2 = _ffn_moe(xn, h2, gates, p, norm_final[None, :], min(T, 1024), final_norm=last)
    return x2.reshape(B, S, D_MODEL)
```

```python
import functools
import math

import jax
import jax.numpy as jnp
from jax import lax
from jax.experimental import pallas as pl
from jax.experimental.pallas import tpu as pltpu

D_MODEL = 1024
EPS = 1e-6
LRU_WIDTH = 512
LRU_BLOCKS = 8
LRU_BLOCK_W = 64
LRU_C = 8.0
CONV_W = 4
N_HEADS = 8
QK_NOPE = 64
QK_ROPE = 32
V_HEAD = 64
Q_LORA = 256
KV_LORA = 128
ROPE_THETA = 10000.0
N_EXPERTS = 8
D_FF_DENSE = 2816
D_FF_EXPERT = 1792

LANES = 128
HEAD_PAD = 128
C_XLRU = 0
C_GATE = LRU_WIDTH
C_CQ = 2 * LRU_WIDTH
C_CKV = C_CQ + Q_LORA
C_KR = C_CKV + KV_LORA
C_KRS = C_KR + HEAD_PAD
IN_COLS_PAD = C_KRS + HEAD_PAD

E_PAD = 16
SEG_ALIGN = 16
XS_COLS = D_MODEL + LANES

VMEM_LIMIT = 56 * 1024 * 1024

F32 = jnp.float32
BF16 = jnp.bfloat16


def _cparams(*sem):
    return pltpu.CompilerParams(dimension_semantics=sem, vmem_limit_bytes=VMEM_LIMIT)


def _rms(x, g):
    return x * lax.rsqrt(jnp.mean(x * x, axis=-1, keepdims=True) + EPS) * g


def _full(shape):
    nd = len(shape)
    return pl.BlockSpec(shape, lambda *_: (0,) * nd)


def _rope_kernel(pos_ref, inv_ref, sgn_ref, c_ref, s_ref):
    ang = pos_ref[...].astype(F32) * inv_ref[...]
    lane = lax.broadcasted_iota(jnp.int32, ang.shape, 1)
    c_ref[...] = jnp.where(lane < QK_NOPE, 1.0,
                           jnp.where(lane < QK_NOPE + QK_ROPE, jnp.cos(ang), 0.0))
    s_ref[...] = jnp.sin(ang) * sgn_ref[...]


def _rope_tables(positions):
    T = positions.size
    tt = min(T, 2048)
    half = QK_ROPE // 2
    inv = 1.0 / (ROPE_THETA ** (jnp.arange(half, dtype=F32) / half))
    zeros = jnp.zeros((QK_NOPE,), F32)
    inv128 = jnp.concatenate([zeros, inv, inv, jnp.zeros((32,), F32)])[None, :]
    sgn128 = jnp.concatenate([zeros, -jnp.ones((half,), F32), jnp.ones((half,), F32),
                              jnp.zeros((32,), F32)])[None, :]
    return pl.pallas_call(
        _rope_kernel,
        grid=(T // tt,),
        in_specs=[pl.BlockSpec((tt, 1), lambda i: (i, 0)), _full((1, LANES)), _full((1, LANES))],
        out_specs=[pl.BlockSpec((tt, LANES), lambda i: (i, 0))] * 2,
        out_shape=[jax.ShapeDtypeStruct((T, LANES), F32)] * 2,
        compiler_params=_cparams("arbitrary"),
        name="rope_tables",
    )(positions.reshape(T, 1), inv128, sgn128)


def _mix_in_kernel(x_ref, gmix_ref, win_ref, convw_ref, convb_ref, wgate_ref, bgate_ref,
                   lam_ref, gq_ref, wq_ref, gkv_ref, wkv_ref, vones_ref, c_ref, s_ref,
                   lru_ref, q_ref, k_ref, v_ref, halo_ref, hcarry_ref, *, ts):
    si = pl.program_id(1)

    @pl.when(si == 0)
    def _():
        halo_ref[0:8, :] = jnp.zeros((8, LRU_WIDTH), F32)
        hcarry_ref[...] = jnp.zeros_like(hcarry_ref)

    h = _rms(x_ref[...], gmix_ref[...]).astype(BF16)
    z = jnp.dot(h, win_ref[...], preferred_element_type=F32)

    halo_ref[8:, :] = z[:, C_XLRU:C_XLRU + LRU_WIDTH]
    xc = convb_ref[...] + jnp.zeros((ts, LRU_WIDTH), F32)
    for kk in range(CONV_W):
        off = 8 - (CONV_W - 1) + kk
        xc = xc + convw_ref[kk:kk + 1, :] * halo_ref[off:off + ts, :]
    halo_ref[0:8, :] = halo_ref[ts:ts + 8, :]

    gates = jnp.dot(xc.astype(BF16), wgate_ref[...], preferred_element_type=F32) + bgate_ref[...]
    r = jax.nn.sigmoid(gates[:, :LRU_WIDTH])
    ig = jax.nn.sigmoid(gates[:, LRU_WIDTH:])
    nlam = -lam_ref[...]
    softplus = jnp.maximum(nlam, 0.0) + jnp.log1p(jnp.exp(-jnp.abs(nlam)))
    log_a = -LRU_C * r * softplus
    a = jnp.exp(log_a)
    mult = jnp.sqrt(-jnp.tanh(log_a) * (a * a + 1.0))
    row = lax.broadcasted_iota(jnp.int32, (ts, LRU_WIDTH), 0)
    mult = jnp.where(row + si * ts == 0, 1.0, mult)
    b = mult * (ig * xc)

    d = 1
    while d < ts:
        keep = row >= d
        a_sh = jnp.where(keep, pltpu.roll(a, d, 0), 1.0)
        b_sh = jnp.where(keep, pltpu.roll(b, d, 0), 0.0)
        b = a * b_sh + b
        a = a * a_sh
        d *= 2
    hseq = b + a * hcarry_ref[...]
    hcarry_ref[...] = hseq[ts - 1:ts, :]
    lru_ref[...] = (hseq * jax.nn.gelu(z[:, C_GATE:C_GATE + LRU_WIDTH])).astype(BF16)

    cmul = c_ref[...]
    smul = s_ref[...]
    hq = _rms(z[:, C_CQ:C_CQ + Q_LORA], gq_ref[...]).astype(BF16)
    qq = jnp.dot(hq, wq_ref[...], preferred_element_type=F32)
    hkv = _rms(z[:, C_CKV:C_CKV + KV_LORA], gkv_ref[...]).astype(BF16)
    kv = jnp.dot(hkv, wkv_ref[...], preferred_element_type=F32)
    kr = z[:, C_KR:C_KR + HEAD_PAD] * cmul + z[:, C_KRS:C_KRS + HEAD_PAD] * smul
    nq = N_HEADS * HEAD_PAD
    for hh in range(N_HEADS):
        lo = hh * HEAD_PAD
        q_ref[:, lo:lo + HEAD_PAD] = (qq[:, lo:lo + HEAD_PAD] * cmul
                                      + qq[:, nq + lo:nq + lo + HEAD_PAD] * smul).astype(BF16)
        k_ref[:, lo:lo + HEAD_PAD] = (kv[:, lo:lo + HEAD_PAD] + kr).astype(BF16)
    v_ref[...] = (kv[:, nq:] + vones_ref[...]).astype(BF16)


def _mix_in(x2, B, S, p, ctab, stab, ts):
    T = B * S
    ns = S // ts
    row = lambda b, s: (b * ns + s, 0)
    kern = functools.partial(_mix_in_kernel, ts=ts)
    nq = N_HEADS * HEAD_PAD
    return pl.pallas_call(
        kern,
        grid=(B, ns),
        in_specs=[
            pl.BlockSpec((ts, D_MODEL), row),
            _full((1, D_MODEL)), _full((D_MODEL, IN_COLS_PAD)),
            _full((CONV_W, LRU_WIDTH)), _full((1, LRU_WIDTH)),
            _full((LRU_WIDTH, 2 * LRU_WIDTH)), _full((1, 2 * LRU_WIDTH)),
            _full((1, LRU_WIDTH)),
            _full((1, Q_LORA)), _full((Q_LORA, 2 * nq)),
            _full((1, KV_LORA)), _full((KV_LORA, 2 * nq)), _full((1, nq)),
            pl.BlockSpec((ts, LANES), row), pl.BlockSpec((ts, LANES), row),
        ],
        out_specs=[
            pl.BlockSpec((ts, LRU_WIDTH), row),
            pl.BlockSpec((ts, nq), row),
            pl.BlockSpec((ts, nq), row),
            pl.BlockSpec((ts, nq), row),
        ],
        out_shape=[
            jax.ShapeDtypeStruct((T, LRU_WIDTH), BF16),
            jax.ShapeDtypeStruct((T, nq), BF16),
            jax.ShapeDtypeStruct((T, nq), BF16),
            jax.ShapeDtypeStruct((T, nq), BF16),
        ],
        scratch_shapes=[pltpu.VMEM((ts + 8, LRU_WIDTH), F32), pltpu.VMEM((1, LRU_WIDTH), F32)],
        compiler_params=_cparams("arbitrary", "arbitrary"),
        name="mix_in",
    )(x2, p["g_mix"], p["w_in"], p["conv_w"], p["conv_b"], p["w_gate"], p["b_gate"], p["lam"],
      p["g_q"], p["w_q"], p["g_kv"], p["w_kv"], p["v_ones"], ctab, stab)


def _attn_kernel(q_ref, k_ref, v_ref, o_ref, *, tq, hpb):
    i = pl.program_id(2)
    row = lax.broadcasted_iota(jnp.int32, (tq, tq), 0)
    col = lax.broadcasted_iota(jnp.int32, (tq, tq), 1)

    def step(j, carry, masked):
        start = pl.multiple_of(j * tq, tq)
        new = []
        for hh in range(hpb):
            lo = hh * HEAD_PAD
            m, acc = carry[hh]
            kj = k_ref[pl.ds(start, tq), lo:lo + HEAD_PAD]
            vj = v_ref[pl.ds(start, tq), lo:lo + HEAD_PAD]
            s = lax.dot_general(q_ref[:, lo:lo + HEAD_PAD], kj, (((1,), (1,)), ((), ())),
                                preferred_element_type=F32)
            if masked:
                s = jnp.where(col <= row, s, -jnp.inf)
            m_new = jnp.maximum(m, jnp.max(s, axis=1, keepdims=True))
            alpha = jnp.exp2(m - m_new)
            pexp = jnp.exp2((s - m_new).astype(BF16))
            acc = alpha * acc + jnp.dot(pexp, vj, preferred_element_type=F32)
            new.append((m_new, acc))
        return tuple(new)

    init = tuple((jnp.full((tq, 1), -jnp.inf, F32), jnp.zeros((tq, HEAD_PAD), F32))
                 for _ in range(hpb))
    carry = lax.fori_loop(0, i, functools.partial(step, masked=False), init)
    accs = [acc for _, acc in step(i, carry, True)]
    lane = lax.broadcasted_iota(jnp.int32, (tq, HEAD_PAD), 1)
    for pr in range(hpb // 2):
        even, odd = accs[2 * pr], accs[2 * pr + 1]
        out = jnp.where(lane < V_HEAD, even / pltpu.roll(even, V_HEAD, 1),
                        odd / pltpu.roll(odd, V_HEAD, 1))
        o_ref[:, pr * HEAD_PAD:(pr + 1) * HEAD_PAD] = out.astype(BF16)


def _attention(q, k, v, B, S, tq, hpb):
    T = B * S
    nq = S // tq
    return pl.pallas_call(
        functools.partial(_attn_kernel, tq=tq, hpb=hpb),
        grid=(B, N_HEADS // hpb, nq),
        in_specs=[
            pl.BlockSpec((tq, hpb * HEAD_PAD), lambda b, h, i: (b * nq + i, h)),
            pl.BlockSpec((S, hpb * HEAD_PAD), lambda b, h, i: (b, h)),
            pl.BlockSpec((S, hpb * HEAD_PAD), lambda b, h, i: (b, h)),
        ],
        out_specs=pl.BlockSpec((tq, hpb * V_HEAD), lambda b, h, i: (b * nq + i, h)),
        out_shape=jax.ShapeDtypeStruct((T, N_HEADS * V_HEAD), BF16),
        compiler_params=_cparams("arbitrary", "arbitrary", "arbitrary"),
        name="mla_attention",
    )(q, k, v)


def _mix_out_kernel(x_ref, lru_ref, att_ref, wo_ref, gffn_ref, xo_ref, h2_ref):
    mixed = jnp.concatenate([lru_ref[...], att_ref[...]], axis=1)
    xn = x_ref[...] + jnp.dot(mixed, wo_ref[...], preferred_element_type=F32)
    xo_ref[...] = xn
    h2_ref[...] = _rms(xn, gffn_ref[...]).astype(BF16)


def _mix_out(x2, lru, att, p, tm):
    T = x2.shape[0]
    row = lambda i: (i, 0)
    return pl.pallas_call(
        _mix_out_kernel,
        grid=(T // tm,),
        in_specs=[
            pl.BlockSpec((tm, D_MODEL), row),
            pl.BlockSpec((tm, LRU_WIDTH), row),
            pl.BlockSpec((tm, N_HEADS * V_HEAD), row),
            _full((D_MODEL, D_MODEL)), _full((1, D_MODEL)),
        ],
        out_specs=[pl.BlockSpec((tm, D_MODEL), row), pl.BlockSpec((tm, D_MODEL), row)],
        out_shape=[jax.ShapeDtypeStruct((T, D_MODEL), F32), jax.ShapeDtypeStruct((T, D_MODEL), BF16)],
        compiler_params=_cparams("arbitrary"),
        name="mix_out",
    )(x2, lru, att, p["w_out"], p["g_ffn"])


def _split3(w):
    hi = w.astype(BF16)
    r1 = w - hi.astype(F32)
    mid = r1.astype(BF16)
    lo = (r1 - mid.astype(F32)).astype(BF16)
    return hi, mid, lo


def _mix_out_route_kernel(x_ref, lru_ref, att_ref, wo_ref, gffn_ref, wrh_ref, wrl_ref,
                          xo_ref, xs_ref, tab_ref, seg_ref, cnt_ref, *, tm):
    mixed = jnp.concatenate([lru_ref[...], att_ref[...]], axis=1)
    xn = x_ref[...] + jnp.dot(mixed, wo_ref[...], preferred_element_type=F32)
    xo_ref[...] = xn
    h2 = _rms(xn, gffn_ref[...])
    h2_hi = h2.astype(BF16)
    h2_lo = (h2 - h2_hi.astype(F32)).astype(BF16)

    nt = (((1,), (1,)), ((), ()))
    logits = (lax.dot_general(wrh_ref[...], h2_hi, nt, preferred_element_type=F32)
              + lax.dot_general(wrh_ref[...], h2_lo, nt, preferred_element_type=F32)
              + lax.dot_general(wrl_ref[...], h2_hi, nt, preferred_element_type=F32))
    eidx = lax.broadcasted_iota(jnp.int32, (E_PAD, tm), 0)
    logits = jnp.where(eidx < N_EXPERTS, logits, -jnp.inf)
    m1 = jnp.max(logits, axis=0, keepdims=True)
    i1 = jnp.min(jnp.where(logits == m1, eidx, E_PAD), axis=0, keepdims=True)
    is0 = eidx == i1
    rest = jnp.where(is0, -jnp.inf, logits)
    m2 = jnp.max(rest, axis=0, keepdims=True)
    i2 = jnp.min(jnp.where(rest == m2, eidx, E_PAD), axis=0, keepdims=True)
    is1 = eidx == i2
    e2 = jnp.exp(m2 - m1)
    den = 1.0 + e2
    w0 = 1.0 / den
    w1 = e2 / den

    sel = jnp.where(is0, 1.0, jnp.where(is1, 1.0, 0.0))
    tr = lax.broadcasted_iota(jnp.int32, (tm, tm), 0)
    tc = lax.broadcasted_iota(jnp.int32, (tm, tm), 1)
    before = jnp.where(tr < tc, 1.0, 0.0).astype(BF16)
    rank = jnp.dot(sel.astype(BF16), before, preferred_element_type=F32).astype(jnp.int32)
    cnt = jnp.sum(sel, axis=1, keepdims=True).astype(jnp.int32)
    cpad = jnp.broadcast_to(((cnt + (SEG_ALIGN - 1)) // SEG_ALIGN) * SEG_ALIGN, (E_PAD, tm))
    inc = cpad
    d = 1
    while d < E_PAD:
        inc = inc + jnp.where(eidx >= d, pltpu.roll(inc, d, 0), 0)
        d *= 2
    segstart = inc - cpad
    dest = segstart + rank
    d0 = jnp.sum(jnp.where(is0, dest, 0), axis=0, keepdims=True)
    d1 = jnp.sum(jnp.where(is1, dest, 0), axis=0, keepdims=True)
    seg_ref[0] = segstart[:, :LANES]
    cnt_ref[0] = cpad[:, :LANES]

    srow = lax.broadcasted_iota(jnp.int32, (LANES, tm), 0)
    stack = jnp.where(srow == 0, w0, jnp.where(srow == 1, w1, jnp.where(
        srow == 2, d0.astype(F32), jnp.where(srow == 3, d1.astype(F32), 0.0))))
    tab = stack.T
    tab_ref[...] = tab

    rr = lax.broadcasted_iota(jnp.int32, (xs_ref.shape[1], tm), 0)
    p0 = jnp.where(rr == d0, 1.0, 0.0).astype(BF16)
    p1 = jnp.where(rr == d1, 1.0, 0.0).astype(BF16)
    xs_ref[0, :, :D_MODEL] = jnp.dot(p0 + p1, h2_hi, preferred_element_type=F32).astype(BF16)
    lane = lax.broadcasted_iota(jnp.int32, (tm, LANES), 1)

    def gate_cols(col):
        hi, mid, lo = (v.astype(F32) for v in
                       _split3(jnp.broadcast_to(tab[:, col:col + 1], (tm, LANES))))
        return jnp.where(lane == 0, hi, jnp.where(lane == 1, mid, jnp.where(
            lane == 2, lo, 0.0))).astype(BF16)

    grows = (jnp.dot(p0, gate_cols(0), preferred_element_type=F32)
             + jnp.dot(p1, gate_cols(1), preferred_element_type=F32))
    xs_ref[0, :, D_MODEL:] = grows.astype(BF16)


def _mix_out_route(x2, lru, att, p, tm):
    T = x2.shape[0]
    nt = T // tm
    rows = 2 * tm + N_EXPERTS * SEG_ALIGN
    row = lambda i: (i, 0)
    blk3 = lambda i: (i, 0, 0)
    return pl.pallas_call(
        functools.partial(_mix_out_route_kernel, tm=tm),
        grid=(nt,),
        in_specs=[
            pl.BlockSpec((tm, D_MODEL), row),
            pl.BlockSpec((tm, LRU_WIDTH), row),
            pl.BlockSpec((tm, N_HEADS * V_HEAD), row),
            _full((D_MODEL, D_MODEL)), _full((1, D_MODEL)),
            _full((E_PAD, D_MODEL)), _full((E_PAD, D_MODEL)),
        ],
        out_specs=[
            pl.BlockSpec((tm, D_MODEL), row),
            pl.BlockSpec((1, rows, XS_COLS), blk3),
            pl.BlockSpec((tm, LANES), row),
            pl.BlockSpec((1, E_PAD, LANES), blk3),
            pl.BlockSpec((1, E_PAD, LANES), blk3),
        ],
        out_shape=[
            jax.ShapeDtypeStruct((T, D_MODEL), F32),
            jax.ShapeDtypeStruct((nt, rows, XS_COLS), BF16),
            jax.ShapeDtypeStruct((T, LANES), F32),
            jax.ShapeDtypeStruct((nt, E_PAD, LANES), jnp.int32),
            jax.ShapeDtypeStruct((nt, E_PAD, LANES), jnp.int32),
        ],
        compiler_params=_cparams("arbitrary"),
        name="mix_out_route",
    )(x2, lru, att, p["w_out"], p["g_ffn"], p["w_router_hi"], p["w_router_lo"])


def _ffn_dense_kernel(x_ref, h2_ref, wg_ref, wu_ref, wd_ref, o_ref, *, fc):
    h2 = h2_ref[...]
    acc = x_ref[...]
    for c in range(D_FF_DENSE // fc):
        g = jnp.dot(h2, wg_ref[:, c * fc:(c + 1) * fc], preferred_element_type=F32)
        u = jnp.dot(h2, wu_ref[:, c * fc:(c + 1) * fc], preferred_element_type=F32)
        act = (g * jax.nn.sigmoid(g) * u).astype(BF16)
        acc = acc + jnp.dot(act, wd_ref[c * fc:(c + 1) * fc, :], preferred_element_type=F32)
    o_ref[...] = acc


def _ffn_dense(xn, h2, p, tm):
    T = xn.shape[0]
    row = lambda i: (i, 0)
    once = pl.Buffered(1)
    return pl.pallas_call(
        functools.partial(_ffn_dense_kernel, fc=256),
        grid=(T // tm,),
        in_specs=[
            pl.BlockSpec((tm, D_MODEL), row), pl.BlockSpec((tm, D_MODEL), row),
            pl.BlockSpec((D_MODEL, D_FF_DENSE), lambda i: (0, 0), pipeline_mode=once),
            pl.BlockSpec((D_MODEL, D_FF_DENSE), lambda i: (0, 0), pipeline_mode=once),
            pl.BlockSpec((D_FF_DENSE, D_MODEL), lambda i: (0, 0), pipeline_mode=once),
        ],
        out_specs=pl.BlockSpec((tm, D_MODEL), row),
        out_shape=jax.ShapeDtypeStruct((T, D_MODEL), F32),
        compiler_params=_cparams("arbitrary"),
        name="ffn_dense",
    )(xn, h2, p["w_gate_d"], p["w_up_d"], p["w_down_d"])


SEG_BITS = (512, 256, 128, 64, 32, 16)


def _moe_expert_kernel(seg_ref, cnt_ref, xs_hbm, wg_ref, wu_ref, wd_ref, yin_hbm, y_hbm,
                       lhs_ref, ost_ref, in_sem, out_sem, *, gj, ch, n_groups):
    del yin_hbm
    k = pl.program_id(0) * n_groups + pl.program_id(1)
    nsteps = N_EXPERTS * n_groups
    slot = lax.rem(k, 2)

    def for_each_piece(step, fn):
        e_s = step // n_groups
        g_s = lax.rem(step, n_groups)
        off = jnp.int32(0)
        for s in range(gj):
            j = g_s * gj + s
            start = seg_ref[j * N_EXPERTS + e_s]
            c = cnt_ref[j * N_EXPERTS + e_s]
            pos = jnp.int32(0)
            for b in SEG_BITS:
                has = (c & b) != 0

                @pl.when(has)
                def _(j=j, src=start + pos, dst=off + pos, b=b):
                    fn(j, pl.multiple_of(src, SEG_ALIGN), pl.multiple_of(dst, SEG_ALIGN), b)

                pos = pos + jnp.where(has, b, 0)
            off = off + c
        return off

    def in_copy(sl):
        def fn(j, src, dst, b):
            return pltpu.make_async_copy(xs_hbm.at[j, pl.ds(src, b), :],
                                         lhs_ref.at[sl, pl.ds(dst, b), :], in_sem.at[sl])
        return fn

    def out_copy(sl):
        def fn(j, src, dst, b):
            return pltpu.make_async_copy(ost_ref.at[sl, pl.ds(dst, b), :],
                                         y_hbm.at[j, pl.ds(src, b), :], out_sem.at[sl])
        return fn

    def start_all(step, mk):
        return for_each_piece(step, lambda *a: mk(*a).start())

    def wait_all(step, mk):
        return for_each_piece(step, lambda *a: mk(*a).wait())

    @pl.when(k == 0)
    def _():
        lhs_ref[...] = jnp.zeros_like(lhs_ref)
        start_all(k, in_copy(slot))

    n_rows = wait_all(k, in_copy(slot))

    @pl.when(k + 1 < nsteps)
    def _():
        start_all(k + 1, in_copy(1 - slot))

    @pl.when(k >= 2)
    def _():
        wait_all(k - 2, out_copy(slot))

    def chunk(ci, carry):
        r0 = pl.multiple_of(ci * ch, ch)
        rows = lhs_ref[slot, pl.ds(r0, ch), :]
        xrow = rows[:, :D_MODEL]
        gate = (rows[:, D_MODEL:D_MODEL + 1].astype(F32) + rows[:, D_MODEL + 1:D_MODEL + 2].astype(F32)
                + rows[:, D_MODEL + 2:D_MODEL + 3].astype(F32))
        g = jnp.dot(xrow, wg_ref[0], preferred_element_type=F32)
        u = jnp.dot(xrow, wu_ref[0], preferred_element_type=F32)
        act = (g * jax.nn.sigmoid(g) * u).astype(BF16)
        y = jnp.dot(act, wd_ref[0], preferred_element_type=F32) * gate
        ost_ref[slot, pl.ds(r0, ch), :] = y.astype(BF16)
        return carry

    lax.fori_loop(0, (n_rows + ch - 1) // ch, chunk, 0)
    start_all(k, out_copy(slot))

    @pl.when(k == nsteps - 1)
    def _():
        if nsteps > 1:
            wait_all(k - 1, out_copy(1 - slot))
        wait_all(k, out_copy(slot))


def _moe_experts(xs, seg, cnt, p, gj, ch):
    nt, rows, _ = xs.shape
    n_groups = nt // gj
    cap = gj * (rows - N_EXPERTS * SEG_ALIGN) // 2 + ch
    wmap = lambda e, g, seg_r, cnt_r: (e, 0, 0)
    grid_spec = pltpu.PrefetchScalarGridSpec(
        num_scalar_prefetch=2,
        grid=(N_EXPERTS, n_groups),
        in_specs=[
            pl.BlockSpec(memory_space=pl.ANY),
            pl.BlockSpec((1, D_MODEL, D_FF_EXPERT), wmap),
            pl.BlockSpec((1, D_MODEL, D_FF_EXPERT), wmap),
            pl.BlockSpec((1, D_FF_EXPERT, D_MODEL), wmap),
            pl.BlockSpec(memory_space=pl.ANY),
        ],
        out_specs=pl.BlockSpec(memory_space=pl.ANY),
        scratch_shapes=[
            pltpu.VMEM((2, cap, XS_COLS), BF16),
            pltpu.VMEM((2, cap, D_MODEL), BF16),
            pltpu.SemaphoreType.DMA((2,)),
            pltpu.SemaphoreType.DMA((2,)),
        ],
    )
    y0 = jnp.zeros((nt, rows, D_MODEL), BF16)
    return pl.pallas_call(
        functools.partial(_moe_expert_kernel, gj=gj, ch=ch, n_groups=n_groups),
        grid_spec=grid_spec,
        out_shape=jax.ShapeDtypeStruct((nt, rows, D_MODEL), BF16),
        input_output_aliases={6: 0},
        compiler_params=_cparams("arbitrary", "arbitrary"),
        name="moe_experts",
    )(seg, cnt, xs, p["w_gate_e"], p["w_up_e"], p["w_down_e"], y0)


def _moe_combine_kernel(x_ref, y_ref, tab_ref, gfin_ref, o_ref, *, final_norm):
    tm = x_ref.shape[0]
    rows = y_ref.shape[1]
    d0 = tab_ref[:, 2:3].astype(jnp.int32)
    d1 = tab_ref[:, 3:4].astype(jnp.int32)
    rr = lax.broadcasted_iota(jnp.int32, (tm, rows), 1)
    pt = jnp.where(rr == d0, 1.0, jnp.where(rr == d1, 1.0, 0.0)).astype(BF16)
    out = x_ref[...] + jnp.dot(pt, y_ref[0], preferred_element_type=F32)
    if final_norm:
        out = _rms(out, gfin_ref[...])
    o_ref[...] = out


def _moe_combine(xn, y, tab, g_final, tm, final_norm):
    T = xn.shape[0]
    rows = y.shape[1]
    row = lambda i: (i, 0)
    return pl.pallas_call(
        functools.partial(_moe_combine_kernel, final_norm=final_norm),
        grid=(T // tm,),
        in_specs=[
            pl.BlockSpec((tm, D_MODEL), row),
            pl.BlockSpec((1, rows, D_MODEL), lambda i: (i, 0, 0)),
            pl.BlockSpec((tm, LANES), row),
            _full((1, D_MODEL)),
        ],
        out_specs=pl.BlockSpec((tm, D_MODEL), row),
        out_shape=jax.ShapeDtypeStruct((T, D_MODEL), F32),
        compiler_params=_cparams("arbitrary"),
        name="moe_combine",
    )(xn, y, tab, g_final)


def _rope_block(w_rope, swap):
    half = QK_ROPE // 2
    if swap:
        w_rope = jnp.concatenate([w_rope[:, half:], w_rope[:, :half]], axis=1)
    k = w_rope.shape[0]
    return jnp.concatenate([jnp.zeros((k, QK_NOPE), w_rope.dtype), w_rope,
                            jnp.zeros((k, HEAD_PAD - QK_NOPE - QK_ROPE), w_rope.dtype)], axis=1)


def _block_diag(w):
    eye = jnp.eye(LRU_BLOCKS, dtype=w.dtype)
    return jnp.einsum("ncd,nm->ncmd", w, eye).reshape(LRU_WIDTH, LRU_WIDTH)


def _layer_params(l, norm_mix, w_in, conv_w, conv_b, lru_wa, lru_ba, lru_wx, lru_bx, lru_lambda,
                  q_norm, w_uq, kv_norm, w_ukv, w_out, norm_ffn):
    scale = math.log2(math.e) / math.sqrt(QK_NOPE + QK_ROPE)
    wi = w_in[l]
    w_kr = wi[:, C_CKV + KV_LORA:]
    w_in_pad = jnp.concatenate([wi[:, :C_KR], _rope_block(w_kr, False), _rope_block(w_kr, True)],
                               axis=1).astype(BF16)
    wq = (w_uq[l] * scale).reshape(Q_LORA, N_HEADS, QK_NOPE + QK_ROPE)
    zpad = jnp.zeros((Q_LORA, N_HEADS, HEAD_PAD - QK_NOPE - QK_ROPE), F32)
    zn = jnp.zeros((Q_LORA, N_HEADS, QK_NOPE), F32)
    half = QK_ROPE // 2
    wq_a = jnp.concatenate([wq, zpad], axis=2).reshape(Q_LORA, N_HEADS * HEAD_PAD)
    wq_sw = jnp.concatenate([wq[:, :, QK_NOPE + half:], wq[:, :, QK_NOPE:QK_NOPE + half]], axis=2)
    wq_b = jnp.concatenate([zn, wq_sw, zpad], axis=2).reshape(Q_LORA, N_HEADS * HEAD_PAD)
    wkv = w_ukv[l].reshape(KV_LORA, N_HEADS, QK_NOPE + V_HEAD)
    wk = jnp.concatenate([wkv[:, :, :QK_NOPE], jnp.zeros((KV_LORA, N_HEADS, HEAD_PAD - QK_NOPE), F32)],
                         axis=2).reshape(KV_LORA, N_HEADS * HEAD_PAD)
    zv = jnp.zeros((KV_LORA, N_HEADS // 2, V_HEAD), F32)
    wv4 = wkv[:, :, QK_NOPE:].reshape(KV_LORA, N_HEADS // 2, 2, V_HEAD)
    wv = jnp.stack([wv4[:, :, 0], zv, zv, wv4[:, :, 1]], axis=2).reshape(KV_LORA, N_HEADS * HEAD_PAD)
    ones_pat = jnp.tile(jnp.concatenate([jnp.zeros((V_HEAD,), F32), jnp.ones((2 * V_HEAD,), F32),
                                         jnp.zeros((V_HEAD,), F32)]), N_HEADS // 2)[None, :]
    return {
        "g_mix": norm_mix[l][None, :],
        "w_in": w_in_pad,
        "conv_w": conv_w[l],
        "conv_b": conv_b[l][None, :],
        "w_gate": jnp.concatenate([_block_diag(lru_wa[l]), _block_diag(lru_wx[l])], axis=1).astype(BF16),
        "b_gate": jnp.concatenate([lru_ba[l].reshape(1, -1), lru_bx[l].reshape(1, -1)], axis=1),
        "lam": lru_lambda[l][None, :],
        "g_q": q_norm[l][None, :],
        "w_q": jnp.concatenate([wq_a, wq_b], axis=1).astype(BF16),
        "g_kv": kv_norm[l][None, :],
        "w_kv": jnp.concatenate([wk, wv], axis=1).astype(BF16),
        "v_ones": ones_pat,
        "w_out": w_out[l].astype(BF16),
        "g_ffn": norm_ffn[l][None, :],
    }


def kernel(x, positions, norm_mix, w_in, conv_w, conv_b, lru_wa, lru_ba, lru_wx, lru_bx, lru_lambda,
           q_norm, w_uq, kv_norm, w_ukv, w_out, norm_ffn, dense_w_gate, dense_w_up, dense_w_down,
           router_w, expert_w_gate, expert_w_up, expert_w_down, norm_final):
    B, S, _ = x.shape
    T = B * S
    depth = norm_mix.shape[0]
    ts = min(S, 256)
    tq = min(S, 512)
    tm = min(T, 512)
    ctab, stab = _rope_tables(positions)
    x2 = x.reshape(T, D_MODEL)
    for l in range(depth):
        p = _layer_params(l, norm_mix, w_in, conv_w, conv_b, lru_wa, lru_ba, lru_wx, lru_bx,
                          lru_lambda, q_norm, w_uq, kv_norm, w_ukv, w_out, norm_ffn)
        lru, q, k, v = _mix_in(x2, B, S, p, ctab, stab, ts)
        att = _attention(q, k, v, B, S, tq, hpb=4)
        j = l // 2
        last = l == depth - 1
        if l % 2 == 0:
            xn, h2 = _mix_out(x2, lru, att, p, tm)
            p["w_gate_d"] = dense_w_gate[j].astype(BF16)
            p["w_up_d"] = dense_w_up[j].astype(BF16)
            p["w_down_d"] = dense_w_down[j].astype(BF16)
            x2 = _ffn_dense(xn, h2, p, tm)
        else:
            wr = jnp.pad(router_w[j].T, ((0, E_PAD - N_EXPERTS), (0, 0)))
            p["w_router_hi"] = wr.astype(BF16)
            p["w_router_lo"] = (wr - p["w_router_hi"].astype(F32)).astype(BF16)
            xn, xs, tab, seg, cnt = _mix_out_route(x2, lru, att, p, tm)
            p["w_gate_e"] = expert_w_gate[j].astype(BF16)
            p["w_up_e"] = expert_w_up[j].astype(BF16)
            p["w_down_e"] = expert_w_down[j].astype(BF16)
            seg = seg[:, :N_EXPERTS, 0].reshape(-1)
            cnt = cnt[:, :N_EXPERTS, 0].reshape(-1)
            y = _moe_experts(xs, seg, cnt, p, gj=min(4, T // tm), ch=128)
            x2 = _moe_combine(xn, y, tab, norm_final[None, :], tm, final_norm=last)
    return x2.reshape(B, S, D_MODEL)
```

```python
import functools
import math

import jax
import jax.numpy as jnp
from jax import lax
from jax.experimental import pallas as pl
from jax.experimental.pallas import tpu as pltpu

D_MODEL = 1024
EPS = 1e-6
LRU_WIDTH = 512
LRU_BLOCKS = 8
LRU_BLOCK_W = 64
LRU_C = 8.0
CONV_W = 4
N_HEADS = 8
QK_NOPE = 64
QK_ROPE = 32
V_HEAD = 64
Q_LORA = 256
KV_LORA = 128
ROPE_THETA = 10000.0
N_EXPERTS = 8
D_FF_DENSE = 2816
D_FF_EXPERT = 1792

LANES = 128
HEAD_PAD = 128
C_XLRU = 0
C_GATE = LRU_WIDTH
C_CQ = 2 * LRU_WIDTH
C_CKV = C_CQ + Q_LORA
C_KR = C_CKV + KV_LORA
C_KRS = C_KR + HEAD_PAD
IN_COLS_PAD = C_KRS + HEAD_PAD

E_PAD = 16
SEG_ALIGN = 16
XS_COLS = D_MODEL + LANES

VMEM_LIMIT = 56 * 1024 * 1024

F32 = jnp.float32
BF16 = jnp.bfloat16


def _cparams(*sem):
    return pltpu.CompilerParams(dimension_semantics=sem, vmem_limit_bytes=VMEM_LIMIT)


def _rms(x, g):
    return x * lax.rsqrt(jnp.mean(x * x, axis=-1, keepdims=True) + EPS) * g


def _full(shape):
    nd = len(shape)
    return pl.BlockSpec(shape, lambda *_: (0,) * nd)


def _rope_kernel(pos_ref, inv_ref, sgn_ref, c_ref, s_ref):
    ang = pos_ref[...].astype(F32) * inv_ref[...]
    lane = lax.broadcasted_iota(jnp.int32, ang.shape, 1)
    c_ref[...] = jnp.where(lane < QK_NOPE, 1.0,
                           jnp.where(lane < QK_NOPE + QK_ROPE, jnp.cos(ang), 0.0))
    s_ref[...] = jnp.sin(ang) * sgn_ref[...]


def _rope_tables(positions):
    T = positions.size
    tt = min(T, 2048)
    half = QK_ROPE // 2
    inv = 1.0 / (ROPE_THETA ** (jnp.arange(half, dtype=F32) / half))
    zeros = jnp.zeros((QK_NOPE,), F32)
    inv128 = jnp.concatenate([zeros, inv, inv, jnp.zeros((32,), F32)])[None, :]
    sgn128 = jnp.concatenate([zeros, -jnp.ones((half,), F32), jnp.ones((half,), F32),
                              jnp.zeros((32,), F32)])[None, :]
    return pl.pallas_call(
        _rope_kernel,
        grid=(T // tt,),
        in_specs=[pl.BlockSpec((tt, 1), lambda i: (i, 0)), _full((1, LANES)), _full((1, LANES))],
        out_specs=[pl.BlockSpec((tt, LANES), lambda i: (i, 0))] * 2,
        out_shape=[jax.ShapeDtypeStruct((T, LANES), F32)] * 2,
        compiler_params=_cparams("arbitrary"),
        name="rope_tables",
    )(positions.reshape(T, 1), inv128, sgn128)


def _mix_in_proj(x_ref, gmix_ref, win_ref):
    h = _rms(x_ref[...], gmix_ref[...]).astype(BF16)
    return jnp.dot(h, win_ref[...], preferred_element_type=F32)


def _mix_in_lru(z, convw_ref, convb_ref, wgate_ref, bgate_ref, lam_ref,
                lru_ref, halo_ref, hcarry_ref, hbuf_ref, *, ts):
    si = pl.program_id(1)
    nlb = LRU_WIDTH // LANES

    ng = ts // 8
    for c in range(nlb):
        halo_ref[c, 8:, :] = z[:, C_XLRU + c * LANES:C_XLRU + (c + 1) * LANES]

    def strided_rows(first):
        return jnp.concatenate([halo_ref[c, pl.ds(first, 8, stride=ng), :] for c in range(nlb)], axis=1)

    xc_parts = []
    for g in range(ng):
        acc = convb_ref[...]
        for kk in range(CONV_W):
            acc = acc + convw_ref[kk:kk + 1, :] * strided_rows(8 + g - (CONV_W - 1 - kk))
        xc_parts.append(acc)
    xc = jnp.concatenate(xc_parts, axis=0)
    halo_ref[:, 0:8, :] = halo_ref[:, ts:ts + 8, :]

    gates = jnp.dot(xc.astype(BF16), wgate_ref[...], preferred_element_type=F32) + bgate_ref[...]
    r = jax.nn.sigmoid(gates[:, :LRU_WIDTH])
    ig = jax.nn.sigmoid(gates[:, LRU_WIDTH:])
    nlam = -lam_ref[...]
    softplus = jnp.maximum(nlam, 0.0) + jnp.log1p(jnp.exp(-jnp.abs(nlam)))
    log_a = -LRU_C * r * softplus
    a = jnp.exp(log_a)
    mult = jnp.sqrt(-jnp.tanh(log_a) * (a * a + 1.0))
    row = lax.broadcasted_iota(jnp.int32, (ts, LRU_WIDTH), 0)
    mult = jnp.where(row + si * ts == 0, 1.0, mult)
    b = mult * (ig * xc)

    hs, ps = [], []
    hrun = jnp.zeros((8, LRU_WIDTH), F32)
    prun = jnp.ones((8, LRU_WIDTH), F32)
    for g in range(ng):
        ag = a[g * 8:(g + 1) * 8, :]
        hrun = ag * hrun + b[g * 8:(g + 1) * 8, :]
        prun = ag * prun
        hs.append(hrun)
        ps.append(prun)
    sub = lax.broadcasted_iota(jnp.int32, (8, LRU_WIDTH), 0)
    d = 1
    while d < 8:
        keep = sub >= d
        hrun = prun * jnp.where(keep, pltpu.roll(hrun, d, 0), 0.0) + hrun
        prun = prun * jnp.where(keep, pltpu.roll(prun, d, 0), 1.0)
        d *= 2
    block_end = hrun + prun * hcarry_ref[...]
    carry_in = jnp.where(sub >= 1, pltpu.roll(block_end, 1, 0), hcarry_ref[...])
    hcarry_ref[...] = block_end[7:8, :]
    for g in range(ng):
        hg = hs[g] + ps[g] * carry_in
        for c in range(nlb):
            hbuf_ref[c, pl.ds(g, 8, stride=ng), :] = hg[:, c * LANES:(c + 1) * LANES]
    hseq = jnp.concatenate([hbuf_ref[c] for c in range(nlb)], axis=1)
    lru_ref[...] = (hseq * jax.nn.gelu(z[:, C_GATE:C_GATE + LRU_WIDTH])).astype(BF16)


def _mix_in_qkv(z, gq_ref, wq_ref, gkv_ref, wkv_ref, vones_ref, c_ref, s_ref, q_ref, k_ref, v_ref):
    cmul = c_ref[...]
    smul = s_ref[...]
    hq = _rms(z[:, C_CQ:C_CQ + Q_LORA], gq_ref[...]).astype(BF16)
    qq = jnp.dot(hq, wq_ref[...], preferred_element_type=F32)
    hkv = _rms(z[:, C_CKV:C_CKV + KV_LORA], gkv_ref[...]).astype(BF16)
    kv = jnp.dot(hkv, wkv_ref[...], preferred_element_type=F32)
    kr = z[:, C_KR:C_KR + HEAD_PAD] * cmul + z[:, C_KRS:C_KRS + HEAD_PAD] * smul
    nq = N_HEADS * HEAD_PAD
    for hh in range(N_HEADS):
        lo = hh * HEAD_PAD
        q_ref[:, lo:lo + HEAD_PAD] = (qq[:, lo:lo + HEAD_PAD] * cmul
                                      + qq[:, nq + lo:nq + lo + HEAD_PAD] * smul).astype(BF16)
        k_ref[:, lo:lo + HEAD_PAD] = (kv[:, lo:lo + HEAD_PAD] + kr).astype(BF16)
    v_ref[...] = (kv[:, nq:] + vones_ref[...]).astype(BF16)


def _mix_in_kernel(x_ref, gmix_ref, win_ref, convw_ref, convb_ref, wgate_ref, bgate_ref,
                   lam_ref, gq_ref, wq_ref, gkv_ref, wkv_ref, vones_ref, c_ref, s_ref,
                   lru_ref, q_ref, k_ref, v_ref, halo_ref, hcarry_ref, hbuf_ref, *, ts, nb):
    @pl.when(pl.program_id(1) == 0)
    def _():
        halo_ref[:, :, 0:8, :] = jnp.zeros((nb, halo_ref.shape[1], 8, LANES), F32)
        hcarry_ref[...] = jnp.zeros_like(hcarry_ref)

    zs = [_mix_in_proj(x_ref.at[bb], gmix_ref, win_ref) for bb in range(nb)]
    for bb in range(nb):
        _mix_in_lru(zs[bb], convw_ref, convb_ref, wgate_ref, bgate_ref, lam_ref,
                    lru_ref.at[bb], halo_ref.at[bb], hcarry_ref.at[bb], hbuf_ref.at[bb], ts=ts)
        _mix_in_qkv(zs[bb], gq_ref, wq_ref, gkv_ref, wkv_ref, vones_ref, c_ref.at[bb], s_ref.at[bb],
                    q_ref.at[bb], k_ref.at[bb], v_ref.at[bb])


def _mix_in(x2, B, S, p, ctab, stab, ts, nb):
    T = B * S
    ns = S // ts
    blk = lambda b, s: (b, s, 0)
    kern = functools.partial(_mix_in_kernel, ts=ts, nb=nb)
    nq = N_HEADS * HEAD_PAD
    nlb = LRU_WIDTH // LANES
    outs = pl.pallas_call(
        kern,
        grid=(B // nb, ns),
        in_specs=[
            pl.BlockSpec((nb, ts, D_MODEL), blk),
            _full((1, D_MODEL)), _full((D_MODEL, IN_COLS_PAD)),
            _full((CONV_W, LRU_WIDTH)), _full((1, LRU_WIDTH)),
            _full((LRU_WIDTH, 2 * LRU_WIDTH)), _full((1, 2 * LRU_WIDTH)),
            _full((1, LRU_WIDTH)),
            _full((1, Q_LORA)), _full((Q_LORA, 2 * nq)),
            _full((1, KV_LORA)), _full((KV_LORA, 2 * nq)), _full((1, nq)),
            pl.BlockSpec((nb, ts, LANES), blk), pl.BlockSpec((nb, ts, LANES), blk),
        ],
        out_specs=[
            pl.BlockSpec((nb, ts, LRU_WIDTH), blk),
            pl.BlockSpec((nb, ts, nq), blk),
            pl.BlockSpec((nb, ts, nq), blk),
            pl.BlockSpec((nb, ts, nq), blk),
        ],
        out_shape=[
            jax.ShapeDtypeStruct((B, S, LRU_WIDTH), BF16),
            jax.ShapeDtypeStruct((B, S, nq), BF16),
            jax.ShapeDtypeStruct((B, S, nq), BF16),
            jax.ShapeDtypeStruct((B, S, nq), BF16),
        ],
        scratch_shapes=[pltpu.VMEM((nb, nlb, ts + 8, LANES), F32),
                        pltpu.VMEM((nb, 1, LRU_WIDTH), F32),
                        pltpu.VMEM((nb, nlb, ts, LANES), F32)],
        compiler_params=_cparams("arbitrary", "arbitrary"),
        name="mix_in",
    )(x2.reshape(B, S, D_MODEL), p["g_mix"], p["w_in"], p["conv_w"], p["conv_b"], p["w_gate"],
      p["b_gate"], p["lam"], p["g_q"], p["w_q"], p["g_kv"], p["w_kv"], p["v_ones"],
      ctab.reshape(B, S, LANES), stab.reshape(B, S, LANES))
    return [o.reshape(T, o.shape[-1]) for o in outs]


def _attn_kernel(q_ref, k_ref, v_ref, o_ref, p_ref, m_ref, al_ref, acc_ref, *, tq, hpb):
    i = pl.program_id(2)
    row = lax.broadcasted_iota(jnp.int32, (tq, tq), 0)
    col = lax.broadcasted_iota(jnp.int32, (tq, tq), 1)
    nt_dims = (((1,), (1,)), ((), ()))

    def scores(j, slot, masked):
        start = pl.multiple_of(j * tq, tq)
        for hh in range(hpb):
            lo = hh * HEAD_PAD
            s = lax.dot_general(q_ref[:, lo:lo + HEAD_PAD], k_ref[pl.ds(start, tq), lo:lo + HEAD_PAD],
                                nt_dims, preferred_element_type=F32)
            if masked:
                s = jnp.where(col <= row, s, -jnp.inf)
            m_old = m_ref[hh]
            m_new = jnp.maximum(m_old, jnp.max(s, axis=1, keepdims=True))
            al_ref[hh] = jnp.exp2(m_old - m_new)
            m_ref[hh] = m_new
            p_ref[slot, hh] = jnp.exp2((s - jnp.tile(m_new, (1, tq // LANES))).astype(BF16))

    def weigh(j, slot):
        start = pl.multiple_of(j * tq, tq)
        for hh in range(hpb):
            acc_ref[hh] = al_ref[hh] * acc_ref[hh] + jnp.dot(
                p_ref[slot, hh], v_ref[pl.ds(start, tq), hh * HEAD_PAD:(hh + 1) * HEAD_PAD],
                preferred_element_type=F32)

    m_ref[...] = jnp.full(m_ref.shape, -jnp.inf, F32)
    acc_ref[...] = jnp.zeros_like(acc_ref)
    scores(i, 0, True)

    def body(j, carry):
        weigh(jnp.where(j == 0, i, j - 1), lax.rem(j, 2))
        scores(j, lax.rem(j + 1, 2), False)
        return carry

    lax.fori_loop(0, i, body, 0)
    weigh(jnp.where(i == 0, i, i - 1), lax.rem(i, 2))
    lane = lax.broadcasted_iota(jnp.int32, (tq, HEAD_PAD), 1)
    for pr in range(hpb // 2):
        even, odd = acc_ref[2 * pr], acc_ref[2 * pr + 1]
        out = jnp.where(lane < V_HEAD, even / pltpu.roll(even, V_HEAD, 1),
                        odd / pltpu.roll(odd, V_HEAD, 1))
        o_ref[:, pr * HEAD_PAD:(pr + 1) * HEAD_PAD] = out.astype(BF16)


def _attention(q, k, v, B, S, tq, hpb):
    T = B * S
    nq = S // tq
    return pl.pallas_call(
        functools.partial(_attn_kernel, tq=tq, hpb=hpb),
        grid=(B, N_HEADS // hpb, nq),
        in_specs=[
            pl.BlockSpec((tq, hpb * HEAD_PAD), lambda b, h, i: (b * nq + i, h)),
            pl.BlockSpec((S, hpb * HEAD_PAD), lambda b, h, i: (b, h)),
            pl.BlockSpec((S, hpb * HEAD_PAD), lambda b, h, i: (b, h)),
        ],
        out_specs=pl.BlockSpec((tq, hpb * V_HEAD), lambda b, h, i: (b * nq + i, h)),
        out_shape=jax.ShapeDtypeStruct((T, N_HEADS * V_HEAD), BF16),
        scratch_shapes=[pltpu.VMEM((2, hpb, tq, tq), BF16),
                        pltpu.VMEM((hpb, tq, LANES), F32), pltpu.VMEM((hpb, tq, LANES), F32),
                        pltpu.VMEM((hpb, tq, HEAD_PAD), F32)],
        compiler_params=_cparams("arbitrary", "arbitrary", "arbitrary"),
        name="mla_attention",
    )(q, k, v)


def _mix_out_kernel(x_ref, lru_ref, att_ref, wo_ref, gffn_ref, xo_ref, h2_ref):
    mixed = jnp.concatenate([lru_ref[...], att_ref[...]], axis=1)
    xn = x_ref[...] + jnp.dot(mixed, wo_ref[...], preferred_element_type=F32)
    xo_ref[...] = xn
    h2_ref[...] = _rms(xn, gffn_ref[...]).astype(BF16)


def _mix_out(x2, lru, att, p, tm):
    T = x2.shape[0]
    row = lambda i: (i, 0)
    return pl.pallas_call(
        _mix_out_kernel,
        grid=(T // tm,),
        in_specs=[
            pl.BlockSpec((tm, D_MODEL), row),
            pl.BlockSpec((tm, LRU_WIDTH), row),
            pl.BlockSpec((tm, N_HEADS * V_HEAD), row),
            _full((D_MODEL, D_MODEL)), _full((1, D_MODEL)),
        ],
        out_specs=[pl.BlockSpec((tm, D_MODEL), row), pl.BlockSpec((tm, D_MODEL), row)],
        out_shape=[jax.ShapeDtypeStruct((T, D_MODEL), F32), jax.ShapeDtypeStruct((T, D_MODEL), BF16)],
        compiler_params=_cparams("arbitrary"),
        name="mix_out",
    )(x2, lru, att, p["w_out"], p["g_ffn"])


def _split3(w):
    hi = w.astype(BF16)
    r1 = w - hi.astype(F32)
    mid = r1.astype(BF16)
    lo = (r1 - mid.astype(F32)).astype(BF16)
    return hi, mid, lo


def _mix_out_route_kernel(x_ref, lru_ref, att_ref, wo_ref, gffn_ref, wrh_ref, wrl_ref,
                          xo_ref, xs_ref, tab_ref, seg_ref, cnt_ref, *, tm):
    mixed = jnp.concatenate([lru_ref[...], att_ref[...]], axis=1)
    xn = x_ref[...] + jnp.dot(mixed, wo_ref[...], preferred_element_type=F32)
    xo_ref[...] = xn
    h2 = _rms(xn, gffn_ref[...])
    h2_hi = h2.astype(BF16)
    h2_lo = (h2 - h2_hi.astype(F32)).astype(BF16)

    nt = (((1,), (1,)), ((), ()))
    logits = (lax.dot_general(wrh_ref[...], h2_hi, nt, preferred_element_type=F32)
              + lax.dot_general(wrh_ref[...], h2_lo, nt, preferred_element_type=F32)
              + lax.dot_general(wrl_ref[...], h2_hi, nt, preferred_element_type=F32))
    eidx = lax.broadcasted_iota(jnp.int32, (E_PAD, tm), 0)
    logits = jnp.where(eidx < N_EXPERTS, logits, -jnp.inf)
    m1 = jnp.max(logits, axis=0, keepdims=True)
    i1 = jnp.min(jnp.where(logits == m1, eidx, E_PAD), axis=0, keepdims=True)
    is0 = eidx == i1
    rest = jnp.where(is0, -jnp.inf, logits)
    m2 = jnp.max(rest, axis=0, keepdims=True)
    i2 = jnp.min(jnp.where(rest == m2, eidx, E_PAD), axis=0, keepdims=True)
    is1 = eidx == i2
    e2 = jnp.exp(m2 - m1)
    den = 1.0 + e2
    w0 = 1.0 / den
    w1 = e2 / den

    sel = jnp.where(is0, 1.0, jnp.where(is1, 1.0, 0.0))
    tr = lax.broadcasted_iota(jnp.int32, (tm, tm), 0)
    tc = lax.broadcasted_iota(jnp.int32, (tm, tm), 1)
    before = jnp.where(tr < tc, 1.0, 0.0).astype(BF16)
    rank = jnp.dot(sel.astype(BF16), before, preferred_element_type=F32).astype(jnp.int32)
    cnt = jnp.sum(sel, axis=1, keepdims=True).astype(jnp.int32)
    cpad = jnp.broadcast_to(((cnt + (SEG_ALIGN - 1)) // SEG_ALIGN) * SEG_ALIGN, (E_PAD, tm))
    inc = cpad
    d = 1
    while d < E_PAD:
        inc = inc + jnp.where(eidx >= d, pltpu.roll(inc, d, 0), 0)
        d *= 2
    segstart = inc - cpad
    dest = segstart + rank
    d0 = jnp.sum(jnp.where(is0, dest, 0), axis=0, keepdims=True)
    d1 = jnp.sum(jnp.where(is1, dest, 0), axis=0, keepdims=True)
    seg_ref[0] = segstart[:, :LANES]
    cnt_ref[0] = cpad[:, :LANES]

    srow = lax.broadcasted_iota(jnp.int32, (LANES, tm), 0)
    stack = jnp.where(srow == 0, w0, jnp.where(srow == 1, w1, jnp.where(
        srow == 2, d0.astype(F32), jnp.where(srow == 3, d1.astype(F32), 0.0))))
    tab = stack.T
    tab_ref[...] = tab

    rr = lax.broadcasted_iota(jnp.int32, (xs_ref.shape[1], tm), 0)
    p0 = jnp.where(rr == d0, 1.0, 0.0).astype(BF16)
    p1 = jnp.where(rr == d1, 1.0, 0.0).astype(BF16)
    xs_ref[0, :, :D_MODEL] = jnp.dot(p0 + p1, h2_hi, preferred_element_type=F32).astype(BF16)
    lane = lax.broadcasted_iota(jnp.int32, (tm, LANES), 1)

    def gate_cols(col):
        hi, mid, lo = (v.astype(F32) for v in
                       _split3(jnp.broadcast_to(tab[:, col:col + 1], (tm, LANES))))
        return jnp.where(lane == 0, hi, jnp.where(lane == 1, mid, jnp.where(
            lane == 2, lo, 0.0))).astype(BF16)

    grows = (jnp.dot(p0, gate_cols(0), preferred_element_type=F32)
             + jnp.dot(p1, gate_cols(1), preferred_element_type=F32))
    xs_ref[0, :, D_MODEL:] = grows.astype(BF16)


def _mix_out_route(x2, lru, att, p, tm):
    T = x2.shape[0]
    nt = T // tm
    rows = 2 * tm + N_EXPERTS * SEG_ALIGN
    row = lambda i: (i, 0)
    blk3 = lambda i: (i, 0, 0)
    return pl.pallas_call(
        functools.partial(_mix_out_route_kernel, tm=tm),
        grid=(nt,),
        in_specs=[
            pl.BlockSpec((tm, D_MODEL), row),
            pl.BlockSpec((tm, LRU_WIDTH), row),
            pl.BlockSpec((tm, N_HEADS * V_HEAD), row),
            _full((D_MODEL, D_MODEL)), _full((1, D_MODEL)),
            _full((E_PAD, D_MODEL)), _full((E_PAD, D_MODEL)),
        ],
        out_specs=[
            pl.BlockSpec((tm, D_MODEL), row),
            pl.BlockSpec((1, rows, XS_COLS), blk3),
            pl.BlockSpec((tm, LANES), row),
            pl.BlockSpec((1, E_PAD, LANES), blk3),
            pl.BlockSpec((1, E_PAD, LANES), blk3),
        ],
        out_shape=[
            jax.ShapeDtypeStruct((T, D_MODEL), F32),
            jax.ShapeDtypeStruct((nt, rows, XS_COLS), BF16),
            jax.ShapeDtypeStruct((T, LANES), F32),
            jax.ShapeDtypeStruct((nt, E_PAD, LANES), jnp.int32),
            jax.ShapeDtypeStruct((nt, E_PAD, LANES), jnp.int32),
        ],
        compiler_params=_cparams("arbitrary"),
        name="mix_out_route",
    )(x2, lru, att, p["w_out"], p["g_ffn"], p["w_router_hi"], p["w_router_lo"])


def _ffn_dense_kernel(x_ref, h2_ref, wg_ref, wu_ref, wd_ref, o_ref, *, fc):
    h2 = h2_ref[...]
    acc = x_ref[...]
    for c in range(D_FF_DENSE // fc):
        g = jnp.dot(h2, wg_ref[:, c * fc:(c + 1) * fc], preferred_element_type=F32)
        u = jnp.dot(h2, wu_ref[:, c * fc:(c + 1) * fc], preferred_element_type=F32)
        act = (g * jax.nn.sigmoid(g) * u).astype(BF16)
        acc = acc + jnp.dot(act, wd_ref[c * fc:(c + 1) * fc, :], preferred_element_type=F32)
    o_ref[...] = acc


def _ffn_dense(xn, h2, p, tm):
    T = xn.shape[0]
    row = lambda i: (i, 0)
    once = pl.Buffered(1)
    return pl.pallas_call(
        functools.partial(_ffn_dense_kernel, fc=256),
        grid=(T // tm,),
        in_specs=[
            pl.BlockSpec((tm, D_MODEL), row), pl.BlockSpec((tm, D_MODEL), row),
            pl.BlockSpec((D_MODEL, D_FF_DENSE), lambda i: (0, 0), pipeline_mode=once),
            pl.BlockSpec((D_MODEL, D_FF_DENSE), lambda i: (0, 0), pipeline_mode=once),
            pl.BlockSpec((D_FF_DENSE, D_MODEL), lambda i: (0, 0), pipeline_mode=once),
        ],
        out_specs=pl.BlockSpec((tm, D_MODEL), row),
        out_shape=jax.ShapeDtypeStruct((T, D_MODEL), F32),
        compiler_params=_cparams("arbitrary"),
        name="ffn_dense",
    )(xn, h2, p["w_gate_d"], p["w_up_d"], p["w_down_d"])


SEG_BITS = (512, 256, 128, 64, 32, 16)


def _moe_expert_kernel(seg_ref, cnt_ref, xs_hbm, wg_ref, wu_ref, wd_ref, yin_hbm, y_hbm,
                       lhs_ref, ost_ref, in_sem, out_sem, *, gj, ch, n_groups):
    del yin_hbm
    k = pl.program_id(0) * n_groups + pl.program_id(1)
    nsteps = N_EXPERTS * n_groups
    slot = lax.rem(k, 2)

    def for_each_piece(step, fn):
        e_s = step // n_groups
        g_s = lax.rem(step, n_groups)
        off = jnp.int32(0)
        for s in range(gj):
            j = g_s * gj + s
            start = seg_ref[j * N_EXPERTS + e_s]
            c = cnt_ref[j * N_EXPERTS + e_s]
            pos = jnp.int32(0)
            for b in SEG_BITS:
                has = (c & b) != 0

                @pl.when(has)
                def _(j=j, src=start + pos, dst=off + pos, b=b):
                    fn(j, pl.multiple_of(src, SEG_ALIGN), pl.multiple_of(dst, SEG_ALIGN), b)

                pos = pos + jnp.where(has, b, 0)
            off = off + c
        return off

    def in_copy(sl):
        def fn(j, src, dst, b):
            return pltpu.make_async_copy(xs_hbm.at[j, pl.ds(src, b), :],
                                         lhs_ref.at[sl, pl.ds(dst, b), :], in_sem.at[sl])
        return fn

    def out_copy(j, src, dst, b):
        return pltpu.make_async_copy(ost_ref.at[pl.ds(dst, b), :],
                                     y_hbm.at[j, pl.ds(src, b), :], out_sem.at[0])

    def start_all(step, mk):
        return for_each_piece(step, lambda *a: mk(*a).start())

    def wait_all(step, mk):
        return for_each_piece(step, lambda *a: mk(*a).wait())

    @pl.when(k == 0)
    def _():
        lhs_ref[...] = jnp.zeros_like(lhs_ref)
        start_all(k, in_copy(slot))

    n_rows = wait_all(k, in_copy(slot))

    @pl.when(k + 1 < nsteps)
    def _():
        start_all(k + 1, in_copy(1 - slot))

    @pl.when(k >= 1)
    def _():
        wait_all(k - 1, out_copy)

    def chunk(ci, carry):
        r0 = pl.multiple_of(ci * ch, ch)
        rows = lhs_ref[slot, pl.ds(r0, ch), :]
        xrow = rows[:, :D_MODEL]
        gate = (rows[:, D_MODEL:D_MODEL + 1].astype(F32) + rows[:, D_MODEL + 1:D_MODEL + 2].astype(F32)
                + rows[:, D_MODEL + 2:D_MODEL + 3].astype(F32))
        g = jnp.dot(xrow, wg_ref[0], preferred_element_type=F32)
        u = jnp.dot(xrow, wu_ref[0], preferred_element_type=F32)
        act = (g * jax.nn.sigmoid(g) * u).astype(BF16)
        y = jnp.dot(act, wd_ref[0], preferred_element_type=F32) * gate
        ost_ref[pl.ds(r0, ch), :] = y.astype(BF16)
        return carry

    lax.fori_loop(0, (n_rows + ch - 1) // ch, chunk, 0)
    start_all(k, out_copy)

    @pl.when(k == nsteps - 1)
    def _():
        wait_all(k, out_copy)


def _moe_experts(xs, seg, cnt, p, gj, ch):
    nt, rows, _ = xs.shape
    n_groups = nt // gj
    cap = gj * (rows - N_EXPERTS * SEG_ALIGN) // 2 + ch
    wmap = lambda e, g, seg_r, cnt_r: (e, 0, 0)
    grid_spec = pltpu.PrefetchScalarGridSpec(
        num_scalar_prefetch=2,
        grid=(N_EXPERTS, n_groups),
        in_specs=[
            pl.BlockSpec(memory_space=pl.ANY),
            pl.BlockSpec((1, D_MODEL, D_FF_EXPERT), wmap),
            pl.BlockSpec((1, D_MODEL, D_FF_EXPERT), wmap),
            pl.BlockSpec((1, D_FF_EXPERT, D_MODEL), wmap),
            pl.BlockSpec(memory_space=pl.ANY),
        ],
        out_specs=pl.BlockSpec(memory_space=pl.ANY),
        scratch_shapes=[
            pltpu.VMEM((2, cap, XS_COLS), BF16),
            pltpu.VMEM((cap, D_MODEL), BF16),
            pltpu.SemaphoreType.DMA((2,)),
            pltpu.SemaphoreType.DMA((1,)),
        ],
    )
    y0 = jnp.zeros((nt, rows, D_MODEL), BF16)
    return pl.pallas_call(
        functools.partial(_moe_expert_kernel, gj=gj, ch=ch, n_groups=n_groups),
        grid_spec=grid_spec,
        out_shape=jax.ShapeDtypeStruct((nt, rows, D_MODEL), BF16),
        input_output_aliases={6: 0},
        compiler_params=_cparams("arbitrary", "arbitrary"),
        name="moe_experts",
    )(seg, cnt, xs, p["w_gate_e"], p["w_up_e"], p["w_down_e"], y0)


def _moe_combine_kernel(x_ref, y_ref, tab_ref, gfin_ref, o_ref, *, final_norm):
    tm = x_ref.shape[0]
    rows = y_ref.shape[1]
    d0 = tab_ref[:, 2:3].astype(jnp.int32)
    d1 = tab_ref[:, 3:4].astype(jnp.int32)
    rr = lax.broadcasted_iota(jnp.int32, (tm, rows), 1)
    pt = jnp.where(rr == d0, 1.0, jnp.where(rr == d1, 1.0, 0.0)).astype(BF16)
    out = x_ref[...] + jnp.dot(pt, y_ref[0], preferred_element_type=F32)
    if final_norm:
        out = _rms(out, gfin_ref[...])
    o_ref[...] = out


def _moe_combine(xn, y, tab, g_final, tm, final_norm):
    T = xn.shape[0]
    rows = y.shape[1]
    row = lambda i: (i, 0)
    return pl.pallas_call(
        functools.partial(_moe_combine_kernel, final_norm=final_norm),
        grid=(T // tm,),
        in_specs=[
            pl.BlockSpec((tm, D_MODEL), row),
            pl.BlockSpec((1, rows, D_MODEL), lambda i: (i, 0, 0)),
            pl.BlockSpec((tm, LANES), row),
            _full((1, D_MODEL)),
        ],
        out_specs=pl.BlockSpec((tm, D_MODEL), row),
        out_shape=jax.ShapeDtypeStruct((T, D_MODEL), F32),
        compiler_params=_cparams("arbitrary"),
        name="moe_combine",
    )(xn, y, tab, g_final)


def _rope_block(w_rope, swap):
    half = QK_ROPE // 2
    if swap:
        w_rope = jnp.concatenate([w_rope[:, half:], w_rope[:, :half]], axis=1)
    k = w_rope.shape[0]
    return jnp.concatenate([jnp.zeros((k, QK_NOPE), w_rope.dtype), w_rope,
                            jnp.zeros((k, HEAD_PAD - QK_NOPE - QK_ROPE), w_rope.dtype)], axis=1)


def _block_diag(w):
    eye = jnp.eye(LRU_BLOCKS, dtype=w.dtype)
    return jnp.einsum("ncd,nm->ncmd", w, eye).reshape(LRU_WIDTH, LRU_WIDTH)


def _layer_params(l, norm_mix, w_in, conv_w, conv_b, lru_wa, lru_ba, lru_wx, lru_bx, lru_lambda,
                  q_norm, w_uq, kv_norm, w_ukv, w_out, norm_ffn):
    scale = math.log2(math.e) / math.sqrt(QK_NOPE + QK_ROPE)
    wi = w_in[l]
    w_kr = wi[:, C_CKV + KV_LORA:]
    w_in_pad = jnp.concatenate([wi[:, :C_KR], _rope_block(w_kr, False), _rope_block(w_kr, True)],
                               axis=1).astype(BF16)
    wq = (w_uq[l] * scale).reshape(Q_LORA, N_HEADS, QK_NOPE + QK_ROPE)
    zpad = jnp.zeros((Q_LORA, N_HEADS, HEAD_PAD - QK_NOPE - QK_ROPE), F32)
    zn = jnp.zeros((Q_LORA, N_HEADS, QK_NOPE), F32)
    half = QK_ROPE // 2
    wq_a = jnp.concatenate([wq, zpad], axis=2).reshape(Q_LORA, N_HEADS * HEAD_PAD)
    wq_sw = jnp.concatenate([wq[:, :, QK_NOPE + half:], wq[:, :, QK_NOPE:QK_NOPE + half]], axis=2)
    wq_b = jnp.concatenate([zn, wq_sw, zpad], axis=2).reshape(Q_LORA, N_HEADS * HEAD_PAD)
    wkv = w_ukv[l].reshape(KV_LORA, N_HEADS, QK_NOPE + V_HEAD)
    wk = jnp.concatenate([wkv[:, :, :QK_NOPE], jnp.zeros((KV_LORA, N_HEADS, HEAD_PAD - QK_NOPE), F32)],
                         axis=2).reshape(KV_LORA, N_HEADS * HEAD_PAD)
    zv = jnp.zeros((KV_LORA, N_HEADS // 2, V_HEAD), F32)
    wv4 = wkv[:, :, QK_NOPE:].reshape(KV_LORA, N_HEADS // 2, 2, V_HEAD)
    wv = jnp.stack([wv4[:, :, 0], zv, zv, wv4[:, :, 1]], axis=2).reshape(KV_LORA, N_HEADS * HEAD_PAD)
    ones_pat = jnp.tile(jnp.concatenate([jnp.zeros((V_HEAD,), F32), jnp.ones((2 * V_HEAD,), F32),
                                         jnp.zeros((V_HEAD,), F32)]), N_HEADS // 2)[None, :]
    return {
        "g_mix": norm_mix[l][None, :],
        "w_in": w_in_pad,
        "conv_w": conv_w[l],
        "conv_b": conv_b[l][None, :],
        "w_gate": jnp.concatenate([_block_diag(lru_wa[l]), _block_diag(lru_wx[l])], axis=1).astype(BF16),
        "b_gate": jnp.concatenate([lru_ba[l].reshape(1, -1), lru_bx[l].reshape(1, -1)], axis=1),
        "lam": lru_lambda[l][None, :],
        "g_q": q_norm[l][None, :],
        "w_q": jnp.concatenate([wq_a, wq_b], axis=1).astype(BF16),
        "g_kv": kv_norm[l][None, :],
        "w_kv": jnp.concatenate([wk, wv], axis=1).astype(BF16),
        "v_ones": ones_pat,
        "w_out": w_out[l].astype(BF16),
        "g_ffn": norm_ffn[l][None, :],
    }


def kernel(x, positions, norm_mix, w_in, conv_w, conv_b, lru_wa, lru_ba, lru_wx, lru_bx, lru_lambda,
           q_norm, w_uq, kv_norm, w_ukv, w_out, norm_ffn, dense_w_gate, dense_w_up, dense_w_down,
           router_w, expert_w_gate, expert_w_up, expert_w_down, norm_final):
    B, S, _ = x.shape
    T = B * S
    depth = norm_mix.shape[0]
    ts = min(S, 256)
    tq = min(S, 512)
    tm = min(T, 512)
    ctab, stab = _rope_tables(positions)
    x2 = x.reshape(T, D_MODEL)
    for l in range(depth):
        p = _layer_params(l, norm_mix, w_in, conv_w, conv_b, lru_wa, lru_ba, lru_wx, lru_bx,
                          lru_lambda, q_norm, w_uq, kv_norm, w_ukv, w_out, norm_ffn)
        lru, q, k, v = _mix_in(x2, B, S, p, ctab, stab, ts, nb=2 if B % 2 == 0 else 1)
        att = _attention(q, k, v, B, S, tq, hpb=4)
        j = l // 2
        last = l == depth - 1
        if l % 2 == 0:
            xn, h2 = _mix_out(x2, lru, att, p, tm)
            p["w_gate_d"] = dense_w_gate[j].astype(BF16)
            p["w_up_d"] = dense_w_up[j].astype(BF16)
            p["w_down_d"] = dense_w_down[j].astype(BF16)
            x2 = _ffn_dense(xn, h2, p, tm)
        else:
            wr = jnp.pad(router_w[j].T, ((0, E_PAD - N_EXPERTS), (0, 0)))
            p["w_router_hi"] = wr.astype(BF16)
            p["w_router_lo"] = (wr - p["w_router_hi"].astype(F32)).astype(BF16)
            xn, xs, tab, seg, cnt = _mix_out_route(x2, lru, att, p, tm)
            p["w_gate_e"] = expert_w_gate[j].astype(BF16)
            p["w_up_e"] = expert_w_up[j].astype(BF16)
            p["w_down_e"] = expert_w_down[j].astype(BF16)
            seg = seg[:, :N_EXPERTS, 0].reshape(-1)
            cnt = cnt[:, :N_EXPERTS, 0].reshape(-1)
            y = _moe_experts(xs, seg, cnt, p, gj=min(8, T // tm), ch=128)
            x2 = _moe_combine(xn, y, tab, norm_final[None, :], tm, final_norm=last)
    return x2.reshape(B, S, D_MODEL)
```

```python
import functools
import math

import jax
import jax.numpy as jnp
from jax import lax
from jax.experimental import pallas as pl
from jax.experimental.pallas import tpu as pltpu

D_MODEL = 1024
EPS = 1e-6
LRU_WIDTH = 512
LRU_BLOCKS = 8
LRU_BLOCK_W = 64
LRU_C = 8.0
CONV_W = 4
N_HEADS = 8
QK_NOPE = 64
QK_ROPE = 32
V_HEAD = 64
Q_LORA = 256
KV_LORA = 128
ROPE_THETA = 10000.0
N_EXPERTS = 8
D_FF_DENSE = 2816
D_FF_EXPERT = 1792

LANES = 128
HEAD_PAD = 128
C_XLRU = 0
C_GATE = LRU_WIDTH
C_CQ = 2 * LRU_WIDTH
C_CKV = C_CQ + Q_LORA
C_KR = C_CKV + KV_LORA
C_KRS = C_KR + HEAD_PAD
IN_COLS_PAD = C_KRS + HEAD_PAD

E_PAD = 16
SEG_ALIGN = 16
XS_COLS = D_MODEL + LANES

VMEM_LIMIT = 56 * 1024 * 1024

F32 = jnp.float32
BF16 = jnp.bfloat16


def _cparams(*sem):
    return pltpu.CompilerParams(dimension_semantics=sem, vmem_limit_bytes=VMEM_LIMIT)


def _rms(x, g):
    return x * lax.rsqrt(jnp.mean(x * x, axis=-1, keepdims=True) + EPS) * g


def _full(shape):
    nd = len(shape)
    return pl.BlockSpec(shape, lambda *_: (0,) * nd)


def _rope_kernel(pos_ref, inv_ref, sgn_ref, c_ref, s_ref):
    ang = pos_ref[...].astype(F32) * inv_ref[...]
    lane = lax.broadcasted_iota(jnp.int32, ang.shape, 1)
    c_ref[...] = jnp.where(lane < QK_NOPE, 1.0,
                           jnp.where(lane < QK_NOPE + QK_ROPE, jnp.cos(ang), 0.0))
    s_ref[...] = jnp.sin(ang) * sgn_ref[...]


def _rope_tables(positions):
    T = positions.size
    tt = min(T, 2048)
    half = QK_ROPE // 2
    inv = 1.0 / (ROPE_THETA ** (jnp.arange(half, dtype=F32) / half))
    zeros = jnp.zeros((QK_NOPE,), F32)
    inv128 = jnp.concatenate([zeros, inv, inv, jnp.zeros((32,), F32)])[None, :]
    sgn128 = jnp.concatenate([zeros, -jnp.ones((half,), F32), jnp.ones((half,), F32),
                              jnp.zeros((32,), F32)])[None, :]
    return pl.pallas_call(
        _rope_kernel,
        grid=(T // tt,),
        in_specs=[pl.BlockSpec((tt, 1), lambda i: (i, 0)), _full((1, LANES)), _full((1, LANES))],
        out_specs=[pl.BlockSpec((tt, LANES), lambda i: (i, 0))] * 2,
        out_shape=[jax.ShapeDtypeStruct((T, LANES), F32)] * 2,
        compiler_params=_cparams("arbitrary"),
        name="rope_tables",
    )(positions.reshape(T, 1), inv128, sgn128)


def _mix_in_proj(x_ref, gmix_ref, win_ref):
    h = _rms(x_ref[...], gmix_ref[...]).astype(BF16)
    return jnp.dot(h, win_ref[...], preferred_element_type=F32)


def _mix_in_lru(z, convw_ref, convb_ref, wgate_ref, bgate_ref, lam_ref,
                lru_ref, halo_ref, hcarry_ref, hbuf_ref, *, ts):
    si = pl.program_id(1)
    nlb = LRU_WIDTH // LANES

    ng = ts // 8
    for c in range(nlb):
        halo_ref[c, 8:, :] = z[:, C_XLRU + c * LANES:C_XLRU + (c + 1) * LANES]

    def strided_rows(first):
        return jnp.concatenate([halo_ref[c, pl.ds(first, 8, stride=ng), :] for c in range(nlb)], axis=1)

    xc_parts = []
    for g in range(ng):
        acc = convb_ref[...]
        for kk in range(CONV_W):
            acc = acc + convw_ref[kk:kk + 1, :] * strided_rows(8 + g - (CONV_W - 1 - kk))
        xc_parts.append(acc)
    xc = jnp.concatenate(xc_parts, axis=0)
    halo_ref[:, 0:8, :] = halo_ref[:, ts:ts + 8, :]

    gates = jnp.dot(xc.astype(BF16), wgate_ref[...], preferred_element_type=F32) + bgate_ref[...]
    r = jax.nn.sigmoid(gates[:, :LRU_WIDTH])
    ig = jax.nn.sigmoid(gates[:, LRU_WIDTH:])
    nlam = -lam_ref[...]
    softplus = jnp.maximum(nlam, 0.0) + jnp.log1p(jnp.exp(-jnp.abs(nlam)))
    log_a = -LRU_C * r * softplus
    a = jnp.exp(log_a)
    mult = jnp.sqrt(-jnp.tanh(log_a) * (a * a + 1.0))
    row = lax.broadcasted_iota(jnp.int32, (ts, LRU_WIDTH), 0)
    mult = jnp.where(row + si * ts == 0, 1.0, mult)
    b = mult * (ig * xc)

    hs, ps = [], []
    hrun = jnp.zeros((8, LRU_WIDTH), F32)
    prun = jnp.ones((8, LRU_WIDTH), F32)
    for g in range(ng):
        ag = a[g * 8:(g + 1) * 8, :]
        hrun = ag * hrun + b[g * 8:(g + 1) * 8, :]
        prun = ag * prun
        hs.append(hrun)
        ps.append(prun)
    sub = lax.broadcasted_iota(jnp.int32, (8, LRU_WIDTH), 0)
    d = 1
    while d < 8:
        keep = sub >= d
        hrun = prun * jnp.where(keep, pltpu.roll(hrun, d, 0), 0.0) + hrun
        prun = prun * jnp.where(keep, pltpu.roll(prun, d, 0), 1.0)
        d *= 2
    block_end = hrun + prun * hcarry_ref[...]
    carry_in = jnp.where(sub >= 1, pltpu.roll(block_end, 1, 0), hcarry_ref[...])
    hcarry_ref[...] = block_end[7:8, :]
    for g in range(ng):
        hg = hs[g] + ps[g] * carry_in
        for c in range(nlb):
            hbuf_ref[c, pl.ds(g, 8, stride=ng), :] = hg[:, c * LANES:(c + 1) * LANES]
    hseq = jnp.concatenate([hbuf_ref[c] for c in range(nlb)], axis=1)
    lru_ref[...] = (hseq * jax.nn.gelu(z[:, C_GATE:C_GATE + LRU_WIDTH])).astype(BF16)


def _mix_in_qkv(z, gq_ref, wq_ref, gkv_ref, wkv_ref, vones_ref, c_ref, s_ref, q_ref, k_ref, v_ref):
    cmul = c_ref[...]
    smul = s_ref[...]
    hq = _rms(z[:, C_CQ:C_CQ + Q_LORA], gq_ref[...]).astype(BF16)
    qq = jnp.dot(hq, wq_ref[...], preferred_element_type=F32)
    hkv = _rms(z[:, C_CKV:C_CKV + KV_LORA], gkv_ref[...]).astype(BF16)
    kv = jnp.dot(hkv, wkv_ref[...], preferred_element_type=F32)
    kr = z[:, C_KR:C_KR + HEAD_PAD] * cmul + z[:, C_KRS:C_KRS + HEAD_PAD] * smul
    nq = N_HEADS * HEAD_PAD
    for hh in range(N_HEADS):
        lo = hh * HEAD_PAD
        q_ref[:, lo:lo + HEAD_PAD] = (qq[:, lo:lo + HEAD_PAD] * cmul
                                      + qq[:, nq + lo:nq + lo + HEAD_PAD] * smul).astype(BF16)
        k_ref[lo:lo + HEAD_PAD, :] = (kv[:, lo:lo + HEAD_PAD] + kr).T.astype(BF16)
    v_ref[...] = (kv[:, nq:] + vones_ref[...]).astype(BF16)


def _mix_in_kernel(x_ref, gmix_ref, win_ref, convw_ref, convb_ref, wgate_ref, bgate_ref,
                   lam_ref, gq_ref, wq_ref, gkv_ref, wkv_ref, vones_ref, c_ref, s_ref,
                   lru_ref, q_ref, k_ref, v_ref, halo_ref, hcarry_ref, hbuf_ref, *, ts, nb):
    @pl.when(pl.program_id(1) == 0)
    def _():
        halo_ref[:, :, 0:8, :] = jnp.zeros((nb, halo_ref.shape[1], 8, LANES), F32)
        hcarry_ref[...] = jnp.zeros_like(hcarry_ref)

    zs = [_mix_in_proj(x_ref.at[bb], gmix_ref, win_ref) for bb in range(nb)]
    for bb in range(nb):
        _mix_in_lru(zs[bb], convw_ref, convb_ref, wgate_ref, bgate_ref, lam_ref,
                    lru_ref.at[bb], halo_ref.at[bb], hcarry_ref.at[bb], hbuf_ref.at[bb], ts=ts)
        _mix_in_qkv(zs[bb], gq_ref, wq_ref, gkv_ref, wkv_ref, vones_ref, c_ref.at[bb], s_ref.at[bb],
                    q_ref.at[bb], k_ref.at[bb], v_ref.at[bb])


def _mix_in(x2, B, S, p, ctab, stab, ts, nb):
    T = B * S
    ns = S // ts
    blk = lambda b, s: (b, s, 0)
    kern = functools.partial(_mix_in_kernel, ts=ts, nb=nb)
    nq = N_HEADS * HEAD_PAD
    nlb = LRU_WIDTH // LANES
    outs = pl.pallas_call(
        kern,
        grid=(B // nb, ns),
        in_specs=[
            pl.BlockSpec((nb, ts, D_MODEL), blk),
            _full((1, D_MODEL)), _full((D_MODEL, IN_COLS_PAD)),
            _full((CONV_W, LRU_WIDTH)), _full((1, LRU_WIDTH)),
            _full((LRU_WIDTH, 2 * LRU_WIDTH)), _full((1, 2 * LRU_WIDTH)),
            _full((1, LRU_WIDTH)),
            _full((1, Q_LORA)), _full((Q_LORA, 2 * nq)),
            _full((1, KV_LORA)), _full((KV_LORA, 2 * nq)), _full((1, nq)),
            pl.BlockSpec((nb, ts, LANES), blk), pl.BlockSpec((nb, ts, LANES), blk),
        ],
        out_specs=[
            pl.BlockSpec((nb, ts, LRU_WIDTH), blk),
            pl.BlockSpec((nb, ts, nq), blk),
            pl.BlockSpec((nb, nq, ts), lambda b, s: (b, 0, s)),
            pl.BlockSpec((nb, ts, nq), blk),
        ],
        out_shape=[
            jax.ShapeDtypeStruct((B, S, LRU_WIDTH), BF16),
            jax.ShapeDtypeStruct((B, S, nq), BF16),
            jax.ShapeDtypeStruct((B, nq, S), BF16),
            jax.ShapeDtypeStruct((B, S, nq), BF16),
        ],
        scratch_shapes=[pltpu.VMEM((nb, nlb, ts + 8, LANES), F32),
                        pltpu.VMEM((nb, 1, LRU_WIDTH), F32),
                        pltpu.VMEM((nb, nlb, ts, LANES), F32)],
        compiler_params=_cparams("arbitrary", "arbitrary"),
        name="mix_in",
    )(x2.reshape(B, S, D_MODEL), p["g_mix"], p["w_in"], p["conv_w"], p["conv_b"], p["w_gate"],
      p["b_gate"], p["lam"], p["g_q"], p["w_q"], p["g_kv"], p["w_kv"], p["v_ones"],
      ctab.reshape(B, S, LANES), stab.reshape(B, S, LANES))
    lru, q, kt, v = outs
    return lru.reshape(T, LRU_WIDTH), q.reshape(T, nq), kt, v.reshape(T, nq)


def _attn_kernel(q_ref, k_ref, v_ref, o_ref, p_ref, m_ref, al_ref, acc_ref, *, tq, hpb):
    i = pl.program_id(2)
    row = lax.broadcasted_iota(jnp.int32, (tq, tq), 0)
    col = lax.broadcasted_iota(jnp.int32, (tq, tq), 1)

    def scores(j, slot, masked):
        start = pl.multiple_of(j * tq, tq)
        for hh in range(hpb):
            lo = hh * HEAD_PAD
            s = jnp.dot(q_ref[:, lo:lo + HEAD_PAD], k_ref[lo:lo + HEAD_PAD, pl.ds(start, tq)],
                        preferred_element_type=F32)
            if masked:
                s = jnp.where(col <= row, s, -jnp.inf)
            m_old = m_ref[hh]
            m_new = jnp.maximum(m_old, jnp.max(s, axis=1, keepdims=True))
            al_ref[hh] = jnp.exp2(m_old - m_new)
            m_ref[hh] = m_new
            p_ref[slot, hh] = jnp.exp2((s - jnp.tile(m_new, (1, tq // LANES))).astype(BF16))

    def weigh(j, slot):
        start = pl.multiple_of(j * tq, tq)
        for hh in range(hpb):
            acc_ref[hh] = al_ref[hh] * acc_ref[hh] + jnp.dot(
                p_ref[slot, hh], v_ref[pl.ds(start, tq), hh * HEAD_PAD:(hh + 1) * HEAD_PAD],
                preferred_element_type=F32)

    m_ref[...] = jnp.full(m_ref.shape, -jnp.inf, F32)
    acc_ref[...] = jnp.zeros_like(acc_ref)
    scores(i, 0, True)

    def body(j, carry):
        weigh(jnp.where(j == 0, i, j - 1), lax.rem(j, 2))
        scores(j, lax.rem(j + 1, 2), False)
        return carry

    lax.fori_loop(0, i, body, 0)
    weigh(jnp.where(i == 0, i, i - 1), lax.rem(i, 2))
    lane = lax.broadcasted_iota(jnp.int32, (tq, HEAD_PAD), 1)
    for pr in range(hpb // 2):
        even, odd = acc_ref[2 * pr], acc_ref[2 * pr + 1]
        out = jnp.where(lane < V_HEAD, even / pltpu.roll(even, V_HEAD, 1),
                        odd / pltpu.roll(odd, V_HEAD, 1))
        o_ref[:, pr * HEAD_PAD:(pr + 1) * HEAD_PAD] = out.astype(BF16)


def _attention(q, k, v, B, S, tq, hpb):
    T = B * S
    nq = S // tq
    return pl.pallas_call(
        functools.partial(_attn_kernel, tq=tq, hpb=hpb),
        grid=(B, N_HEADS // hpb, nq),
        in_specs=[
            pl.BlockSpec((tq, hpb * HEAD_PAD), lambda b, h, i: (b * nq + i, h)),
            pl.BlockSpec((None, hpb * HEAD_PAD, S), lambda b, h, i: (b, h, 0)),
            pl.BlockSpec((S, hpb * HEAD_PAD), lambda b, h, i: (b, h)),
        ],
        out_specs=pl.BlockSpec((tq, hpb * V_HEAD), lambda b, h, i: (b * nq + i, h)),
        out_shape=jax.ShapeDtypeStruct((T, N_HEADS * V_HEAD), BF16),
        scratch_shapes=[pltpu.VMEM((2, hpb, tq, tq), BF16),
                        pltpu.VMEM((hpb, tq, LANES), F32), pltpu.VMEM((hpb, tq, LANES), F32),
                        pltpu.VMEM((hpb, tq, HEAD_PAD), F32)],
        compiler_params=_cparams("arbitrary", "arbitrary", "arbitrary"),
        name="mla_attention",
    )(q, k, v)


def _split3(w):
    hi = w.astype(BF16)
    r1 = w - hi.astype(F32)
    mid = r1.astype(BF16)
    lo = (r1 - mid.astype(F32)).astype(BF16)
    return hi, mid, lo


def _mix_out_route_kernel(x_ref, lru_ref, att_ref, wo_ref, gffn_ref, wrh_ref, wrl_ref,
                          xo_ref, xs_ref, tab_ref, seg_ref, cnt_ref, *, tm):
    mixed = jnp.concatenate([lru_ref[...], att_ref[...]], axis=1)
    xn = x_ref[...] + jnp.dot(mixed, wo_ref[...], preferred_element_type=F32)
    xo_ref[...] = xn
    h2 = _rms(xn, gffn_ref[...])
    h2_hi = h2.astype(BF16)
    h2_lo = (h2 - h2_hi.astype(F32)).astype(BF16)

    nt = (((1,), (1,)), ((), ()))
    logits = (lax.dot_general(wrh_ref[...], h2_hi, nt, preferred_element_type=F32)
              + lax.dot_general(wrh_ref[...], h2_lo, nt, preferred_element_type=F32)
              + lax.dot_general(wrl_ref[...], h2_hi, nt, preferred_element_type=F32))
    eidx = lax.broadcasted_iota(jnp.int32, (E_PAD, tm), 0)
    logits = jnp.where(eidx < N_EXPERTS, logits, -jnp.inf)
    m1 = jnp.max(logits, axis=0, keepdims=True)
    i1 = jnp.min(jnp.where(logits == m1, eidx, E_PAD), axis=0, keepdims=True)
    is0 = eidx == i1
    rest = jnp.where(is0, -jnp.inf, logits)
    m2 = jnp.max(rest, axis=0, keepdims=True)
    i2 = jnp.min(jnp.where(rest == m2, eidx, E_PAD), axis=0, keepdims=True)
    is1 = eidx == i2
    e2 = jnp.exp(m2 - m1)
    den = 1.0 + e2
    w0 = 1.0 / den
    w1 = e2 / den

    sel = jnp.where(is0, 1.0, jnp.where(is1, 1.0, 0.0))
    tr = lax.broadcasted_iota(jnp.int32, (tm, tm), 0)
    tc = lax.broadcasted_iota(jnp.int32, (tm, tm), 1)
    before = jnp.where(tr < tc, 1.0, 0.0).astype(BF16)
    rank = jnp.dot(sel.astype(BF16), before, preferred_element_type=F32).astype(jnp.int32)
    cnt = jnp.sum(sel, axis=1, keepdims=True).astype(jnp.int32)
    cpad = jnp.broadcast_to(((cnt + (SEG_ALIGN - 1)) // SEG_ALIGN) * SEG_ALIGN, (E_PAD, tm))
    inc = cpad
    d = 1
    while d < E_PAD:
        inc = inc + jnp.where(eidx >= d, pltpu.roll(inc, d, 0), 0)
        d *= 2
    segstart = inc - cpad
    dest = segstart + rank
    d0 = jnp.sum(jnp.where(is0, dest, 0), axis=0, keepdims=True)
    d1 = jnp.sum(jnp.where(is1, dest, 0), axis=0, keepdims=True)
    seg_ref[0] = segstart[:, :LANES]
    cnt_ref[0] = cpad[:, :LANES]

    srow = lax.broadcasted_iota(jnp.int32, (LANES, tm), 0)
    stack = jnp.where(srow == 0, w0, jnp.where(srow == 1, w1, jnp.where(
        srow == 2, d0.astype(F32), jnp.where(srow == 3, d1.astype(F32), 0.0))))
    tab = stack.T
    tab_ref[...] = tab

    rr = lax.broadcasted_iota(jnp.int32, (xs_ref.shape[1], tm), 0)
    p0 = jnp.where(rr == d0, 1.0, 0.0).astype(BF16)
    p1 = jnp.where(rr == d1, 1.0, 0.0).astype(BF16)
    xs_ref[0, :, :D_MODEL] = jnp.dot(p0 + p1, h2_hi, preferred_element_type=F32).astype(BF16)
    lane = lax.broadcasted_iota(jnp.int32, (tm, LANES), 1)

    def gate_cols(col):
        hi, mid, lo = (v.astype(F32) for v in
                       _split3(jnp.broadcast_to(tab[:, col:col + 1], (tm, LANES))))
        return jnp.where(lane == 0, hi, jnp.where(lane == 1, mid, jnp.where(
            lane == 2, lo, 0.0))).astype(BF16)

    grows = (jnp.dot(p0, gate_cols(0), preferred_element_type=F32)
             + jnp.dot(p1, gate_cols(1), preferred_element_type=F32))
    xs_ref[0, :, D_MODEL:] = grows.astype(BF16)


def _mix_out_route(x2, lru, att, p, tm):
    T = x2.shape[0]
    nt = T // tm
    rows = 2 * tm + N_EXPERTS * SEG_ALIGN
    row = lambda i: (i, 0)
    blk3 = lambda i: (i, 0, 0)
    return pl.pallas_call(
        functools.partial(_mix_out_route_kernel, tm=tm),
        grid=(nt,),
        in_specs=[
            pl.BlockSpec((tm, D_MODEL), row),
            pl.BlockSpec((tm, LRU_WIDTH), row),
            pl.BlockSpec((tm, N_HEADS * V_HEAD), row),
            _full((D_MODEL, D_MODEL)), _full((1, D_MODEL)),
            _full((E_PAD, D_MODEL)), _full((E_PAD, D_MODEL)),
        ],
        out_specs=[
            pl.BlockSpec((tm, D_MODEL), row),
            pl.BlockSpec((1, rows, XS_COLS), blk3),
            pl.BlockSpec((tm, LANES), row),
            pl.BlockSpec((1, E_PAD, LANES), blk3),
            pl.BlockSpec((1, E_PAD, LANES), blk3),
        ],
        out_shape=[
            jax.ShapeDtypeStruct((T, D_MODEL), F32),
            jax.ShapeDtypeStruct((nt, rows, XS_COLS), BF16),
            jax.ShapeDtypeStruct((T, LANES), F32),
            jax.ShapeDtypeStruct((nt, E_PAD, LANES), jnp.int32),
            jax.ShapeDtypeStruct((nt, E_PAD, LANES), jnp.int32),
        ],
        compiler_params=_cparams("arbitrary"),
        name="mix_out_route",
    )(x2, lru, att, p["w_out"], p["g_ffn"], p["w_router_hi"], p["w_router_lo"])


def _ffn_dense_kernel(x_ref, lru_ref, att_ref, wo_ref, gffn_ref, wg_ref, wu_ref, wd_ref, o_ref, *, fc):
    mixed = jnp.concatenate([lru_ref[...], att_ref[...]], axis=1)
    acc = x_ref[...] + jnp.dot(mixed, wo_ref[...], preferred_element_type=F32)
    h2 = _rms(acc, gffn_ref[...]).astype(BF16)
    for c in range(D_FF_DENSE // fc):
        g = jnp.dot(h2, wg_ref[:, c * fc:(c + 1) * fc], preferred_element_type=F32)
        u = jnp.dot(h2, wu_ref[:, c * fc:(c + 1) * fc], preferred_element_type=F32)
        act = (g * jax.nn.sigmoid(g) * u).astype(BF16)
        acc = acc + jnp.dot(act, wd_ref[c * fc:(c + 1) * fc, :], preferred_element_type=F32)
    o_ref[...] = acc


def _ffn_dense(x2, lru, att, p, tm):
    T = x2.shape[0]
    row = lambda i: (i, 0)
    once = pl.Buffered(1)
    return pl.pallas_call(
        functools.partial(_ffn_dense_kernel, fc=256),
        grid=(T // tm,),
        in_specs=[
            pl.BlockSpec((tm, D_MODEL), row),
            pl.BlockSpec((tm, LRU_WIDTH), row),
            pl.BlockSpec((tm, N_HEADS * V_HEAD), row),
            pl.BlockSpec((D_MODEL, D_MODEL), lambda i: (0, 0), pipeline_mode=once),
            _full((1, D_MODEL)),
            pl.BlockSpec((D_MODEL, D_FF_DENSE), lambda i: (0, 0), pipeline_mode=once),
            pl.BlockSpec((D_MODEL, D_FF_DENSE), lambda i: (0, 0), pipeline_mode=once),
            pl.BlockSpec((D_FF_DENSE, D_MODEL), lambda i: (0, 0), pipeline_mode=once),
        ],
        out_specs=pl.BlockSpec((tm, D_MODEL), row),
        out_shape=jax.ShapeDtypeStruct((T, D_MODEL), F32),
        compiler_params=_cparams("arbitrary"),
        name="ffn_dense",
    )(x2, lru, att, p["w_out"], p["g_ffn"], p["w_gate_d"], p["w_up_d"], p["w_down_d"])


SEG_BITS = (512, 256, 128, 64, 32, 16)


def _moe_expert_kernel(seg_ref, cnt_ref, xs_hbm, wg_ref, wu_ref, wd_ref, yin_hbm, y_hbm,
                       lhs_ref, ost_ref, in_sem, out_sem, *, gj, big, ch, fc, n_groups):
    del yin_hbm
    k = pl.program_id(0) * n_groups + pl.program_id(1)
    nsteps = N_EXPERTS * n_groups
    slot = lax.rem(k, 2)

    def for_each_piece(step, fn):
        e_s = step // n_groups
        g_s = lax.rem(step, n_groups)
        off = jnp.int32(0)
        for s in range(gj):
            j = g_s * gj + s
            start = seg_ref[j * N_EXPERTS + e_s]
            c = cnt_ref[j * N_EXPERTS + e_s]
            pos = jnp.int32(0)
            for b in SEG_BITS:
                has = (c & b) != 0

                @pl.when(has)
                def _(j=j, src=start + pos, dst=off + pos, b=b):
                    fn(j, pl.multiple_of(src, SEG_ALIGN), pl.multiple_of(dst, SEG_ALIGN), b)

                pos = pos + jnp.where(has, b, 0)
            off = off + c
        return off

    def in_copy(sl):
        def fn(j, src, dst, b):
            return pltpu.make_async_copy(xs_hbm.at[j, pl.ds(src, b), :],
                                         lhs_ref.at[sl, pl.ds(dst, b), :], in_sem.at[sl])
        return fn

    def out_copy(j, src, dst, b):
        return pltpu.make_async_copy(ost_ref.at[pl.ds(dst, b), :],
                                     y_hbm.at[j, pl.ds(src, b), :], out_sem.at[0])

    def start_all(step, mk):
        return for_each_piece(step, lambda *a: mk(*a).start())

    def wait_all(step, mk):
        return for_each_piece(step, lambda *a: mk(*a).wait())

    @pl.when(k == 0)
    def _():
        lhs_ref[...] = jnp.zeros_like(lhs_ref)
        start_all(k, in_copy(slot))

    n_rows = wait_all(k, in_copy(slot))

    @pl.when(k + 1 < nsteps)
    def _():
        start_all(k + 1, in_copy(1 - slot))

    @pl.when(k >= 1)
    def _():
        wait_all(k - 1, out_copy)

    def chunk(r0, size, fc):
        rows = lhs_ref[slot, pl.ds(r0, size), :]
        xrow = rows[:, :D_MODEL]
        gate = (rows[:, D_MODEL:D_MODEL + 1].astype(F32) + rows[:, D_MODEL + 1:D_MODEL + 2].astype(F32)
                + rows[:, D_MODEL + 2:D_MODEL + 3].astype(F32))
        y = jnp.zeros((size, D_MODEL), F32)
        for c in range(D_FF_EXPERT // fc):
            g = jnp.dot(xrow, wg_ref[0, :, c * fc:(c + 1) * fc], preferred_element_type=F32)
            u = jnp.dot(xrow, wu_ref[0, :, c * fc:(c + 1) * fc], preferred_element_type=F32)
            act = (g * jax.nn.sigmoid(g) * u).astype(BF16)
            y = y + jnp.dot(act, wd_ref[0, c * fc:(c + 1) * fc, :], preferred_element_type=F32)
        ost_ref[pl.ds(r0, size), :] = (y * gate).astype(BF16)

    n_big = n_rows // big

    def big_chunk(ci, carry):
        chunk(pl.multiple_of(ci * big, big), big, fc)
        return carry

    def small_chunk(ci, carry):
        chunk(pl.multiple_of(n_big * big + ci * ch, ch), ch, D_FF_EXPERT)
        return carry

    lax.fori_loop(0, n_big, big_chunk, 0)
    lax.fori_loop(0, (n_rows - n_big * big + ch - 1) // ch, small_chunk, 0)
    start_all(k, out_copy)

    @pl.when(k == nsteps - 1)
    def _():
        wait_all(k, out_copy)


def _moe_experts(xs, seg, cnt, p, gj, big, ch):
    nt, rows, _ = xs.shape
    n_groups = nt // gj
    cap = gj * (rows - N_EXPERTS * SEG_ALIGN) // 2 + ch
    wmap = lambda e, g, seg_r, cnt_r: (e, 0, 0)
    grid_spec = pltpu.PrefetchScalarGridSpec(
        num_scalar_prefetch=2,
        grid=(N_EXPERTS, n_groups),
        in_specs=[
            pl.BlockSpec(memory_space=pl.ANY),
            pl.BlockSpec((1, D_MODEL, D_FF_EXPERT), wmap),
            pl.BlockSpec((1, D_MODEL, D_FF_EXPERT), wmap),
            pl.BlockSpec((1, D_FF_EXPERT, D_MODEL), wmap),
            pl.BlockSpec(memory_space=pl.ANY),
        ],
        out_specs=pl.BlockSpec(memory_space=pl.ANY),
        scratch_shapes=[
            pltpu.VMEM((2, cap, XS_COLS), BF16),
            pltpu.VMEM((cap, D_MODEL), BF16),
            pltpu.SemaphoreType.DMA((2,)),
            pltpu.SemaphoreType.DMA((1,)),
        ],
    )
    y0 = jnp.zeros((nt, rows, D_MODEL), BF16)
    return pl.pallas_call(
        functools.partial(_moe_expert_kernel, gj=gj, big=big, ch=ch, fc=256, n_groups=n_groups),
        grid_spec=grid_spec,
        out_shape=jax.ShapeDtypeStruct((nt, rows, D_MODEL), BF16),
        input_output_aliases={6: 0},
        compiler_params=_cparams("arbitrary", "arbitrary"),
        name="moe_experts",
    )(seg, cnt, xs, p["w_gate_e"], p["w_up_e"], p["w_down_e"], y0)


def _moe_combine_kernel(x_ref, y_ref, tab_ref, gfin_ref, o_ref, *, final_norm):
    tm = x_ref.shape[0]
    rows = y_ref.shape[1]
    d0 = tab_ref[:, 2:3].astype(jnp.int32)
    d1 = tab_ref[:, 3:4].astype(jnp.int32)
    rr = lax.broadcasted_iota(jnp.int32, (tm, rows), 1)
    pt = jnp.where(rr == d0, 1.0, jnp.where(rr == d1, 1.0, 0.0)).astype(BF16)
    out = x_ref[...] + jnp.dot(pt, y_ref[0], preferred_element_type=F32)
    if final_norm:
        out = _rms(out, gfin_ref[...])
    o_ref[...] = out


def _moe_combine(xn, y, tab, g_final, tm, final_norm):
    T = xn.shape[0]
    rows = y.shape[1]
    row = lambda i: (i, 0)
    return pl.pallas_call(
        functools.partial(_moe_combine_kernel, final_norm=final_norm),
        grid=(T // tm,),
        in_specs=[
            pl.BlockSpec((tm, D_MODEL), row),
            pl.BlockSpec((1, rows, D_MODEL), lambda i: (i, 0, 0)),
            pl.BlockSpec((tm, LANES), row),
            _full((1, D_MODEL)),
        ],
        out_specs=pl.BlockSpec((tm, D_MODEL), row),
        out_shape=jax.ShapeDtypeStruct((T, D_MODEL), F32),
        compiler_params=_cparams("arbitrary"),
        name="moe_combine",
    )(xn, y, tab, g_final)


def _rope_block(w_rope, swap):
    half = QK_ROPE // 2
    if swap:
        w_rope = jnp.concatenate([w_rope[:, half:], w_rope[:, :half]], axis=1)
    k = w_rope.shape[0]
    return jnp.concatenate([jnp.zeros((k, QK_NOPE), w_rope.dtype), w_rope,
                            jnp.zeros((k, HEAD_PAD - QK_NOPE - QK_ROPE), w_rope.dtype)], axis=1)


def _block_diag(w):
    eye = jnp.eye(LRU_BLOCKS, dtype=w.dtype)
    return jnp.einsum("ncd,nm->ncmd", w, eye).reshape(LRU_WIDTH, LRU_WIDTH)


def _layer_params(l, norm_mix, w_in, conv_w, conv_b, lru_wa, lru_ba, lru_wx, lru_bx, lru_lambda,
                  q_norm, w_uq, kv_norm, w_ukv, w_out, norm_ffn):
    scale = math.log2(math.e) / math.sqrt(QK_NOPE + QK_ROPE)
    wi = w_in[l]
    w_kr = wi[:, C_CKV + KV_LORA:]
    w_in_pad = jnp.concatenate([wi[:, :C_KR], _rope_block(w_kr, False), _rope_block(w_kr, True)],
                               axis=1).astype(BF16)
    wq = (w_uq[l] * scale).reshape(Q_LORA, N_HEADS, QK_NOPE + QK_ROPE)
    zpad = jnp.zeros((Q_LORA, N_HEADS, HEAD_PAD - QK_NOPE - QK_ROPE), F32)
    zn = jnp.zeros((Q_LORA, N_HEADS, QK_NOPE), F32)
    half = QK_ROPE // 2
    wq_a = jnp.concatenate([wq, zpad], axis=2).reshape(Q_LORA, N_HEADS * HEAD_PAD)
    wq_sw = jnp.concatenate([wq[:, :, QK_NOPE + half:], wq[:, :, QK_NOPE:QK_NOPE + half]], axis=2)
    wq_b = jnp.concatenate([zn, wq_sw, zpad], axis=2).reshape(Q_LORA, N_HEADS * HEAD_PAD)
    wkv = w_ukv[l].reshape(KV_LORA, N_HEADS, QK_NOPE + V_HEAD)
    wk = jnp.concatenate([wkv[:, :, :QK_NOPE], jnp.zeros((KV_LORA, N_HEADS, HEAD_PAD - QK_NOPE), F32)],
                         axis=2).reshape(KV_LORA, N_HEADS * HEAD_PAD)
    zv = jnp.zeros((KV_LORA, N_HEADS // 2, V_HEAD), F32)
    wv4 = wkv[:, :, QK_NOPE:].reshape(KV_LORA, N_HEADS // 2, 2, V_HEAD)
    wv = jnp.stack([wv4[:, :, 0], zv, zv, wv4[:, :, 1]], axis=2).reshape(KV_LORA, N_HEADS * HEAD_PAD)
    ones_pat = jnp.tile(jnp.concatenate([jnp.zeros((V_HEAD,), F32), jnp.ones((2 * V_HEAD,), F32),
                                         jnp.zeros((V_HEAD,), F32)]), N_HEADS // 2)[None, :]
    return {
        "g_mix": norm_mix[l][None, :],
        "w_in": w_in_pad,
        "conv_w": conv_w[l],
        "conv_b": conv_b[l][None, :],
        "w_gate": jnp.concatenate([_block_diag(lru_wa[l]), _block_diag(lru_wx[l])], axis=1).astype(BF16),
        "b_gate": jnp.concatenate([lru_ba[l].reshape(1, -1), lru_bx[l].reshape(1, -1)], axis=1),
        "lam": lru_lambda[l][None, :],
        "g_q": q_norm[l][None, :],
        "w_q": jnp.concatenate([wq_a, wq_b], axis=1).astype(BF16),
        "g_kv": kv_norm[l][None, :],
        "w_kv": jnp.concatenate([wk, wv], axis=1).astype(BF16),
        "v_ones": ones_pat,
        "w_out": w_out[l].astype(BF16),
        "g_ffn": norm_ffn[l][None, :],
    }


def kernel(x, positions, norm_mix, w_in, conv_w, conv_b, lru_wa, lru_ba, lru_wx, lru_bx, lru_lambda,
           q_norm, w_uq, kv_norm, w_ukv, w_out, norm_ffn, dense_w_gate, dense_w_up, dense_w_down,
           router_w, expert_w_gate, expert_w_up, expert_w_down, norm_final):
    B, S, _ = x.shape
    T = B * S
    depth = norm_mix.shape[0]
    ts = min(S, 256)
    tq = min(S, 512)
    tm = min(T, 512)
    ctab, stab = _rope_tables(positions)
    x2 = x.reshape(T, D_MODEL)
    for l in range(depth):
        p = _layer_params(l, norm_mix, w_in, conv_w, conv_b, lru_wa, lru_ba, lru_wx, lru_bx,
                          lru_lambda, q_norm, w_uq, kv_norm, w_ukv, w_out, norm_ffn)
        lru, q, k, v = _mix_in(x2, B, S, p, ctab, stab, ts, nb=2 if B % 2 == 0 else 1)
        att = _attention(q, k, v, B, S, tq, hpb=4)
        j = l // 2
        last = l == depth - 1
        if l % 2 == 0:
            p["w_gate_d"] = dense_w_gate[j].astype(BF16)
            p["w_up_d"] = dense_w_up[j].astype(BF16)
            p["w_down_d"] = dense_w_down[j].astype(BF16)
            x2 = _ffn_dense(x2, lru, att, p, tm)
        else:
            wr = jnp.pad(router_w[j].T, ((0, E_PAD - N_EXPERTS), (0, 0)))
            p["w_router_hi"] = wr.astype(BF16)
            p["w_router_lo"] = (wr - p["w_router_hi"].astype(F32)).astype(BF16)
            xn, xs, tab, seg, cnt = _mix_out_route(x2, lru, att, p, tm)
            p["w_gate_e"] = expert_w_gate[j].astype(BF16)
            p["w_up_e"] = expert_w_up[j].astype(BF16)
            p["w_down_e"] = expert_w_down[j].astype(BF16)
            seg = seg[:, :N_EXPERTS, 0].reshape(-1)
            cnt = cnt[:, :N_EXPERTS, 0].reshape(-1)
            y = _moe_experts(xs, seg, cnt, p, gj=min(8, T // tm), big=512, ch=128)
            x2 = _moe_combine(xn, y, tab, norm_final[None, :], tm, final_norm=last)
    return x2.reshape(B, S, D_MODEL)
```

```python
import functools
import math

import jax
import jax.numpy as jnp
from jax import lax
from jax.experimental import pallas as pl
from jax.experimental.pallas import tpu as pltpu

D_MODEL = 1024
EPS = 1e-6
LRU_WIDTH = 512
LRU_BLOCKS = 8
LRU_BLOCK_W = 64
LRU_C = 8.0
CONV_W = 4
N_HEADS = 8
QK_NOPE = 64
QK_ROPE = 32
V_HEAD = 64
Q_LORA = 256
KV_LORA = 128
ROPE_THETA = 10000.0
N_EXPERTS = 8
D_FF_DENSE = 2816
D_FF_EXPERT = 1792

LANES = 128
HEAD_PAD = 128
C_XLRU = 0
C_GATE = LRU_WIDTH
C_CQ = 2 * LRU_WIDTH
C_CKV = C_CQ + Q_LORA
C_KR = C_CKV + KV_LORA
C_KRS = C_KR + HEAD_PAD
IN_COLS_PAD = C_KRS + HEAD_PAD

E_PAD = 16
SEG_ALIGN = 16
XS_COLS = D_MODEL + LANES

VMEM_LIMIT = 56 * 1024 * 1024

F32 = jnp.float32
BF16 = jnp.bfloat16


def _cparams(*sem):
    return pltpu.CompilerParams(dimension_semantics=sem, vmem_limit_bytes=VMEM_LIMIT)


def _rms(x, g):
    return x * lax.rsqrt(jnp.mean(x * x, axis=-1, keepdims=True) + EPS) * g


def _full(shape):
    nd = len(shape)
    return pl.BlockSpec(shape, lambda *_: (0,) * nd)


def _rope_kernel(pos_ref, inv_ref, sgn_ref, c_ref, s_ref):
    ang = pos_ref[...].astype(F32) * inv_ref[...]
    lane = lax.broadcasted_iota(jnp.int32, ang.shape, 1)
    c_ref[...] = jnp.where(lane < QK_NOPE, 1.0,
                           jnp.where(lane < QK_NOPE + QK_ROPE, jnp.cos(ang), 0.0))
    s_ref[...] = jnp.sin(ang) * sgn_ref[...]


def _rope_tables(positions):
    T = positions.size
    tt = min(T, 2048)
    half = QK_ROPE // 2
    inv = 1.0 / (ROPE_THETA ** (jnp.arange(half, dtype=F32) / half))
    zeros = jnp.zeros((QK_NOPE,), F32)
    inv128 = jnp.concatenate([zeros, inv, inv, jnp.zeros((32,), F32)])[None, :]
    sgn128 = jnp.concatenate([zeros, -jnp.ones((half,), F32), jnp.ones((half,), F32),
                              jnp.zeros((32,), F32)])[None, :]
    return pl.pallas_call(
        _rope_kernel,
        grid=(T // tt,),
        in_specs=[pl.BlockSpec((tt, 1), lambda i: (i, 0)), _full((1, LANES)), _full((1, LANES))],
        out_specs=[pl.BlockSpec((tt, LANES), lambda i: (i, 0))] * 2,
        out_shape=[jax.ShapeDtypeStruct((T, LANES), F32)] * 2,
        compiler_params=_cparams("arbitrary"),
        name="rope_tables",
    )(positions.reshape(T, 1), inv128, sgn128)


def _mix_in_proj(x_ref, gmix_ref, win_ref):
    h = _rms(x_ref[...], gmix_ref[...]).astype(BF16)
    return jnp.dot(h, win_ref[...], preferred_element_type=F32)


NLB = LRU_WIDTH // LANES


def _lru_conv(z, convw_ref, convb_ref, halo_ref, *, ts):
    nlb = NLB
    ng = ts // 8
    for c in range(nlb):
        halo_ref[c, 8:, :] = z[:, C_XLRU + c * LANES:C_XLRU + (c + 1) * LANES]

    def strided_rows(first):
        return jnp.concatenate([halo_ref[c, pl.ds(first, 8, stride=ng), :] for c in range(nlb)], axis=1)

    xc_parts = []
    for g in range(ng):
        acc = convb_ref[...]
        for kk in range(CONV_W):
            acc = acc + convw_ref[kk:kk + 1, :] * strided_rows(8 + g - (CONV_W - 1 - kk))
        xc_parts.append(acc)
    xc = jnp.concatenate(xc_parts, axis=0)
    halo_ref[:, 0:8, :] = halo_ref[:, ts:ts + 8, :]
    return xc


def _lru_scan(z, xc, gates, lam_ref, lru_ref, hcarry_ref, hbuf_ref, *, ts):
    si = pl.program_id(1)
    nlb = NLB
    ng = ts // 8
    r = jax.nn.sigmoid(gates[:, :LRU_WIDTH])
    ig = jax.nn.sigmoid(gates[:, LRU_WIDTH:])
    nlam = -lam_ref[...]
    softplus = jnp.maximum(nlam, 0.0) + jnp.log1p(jnp.exp(-jnp.abs(nlam)))
    log_a = -LRU_C * r * softplus
    a = jnp.exp(log_a)
    mult = jnp.sqrt(-jnp.tanh(log_a) * (a * a + 1.0))
    row = lax.broadcasted_iota(jnp.int32, (ts, LRU_WIDTH), 0)
    mult = jnp.where(row + si * ts == 0, 1.0, mult)
    b = mult * (ig * xc)

    hs, ps = [], []
    hrun = jnp.zeros((8, LRU_WIDTH), F32)
    prun = jnp.ones((8, LRU_WIDTH), F32)
    for g in range(ng):
        ag = a[g * 8:(g + 1) * 8, :]
        hrun = ag * hrun + b[g * 8:(g + 1) * 8, :]
        prun = ag * prun
        hs.append(hrun)
        ps.append(prun)
    sub = lax.broadcasted_iota(jnp.int32, (8, LRU_WIDTH), 0)
    d = 1
    while d < 8:
        keep = sub >= d
        hrun = prun * jnp.where(keep, pltpu.roll(hrun, d, 0), 0.0) + hrun
        prun = prun * jnp.where(keep, pltpu.roll(prun, d, 0), 1.0)
        d *= 2
    block_end = hrun + prun * hcarry_ref[...]
    carry_in = jnp.where(sub >= 1, pltpu.roll(block_end, 1, 0), hcarry_ref[...])
    hcarry_ref[...] = block_end[7:8, :]
    for g in range(ng):
        hg = hs[g] + ps[g] * carry_in
        for c in range(nlb):
            hbuf_ref[c, pl.ds(g, 8, stride=ng), :] = hg[:, c * LANES:(c + 1) * LANES]
    hseq = jnp.concatenate([hbuf_ref[c] for c in range(nlb)], axis=1)
    lru_ref[...] = (hseq * jax.nn.gelu(z[:, C_GATE:C_GATE + LRU_WIDTH])).astype(BF16)


def _qkv_project(z, gq_ref, wq_ref, gkv_ref, wkv_ref):
    hq = _rms(z[:, C_CQ:C_CQ + Q_LORA], gq_ref[...]).astype(BF16)
    qq = jnp.dot(hq, wq_ref[...], preferred_element_type=F32)
    hkv = _rms(z[:, C_CKV:C_CKV + KV_LORA], gkv_ref[...]).astype(BF16)
    kv = jnp.dot(hkv, wkv_ref[...], preferred_element_type=F32)
    return qq, kv


def _qkv_rotary_store(z, qq, kv, vones_ref, c_ref, s_ref, q_ref, k_ref, v_ref):
    cmul = c_ref[...]
    smul = s_ref[...]
    kr = z[:, C_KR:C_KR + HEAD_PAD] * cmul + z[:, C_KRS:C_KRS + HEAD_PAD] * smul
    nq = N_HEADS * HEAD_PAD
    for hh in range(N_HEADS):
        lo = hh * HEAD_PAD
        q_ref[:, lo:lo + HEAD_PAD] = (qq[:, lo:lo + HEAD_PAD] * cmul
                                      + qq[:, nq + lo:nq + lo + HEAD_PAD] * smul).astype(BF16)
        k_ref[lo:lo + HEAD_PAD, :] = (kv[:, lo:lo + HEAD_PAD] + kr).T.astype(BF16)
    v_ref[...] = (kv[:, nq:] + vones_ref[...]).astype(BF16)


def _mix_in_kernel(x_ref, gmix_ref, win_ref, convw_ref, convb_ref, wgate_ref, bgate_ref,
                   lam_ref, gq_ref, wq_ref, gkv_ref, wkv_ref, vones_ref, c_ref, s_ref,
                   lru_ref, q_ref, k_ref, v_ref, halo_ref, hcarry_ref, hbuf_ref, *, ts, nb):
    @pl.when(pl.program_id(1) == 0)
    def _():
        halo_ref[:, :, 0:8, :] = jnp.zeros((nb, halo_ref.shape[1], 8, LANES), F32)
        hcarry_ref[...] = jnp.zeros_like(hcarry_ref)

    seqs = range(nb)
    zs = [_mix_in_proj(x_ref.at[bb], gmix_ref, win_ref) for bb in seqs]
    xcs = [_lru_conv(zs[bb], convw_ref, convb_ref, halo_ref.at[bb], ts=ts) for bb in seqs]
    gates = [jnp.dot(xcs[bb].astype(BF16), wgate_ref[...], preferred_element_type=F32) + bgate_ref[...]
             for bb in seqs]
    qkvs = [_qkv_project(zs[bb], gq_ref, wq_ref, gkv_ref, wkv_ref) for bb in seqs]
    for bb in seqs:
        _lru_scan(zs[bb], xcs[bb], gates[bb], lam_ref, lru_ref.at[bb], hcarry_ref.at[bb],
                  hbuf_ref.at[bb], ts=ts)
    for bb in seqs:
        _qkv_rotary_store(zs[bb], *qkvs[bb], vones_ref, c_ref.at[bb], s_ref.at[bb],
                          q_ref.at[bb], k_ref.at[bb], v_ref.at[bb])


def _mix_in(x2, B, S, p, ctab, stab, ts, nb):
    T = B * S
    ns = S // ts
    blk = lambda b, s: (b, s, 0)
    kern = functools.partial(_mix_in_kernel, ts=ts, nb=nb)
    nq = N_HEADS * HEAD_PAD
    nlb = LRU_WIDTH // LANES
    outs = pl.pallas_call(
        kern,
        grid=(B // nb, ns),
        in_specs=[
            pl.BlockSpec((nb, ts, D_MODEL), blk),
            _full((1, D_MODEL)), _full((D_MODEL, IN_COLS_PAD)),
            _full((CONV_W, LRU_WIDTH)), _full((1, LRU_WIDTH)),
            _full((LRU_WIDTH, 2 * LRU_WIDTH)), _full((1, 2 * LRU_WIDTH)),
            _full((1, LRU_WIDTH)),
            _full((1, Q_LORA)), _full((Q_LORA, 2 * nq)),
            _full((1, KV_LORA)), _full((KV_LORA, 2 * nq)), _full((1, nq)),
            pl.BlockSpec((nb, ts, LANES), blk), pl.BlockSpec((nb, ts, LANES), blk),
        ],
        out_specs=[
            pl.BlockSpec((nb, ts, LRU_WIDTH), blk),
            pl.BlockSpec((nb, ts, nq), blk),
            pl.BlockSpec((nb, nq, ts), lambda b, s: (b, 0, s)),
            pl.BlockSpec((nb, ts, nq), blk),
        ],
        out_shape=[
            jax.ShapeDtypeStruct((B, S, LRU_WIDTH), BF16),
            jax.ShapeDtypeStruct((B, S, nq), BF16),
            jax.ShapeDtypeStruct((B, nq, S), BF16),
            jax.ShapeDtypeStruct((B, S, nq), BF16),
        ],
        scratch_shapes=[pltpu.VMEM((nb, nlb, ts + 8, LANES), F32),
                        pltpu.VMEM((nb, 1, LRU_WIDTH), F32),
                        pltpu.VMEM((nb, nlb, ts, LANES), F32)],
        compiler_params=_cparams("arbitrary", "arbitrary"),
        name="mix_in",
    )(x2.reshape(B, S, D_MODEL), p["g_mix"], p["w_in"], p["conv_w"], p["conv_b"], p["w_gate"],
      p["b_gate"], p["lam"], p["g_q"], p["w_q"], p["g_kv"], p["w_kv"], p["v_ones"],
      ctab.reshape(B, S, LANES), stab.reshape(B, S, LANES))
    lru, q, kt, v = outs
    return lru.reshape(T, LRU_WIDTH), q.reshape(T, nq), kt, v.reshape(T, nq)


def _attn_kernel(q_ref, k_ref, v_ref, o_ref, p_ref, m_ref, al_ref, acc_ref, *, tq, hpb):
    i = pl.program_id(2)
    row = lax.broadcasted_iota(jnp.int32, (tq, tq), 0)
    col = lax.broadcasted_iota(jnp.int32, (tq, tq), 1)

    def scores(j, slot, masked):
        start = pl.multiple_of(j * tq, tq)
        for hh in range(hpb):
            lo = hh * HEAD_PAD
            s = jnp.dot(q_ref[:, lo:lo + HEAD_PAD], k_ref[lo:lo + HEAD_PAD, pl.ds(start, tq)],
                        preferred_element_type=F32)
            if masked:
                s = jnp.where(col <= row, s, -jnp.inf)
            m_old = m_ref[hh]
            m_new = jnp.maximum(m_old, jnp.max(s, axis=1, keepdims=True))
            al_ref[slot, hh] = jnp.exp2(m_old - m_new)
            m_ref[hh] = m_new
            p_ref[slot, hh] = jnp.exp2((s - jnp.tile(m_new, (1, tq // LANES))).astype(BF16))

    def weigh(j, slot):
        start = pl.multiple_of(j * tq, tq)
        for hh in range(hpb):
            acc_ref[hh] = al_ref[slot, hh] * acc_ref[hh] + jnp.dot(
                p_ref[slot, hh], v_ref[pl.ds(start, tq), hh * HEAD_PAD:(hh + 1) * HEAD_PAD],
                preferred_element_type=F32)

    m_ref[...] = jnp.full(m_ref.shape, -jnp.inf, F32)
    acc_ref[...] = jnp.zeros_like(acc_ref)
    scores(i, 0, True)

    def pair(t, carry):
        a = 2 * t
        scores(a, 1, False)
        weigh(jnp.where(t == 0, i, a - 1), 0)
        scores(a + 1, 0, False)
        weigh(a, 1)
        return carry

    lax.fori_loop(0, i // 2, pair, 0)

    @pl.when(lax.rem(i, 2) == 1)
    def _():
        scores(i - 1, 1, False)
        weigh(jnp.where(i == 1, i, i - 2), 0)
        weigh(i - 1, 1)

    @pl.when(lax.rem(i, 2) == 0)
    def _():
        weigh(jnp.where(i == 0, i, i - 1), 0)

    lane = lax.broadcasted_iota(jnp.int32, (tq, HEAD_PAD), 1)
    for pr in range(hpb // 2):
        even, odd = acc_ref[2 * pr], acc_ref[2 * pr + 1]
        out = jnp.where(lane < V_HEAD, even / pltpu.roll(even, V_HEAD, 1),
                        odd / pltpu.roll(odd, V_HEAD, 1))
        o_ref[:, pr * HEAD_PAD:(pr + 1) * HEAD_PAD] = out.astype(BF16)


def _attention(q, k, v, B, S, tq, hpb):
    T = B * S
    nq = S // tq
    return pl.pallas_call(
        functools.partial(_attn_kernel, tq=tq, hpb=hpb),
        grid=(B, N_HEADS // hpb, nq),
        in_specs=[
            pl.BlockSpec((tq, hpb * HEAD_PAD), lambda b, h, i: (b * nq + i, h)),
            pl.BlockSpec((None, hpb * HEAD_PAD, S), lambda b, h, i: (b, h, 0)),
            pl.BlockSpec((S, hpb * HEAD_PAD), lambda b, h, i: (b, h)),
        ],
        out_specs=pl.BlockSpec((tq, hpb * V_HEAD), lambda b, h, i: (b * nq + i, h)),
        out_shape=jax.ShapeDtypeStruct((T, N_HEADS * V_HEAD), BF16),
        scratch_shapes=[pltpu.VMEM((2, hpb, tq, tq), BF16),
                        pltpu.VMEM((hpb, tq, LANES), F32), pltpu.VMEM((2, hpb, tq, LANES), F32),
                        pltpu.VMEM((hpb, tq, HEAD_PAD), F32)],
        compiler_params=_cparams("arbitrary", "arbitrary", "arbitrary"),
        name="mla_attention",
    )(q, k, v)


def _split3(w):
    hi = w.astype(BF16)
    r1 = w - hi.astype(F32)
    mid = r1.astype(BF16)
    lo = (r1 - mid.astype(F32)).astype(BF16)
    return hi, mid, lo


def _mix_out_route_kernel(x_ref, lru_ref, att_ref, wo_ref, gffn_ref, wrh_ref, wrl_ref,
                          xo_ref, xs_ref, tab_ref, seg_ref, cnt_ref, *, tm):
    mixed = jnp.concatenate([lru_ref[...], att_ref[...]], axis=1)
    xn = x_ref[...] + jnp.dot(mixed, wo_ref[...], preferred_element_type=F32)
    xo_ref[...] = xn
    h2 = _rms(xn, gffn_ref[...])
    h2_hi = h2.astype(BF16)
    h2_lo = (h2 - h2_hi.astype(F32)).astype(BF16)

    nt = (((1,), (1,)), ((), ()))
    logits = (lax.dot_general(wrh_ref[...], h2_hi, nt, preferred_element_type=F32)
              + lax.dot_general(wrh_ref[...], h2_lo, nt, preferred_element_type=F32)
              + lax.dot_general(wrl_ref[...], h2_hi, nt, preferred_element_type=F32))
    eidx = lax.broadcasted_iota(jnp.int32, (E_PAD, tm), 0)
    logits = jnp.where(eidx < N_EXPERTS, logits, -jnp.inf)
    m1 = jnp.max(logits, axis=0, keepdims=True)
    i1 = jnp.min(jnp.where(logits == m1, eidx, E_PAD), axis=0, keepdims=True)
    is0 = eidx == i1
    rest = jnp.where(is0, -jnp.inf, logits)
    m2 = jnp.max(rest, axis=0, keepdims=True)
    i2 = jnp.min(jnp.where(rest == m2, eidx, E_PAD), axis=0, keepdims=True)
    is1 = eidx == i2
    e2 = jnp.exp(m2 - m1)
    den = 1.0 + e2
    w0 = 1.0 / den
    w1 = e2 / den

    sel = jnp.where(is0, 1.0, jnp.where(is1, 1.0, 0.0))
    tr = lax.broadcasted_iota(jnp.int32, (tm, tm), 0)
    tc = lax.broadcasted_iota(jnp.int32, (tm, tm), 1)
    before = jnp.where(tr < tc, 1.0, 0.0).astype(BF16)
    rank = jnp.dot(sel.astype(BF16), before, preferred_element_type=F32).astype(jnp.int32)
    cnt = jnp.sum(sel, axis=1, keepdims=True).astype(jnp.int32)
    cpad = jnp.broadcast_to(((cnt + (SEG_ALIGN - 1)) // SEG_ALIGN) * SEG_ALIGN, (E_PAD, tm))
    inc = cpad
    d = 1
    while d < E_PAD:
        inc = inc + jnp.where(eidx >= d, pltpu.roll(inc, d, 0), 0)
        d *= 2
    segstart = inc - cpad
    dest = segstart + rank
    d0 = jnp.sum(jnp.where(is0, dest, 0), axis=0, keepdims=True)
    d1 = jnp.sum(jnp.where(is1, dest, 0), axis=0, keepdims=True)
    seg_ref[0] = segstart[:, :LANES]
    cnt_ref[0] = cpad[:, :LANES]

    srow = lax.broadcasted_iota(jnp.int32, (LANES, tm), 0)
    stack = jnp.where(srow == 0, w0, jnp.where(srow == 1, w1, jnp.where(
        srow == 2, d0.astype(F32), jnp.where(srow == 3, d1.astype(F32), 0.0))))
    tab = stack.T
    tab_ref[...] = tab

    rr = lax.broadcasted_iota(jnp.int32, (xs_ref.shape[1], tm), 0)
    p0 = jnp.where(rr == d0, 1.0, 0.0).astype(BF16)
    p1 = jnp.where(rr == d1, 1.0, 0.0).astype(BF16)
    xs_ref[0, :, :D_MODEL] = jnp.dot(p0 + p1, h2_hi, preferred_element_type=F32).astype(BF16)
    lane = lax.broadcasted_iota(jnp.int32, (tm, LANES), 1)

    def gate_cols(col):
        hi, mid, lo = (v.astype(F32) for v in
                       _split3(jnp.broadcast_to(tab[:, col:col + 1], (tm, LANES))))
        return jnp.where(lane == 0, hi, jnp.where(lane == 1, mid, jnp.where(
            lane == 2, lo, 0.0))).astype(BF16)

    grows = (jnp.dot(p0, gate_cols(0), preferred_element_type=F32)
             + jnp.dot(p1, gate_cols(1), preferred_element_type=F32))
    xs_ref[0, :, D_MODEL:] = grows.astype(BF16)


def _mix_out_route(x2, lru, att, p, tm):
    T = x2.shape[0]
    nt = T // tm
    rows = 2 * tm + N_EXPERTS * SEG_ALIGN
    row = lambda i: (i, 0)
    blk3 = lambda i: (i, 0, 0)
    return pl.pallas_call(
        functools.partial(_mix_out_route_kernel, tm=tm),
        grid=(nt,),
        in_specs=[
            pl.BlockSpec((tm, D_MODEL), row),
            pl.BlockSpec((tm, LRU_WIDTH), row),
            pl.BlockSpec((tm, N_HEADS * V_HEAD), row),
            _full((D_MODEL, D_MODEL)), _full((1, D_MODEL)),
            _full((E_PAD, D_MODEL)), _full((E_PAD, D_MODEL)),
        ],
        out_specs=[
            pl.BlockSpec((tm, D_MODEL), row),
            pl.BlockSpec((1, rows, XS_COLS), blk3),
            pl.BlockSpec((tm, LANES), row),
            pl.BlockSpec((1, E_PAD, LANES), blk3),
            pl.BlockSpec((1, E_PAD, LANES), blk3),
        ],
        out_shape=[
            jax.ShapeDtypeStruct((T, D_MODEL), F32),
            jax.ShapeDtypeStruct((nt, rows, XS_COLS), BF16),
            jax.ShapeDtypeStruct((T, LANES), F32),
            jax.ShapeDtypeStruct((nt, E_PAD, LANES), jnp.int32),
            jax.ShapeDtypeStruct((nt, E_PAD, LANES), jnp.int32),
        ],
        compiler_params=_cparams("arbitrary"),
        name="mix_out_route",
    )(x2, lru, att, p["w_out"], p["g_ffn"], p["w_router_hi"], p["w_router_lo"])


def _ffn_dense_kernel(x_ref, lru_ref, att_ref, wo_ref, gffn_ref, wg_ref, wu_ref, wd_ref, o_ref, *, fc):
    mixed = jnp.concatenate([lru_ref[...], att_ref[...]], axis=1)
    acc = x_ref[...] + jnp.dot(mixed, wo_ref[...], preferred_element_type=F32)
    h2 = _rms(acc, gffn_ref[...]).astype(BF16)
    for c in range(D_FF_DENSE // fc):
        g = jnp.dot(h2, wg_ref[:, c * fc:(c + 1) * fc], preferred_element_type=F32)
        u = jnp.dot(h2, wu_ref[:, c * fc:(c + 1) * fc], preferred_element_type=F32)
        act = (g * jax.nn.sigmoid(g) * u).astype(BF16)
        acc = acc + jnp.dot(act, wd_ref[c * fc:(c + 1) * fc, :], preferred_element_type=F32)
    o_ref[...] = acc


def _ffn_dense(x2, lru, att, p, tm):
    T = x2.shape[0]
    row = lambda i: (i, 0)
    once = pl.Buffered(1)
    return pl.pallas_call(
        functools.partial(_ffn_dense_kernel, fc=256),
        grid=(T // tm,),
        in_specs=[
            pl.BlockSpec((tm, D_MODEL), row),
            pl.BlockSpec((tm, LRU_WIDTH), row),
            pl.BlockSpec((tm, N_HEADS * V_HEAD), row),
            pl.BlockSpec((D_MODEL, D_MODEL), lambda i: (0, 0), pipeline_mode=once),
            _full((1, D_MODEL)),
            pl.BlockSpec((D_MODEL, D_FF_DENSE), lambda i: (0, 0), pipeline_mode=once),
            pl.BlockSpec((D_MODEL, D_FF_DENSE), lambda i: (0, 0), pipeline_mode=once),
            pl.BlockSpec((D_FF_DENSE, D_MODEL), lambda i: (0, 0), pipeline_mode=once),
        ],
        out_specs=pl.BlockSpec((tm, D_MODEL), row),
        out_shape=jax.ShapeDtypeStruct((T, D_MODEL), F32),
        compiler_params=_cparams("arbitrary"),
        name="ffn_dense",
    )(x2, lru, att, p["w_out"], p["g_ffn"], p["w_gate_d"], p["w_up_d"], p["w_down_d"])


SEG_BITS = (512, 256, 128, 64, 32, 16)
TAIL_BITS = (128, 64, 32, 16)


def _moe_expert_kernel(seg_ref, cnt_ref, xs_hbm, wg_ref, wu_ref, wd_ref, y_hbm,
                       lhs_ref, ost_ref, zero_ref, in_sem, out_sem, zero_sem, *, gj, big, ch, fc, n_groups):
    k = pl.program_id(0) * n_groups + pl.program_id(1)
    nsteps = N_EXPERTS * n_groups
    slot = lax.rem(k, 2)

    def for_each_piece(step, fn):
        e_s = step // n_groups
        g_s = lax.rem(step, n_groups)
        off = jnp.int32(0)
        for s in range(gj):
            j = g_s * gj + s
            start = seg_ref[j * N_EXPERTS + e_s]
            c = cnt_ref[j * N_EXPERTS + e_s]
            pos = jnp.int32(0)
            for b in SEG_BITS:
                has = (c & b) != 0

                @pl.when(has)
                def _(j=j, src=start + pos, dst=off + pos, b=b):
                    fn(j, pl.multiple_of(src, SEG_ALIGN), pl.multiple_of(dst, SEG_ALIGN), b)

                pos = pos + jnp.where(has, b, 0)
            off = off + c
        return off

    def in_copy(sl):
        def fn(j, src, dst, b):
            return pltpu.make_async_copy(xs_hbm.at[j, pl.ds(src, b), :],
                                         lhs_ref.at[sl, pl.ds(dst, b), :], in_sem.at[sl])
        return fn

    def out_copy(j, src, dst, b):
        return pltpu.make_async_copy(ost_ref.at[pl.ds(dst, b), :],
                                     y_hbm.at[j, pl.ds(src, b), :], out_sem.at[0])

    def start_all(step, mk):
        return for_each_piece(step, lambda *a: mk(*a).start())

    def wait_all(step, mk):
        return for_each_piece(step, lambda *a: mk(*a).wait())

    def for_each_tail_piece(step, fn):
        g_s = lax.rem(step, n_groups)
        for s in range(gj):
            j = g_s * gj + s
            last = j * N_EXPERTS + N_EXPERTS - 1
            tail = seg_ref[last] + cnt_ref[last]
            c = y_hbm.shape[1] - tail
            pos = jnp.int32(0)
            for b in TAIL_BITS:
                has = (c & b) != 0

                @pl.when(has)
                def _(j=j, dst=tail + pos, b=b):
                    fn(j, pl.multiple_of(dst, SEG_ALIGN), b)

                pos = pos + jnp.where(has, b, 0)

    def zero_copy(j, dst, b):
        return pltpu.make_async_copy(zero_ref.at[pl.ds(0, b), :], y_hbm.at[j, pl.ds(dst, b), :],
                                     zero_sem.at[0])

    @pl.when(k == 0)
    def _():
        lhs_ref[...] = jnp.zeros_like(lhs_ref)
        zero_ref[...] = jnp.zeros_like(zero_ref)
        start_all(k, in_copy(slot))

    @pl.when(k < n_groups)
    def _():
        for_each_tail_piece(k, lambda *a: zero_copy(*a).start())

    n_rows = wait_all(k, in_copy(slot))

    @pl.when(k + 1 < nsteps)
    def _():
        start_all(k + 1, in_copy(1 - slot))

    @pl.when(k >= 1)
    def _():
        wait_all(k - 1, out_copy)

    def chunk(r0, size, fc):
        rows = lhs_ref[slot, pl.ds(r0, size), :]
        xrow = rows[:, :D_MODEL]
        gate = (rows[:, D_MODEL:D_MODEL + 1].astype(F32) + rows[:, D_MODEL + 1:D_MODEL + 2].astype(F32)
                + rows[:, D_MODEL + 2:D_MODEL + 3].astype(F32))
        y = jnp.zeros((size, D_MODEL), F32)
        for c in range(D_FF_EXPERT // fc):
            g = jnp.dot(xrow, wg_ref[0, :, c * fc:(c + 1) * fc], preferred_element_type=F32)
            u = jnp.dot(xrow, wu_ref[0, :, c * fc:(c + 1) * fc], preferred_element_type=F32)
            act = (g * jax.nn.sigmoid(g) * u).astype(BF16)
            y = y + jnp.dot(act, wd_ref[0, c * fc:(c + 1) * fc, :], preferred_element_type=F32)
        ost_ref[pl.ds(r0, size), :] = (y * gate).astype(BF16)

    n_big = n_rows // big

    def big_chunk(ci, carry):
        chunk(pl.multiple_of(ci * big, big), big, fc)
        return carry

    def small_chunk(ci, carry):
        chunk(pl.multiple_of(n_big * big + ci * ch, ch), ch, D_FF_EXPERT)
        return carry

    lax.fori_loop(0, n_big, big_chunk, 0)
    lax.fori_loop(0, (n_rows - n_big * big + ch - 1) // ch, small_chunk, 0)
    start_all(k, out_copy)

    @pl.when(k < n_groups)
    def _():
        for_each_tail_piece(k, lambda *a: zero_copy(*a).wait())

    @pl.when(k == nsteps - 1)
    def _():
        wait_all(k, out_copy)


def _moe_experts(xs, seg, cnt, p, gj, big, ch):
    nt, rows, _ = xs.shape
    n_groups = nt // gj
    cap = gj * (rows - N_EXPERTS * SEG_ALIGN) // 2 + ch
    wmap = lambda e, g, seg_r, cnt_r: (e, 0, 0)
    grid_spec = pltpu.PrefetchScalarGridSpec(
        num_scalar_prefetch=2,
        grid=(N_EXPERTS, n_groups),
        in_specs=[
            pl.BlockSpec(memory_space=pl.ANY),
            pl.BlockSpec((1, D_MODEL, D_FF_EXPERT), wmap),
            pl.BlockSpec((1, D_MODEL, D_FF_EXPERT), wmap),
            pl.BlockSpec((1, D_FF_EXPERT, D_MODEL), wmap),
        ],
        out_specs=pl.BlockSpec(memory_space=pl.ANY),
        scratch_shapes=[
            pltpu.VMEM((2, cap, XS_COLS), BF16),
            pltpu.VMEM((cap, D_MODEL), BF16),
            pltpu.VMEM((sum(TAIL_BITS), D_MODEL), BF16),
            pltpu.SemaphoreType.DMA((2,)),
            pltpu.SemaphoreType.DMA((1,)),
            pltpu.SemaphoreType.DMA((1,)),
        ],
    )
    assert N_EXPERTS * SEG_ALIGN <= sum(TAIL_BITS)
    return pl.pallas_call(
        functools.partial(_moe_expert_kernel, gj=gj, big=big, ch=ch, fc=256, n_groups=n_groups),
        grid_spec=grid_spec,
        out_shape=jax.ShapeDtypeStruct((nt, rows, D_MODEL), BF16),
        compiler_params=_cparams("arbitrary", "arbitrary"),
        name="moe_experts",
    )(seg, cnt, xs, p["w_gate_e"], p["w_up_e"], p["w_down_e"])


def _moe_combine_kernel(x_ref, y_ref, tab_ref, gfin_ref, o_ref, *, final_norm):
    tm = x_ref.shape[0]
    rows = y_ref.shape[1]
    d0 = tab_ref[:, 2:3].astype(jnp.int32)
    d1 = tab_ref[:, 3:4].astype(jnp.int32)
    rr = lax.broadcasted_iota(jnp.int32, (tm, rows), 1)
    pt = jnp.where(rr == d0, 1.0, jnp.where(rr == d1, 1.0, 0.0)).astype(BF16)
    out = x_ref[...] + jnp.dot(pt, y_ref[0], preferred_element_type=F32)
    if final_norm:
        out = _rms(out, gfin_ref[...])
    o_ref[...] = out


def _moe_combine(xn, y, tab, g_final, tm, final_norm):
    T = xn.shape[0]
    rows = y.shape[1]
    row = lambda i: (i, 0)
    return pl.pallas_call(
        functools.partial(_moe_combine_kernel, final_norm=final_norm),
        grid=(T // tm,),
        in_specs=[
            pl.BlockSpec((tm, D_MODEL), row),
            pl.BlockSpec((1, rows, D_MODEL), lambda i: (i, 0, 0)),
            pl.BlockSpec((tm, LANES), row),
            _full((1, D_MODEL)),
        ],
        out_specs=pl.BlockSpec((tm, D_MODEL), row),
        out_shape=jax.ShapeDtypeStruct((T, D_MODEL), F32),
        compiler_params=_cparams("arbitrary"),
        name="moe_combine",
    )(xn, y, tab, g_final)


def _rope_block(w_rope, swap):
    half = QK_ROPE // 2
    if swap:
        w_rope = jnp.concatenate([w_rope[:, half:], w_rope[:, :half]], axis=1)
    k = w_rope.shape[0]
    return jnp.concatenate([jnp.zeros((k, QK_NOPE), w_rope.dtype), w_rope,
                            jnp.zeros((k, HEAD_PAD - QK_NOPE - QK_ROPE), w_rope.dtype)], axis=1)


def _block_diag(w):
    eye = jnp.eye(LRU_BLOCKS, dtype=w.dtype)
    return jnp.einsum("ncd,nm->ncmd", w, eye).reshape(LRU_WIDTH, LRU_WIDTH)


def _layer_params(l, norm_mix, w_in, conv_w, conv_b, lru_wa, lru_ba, lru_wx, lru_bx, lru_lambda,
                  q_norm, w_uq, kv_norm, w_ukv, w_out, norm_ffn):
    scale = math.log2(math.e) / math.sqrt(QK_NOPE + QK_ROPE)
    wi = w_in[l]
    w_kr = wi[:, C_CKV + KV_LORA:]
    w_in_pad = jnp.concatenate([wi[:, :C_KR], _rope_block(w_kr, False), _rope_block(w_kr, True)],
                               axis=1).astype(BF16)
    wq = (w_uq[l] * scale).reshape(Q_LORA, N_HEADS, QK_NOPE + QK_ROPE)
    zpad = jnp.zeros((Q_LORA, N_HEADS, HEAD_PAD - QK_NOPE - QK_ROPE), F32)
    zn = jnp.zeros((Q_LORA, N_HEADS, QK_NOPE), F32)
    half = QK_ROPE // 2
    wq_a = jnp.concatenate([wq, zpad], axis=2).reshape(Q_LORA, N_HEADS * HEAD_PAD)
    wq_sw = jnp.concatenate([wq[:, :, QK_NOPE + half:], wq[:, :, QK_NOPE:QK_NOPE + half]], axis=2)
    wq_b = jnp.concatenate([zn, wq_sw, zpad], axis=2).reshape(Q_LORA, N_HEADS * HEAD_PAD)
    wkv = w_ukv[l].reshape(KV_LORA, N_HEADS, QK_NOPE + V_HEAD)
    wk = jnp.concatenate([wkv[:, :, :QK_NOPE], jnp.zeros((KV_LORA, N_HEADS, HEAD_PAD - QK_NOPE), F32)],
                         axis=2).reshape(KV_LORA, N_HEADS * HEAD_PAD)
    zv = jnp.zeros((KV_LORA, N_HEADS // 2, V_HEAD), F32)
    wv4 = wkv[:, :, QK_NOPE:].reshape(KV_LORA, N_HEADS // 2, 2, V_HEAD)
    wv = jnp.stack([wv4[:, :, 0], zv, zv, wv4[:, :, 1]], axis=2).reshape(KV_LORA, N_HEADS * HEAD_PAD)
    ones_pat = jnp.tile(jnp.concatenate([jnp.zeros((V_HEAD,), F32), jnp.ones((2 * V_HEAD,), F32),
                                         jnp.zeros((V_HEAD,), F32)]), N_HEADS // 2)[None, :]
    return {
        "g_mix": norm_mix[l][None, :],
        "w_in": w_in_pad,
        "conv_w": conv_w[l],
        "conv_b": conv_b[l][None, :],
        "w_gate": jnp.concatenate([_block_diag(lru_wa[l]), _block_diag(lru_wx[l])], axis=1).astype(BF16),
        "b_gate": jnp.concatenate([lru_ba[l].reshape(1, -1), lru_bx[l].reshape(1, -1)], axis=1),
        "lam": lru_lambda[l][None, :],
        "g_q": q_norm[l][None, :],
        "w_q": jnp.concatenate([wq_a, wq_b], axis=1).astype(BF16),
        "g_kv": kv_norm[l][None, :],
        "w_kv": jnp.concatenate([wk, wv], axis=1).astype(BF16),
        "v_ones": ones_pat,
        "w_out": w_out[l].astype(BF16),
        "g_ffn": norm_ffn[l][None, :],
    }


def kernel(x, positions, norm_mix, w_in, conv_w, conv_b, lru_wa, lru_ba, lru_wx, lru_bx, lru_lambda,
           q_norm, w_uq, kv_norm, w_ukv, w_out, norm_ffn, dense_w_gate, dense_w_up, dense_w_down,
           router_w, expert_w_gate, expert_w_up, expert_w_down, norm_final):
    B, S, _ = x.shape
    T = B * S
    depth = norm_mix.shape[0]
    ts = min(S, 256)
    tq = min(S, 512)
    tm = min(T, 512)
    ctab, stab = _rope_tables(positions)
    x2 = x.reshape(T, D_MODEL)
    for l in range(depth):
        p = _layer_params(l, norm_mix, w_in, conv_w, conv_b, lru_wa, lru_ba, lru_wx, lru_bx,
                          lru_lambda, q_norm, w_uq, kv_norm, w_ukv, w_out, norm_ffn)
        lru, q, k, v = _mix_in(x2, B, S, p, ctab, stab, ts, nb=2 if B % 2 == 0 else 1)
        att = _attention(q, k, v, B, S, tq, hpb=4)
        j = l // 2
        last = l == depth - 1
        if l % 2 == 0:
            p["w_gate_d"] = dense_w_gate[j].astype(BF16)
            p["w_up_d"] = dense_w_up[j].astype(BF16)
            p["w_down_d"] = dense_w_down[j].astype(BF16)
            x2 = _ffn_dense(x2, lru, att, p, tm)
        else:
            wr = jnp.pad(router_w[j].T, ((0, E_PAD - N_EXPERTS), (0, 0)))
            p["w_router_hi"] = wr.astype(BF16)
            p["w_router_lo"] = (wr - p["w_router_hi"].astype(F32)).astype(BF16)
            xn, xs, tab, seg, cnt = _mix_out_route(x2, lru, att, p, tm)
            p["w_gate_e"] = expert_w_gate[j].astype(BF16)
            p["w_up_e"] = expert_w_up[j].astype(BF16)
            p["w_down_e"] = expert_w_down[j].astype(BF16)
            seg = seg[:, :N_EXPERTS, 0].reshape(-1)
            cnt = cnt[:, :N_EXPERTS, 0].reshape(-1)
            y = _moe_experts(xs, seg, cnt, p, gj=min(8, T // tm), big=512, ch=128)
            x2 = _moe_combine(xn, y, tab, norm_final[None, :], tm, final_norm=last)
    return x2.reshape(B, S, D_MODEL)
```

```python
import functools
import math

import jax
import jax.numpy as jnp
from jax import lax
from jax.experimental import pallas as pl
from jax.experimental.pallas import tpu as pltpu

D_MODEL = 1024
EPS = 1e-6
LRU_WIDTH = 512
LRU_BLOCKS = 8
LRU_BLOCK_W = 64
LRU_C = 8.0
CONV_W = 4
N_HEADS = 8
QK_NOPE = 64
QK_ROPE = 32
V_HEAD = 64
Q_LORA = 256
KV_LORA = 128
ROPE_THETA = 10000.0
N_EXPERTS = 8
D_FF_DENSE = 2816
D_FF_EXPERT = 1792

LANES = 128
HEAD_PAD = 128
C_XLRU = 0
C_GATE = LRU_WIDTH
C_CQ = 2 * LRU_WIDTH
C_CKV = C_CQ + Q_LORA
C_KR = C_CKV + KV_LORA
C_KRS = C_KR + HEAD_PAD
IN_COLS_PAD = C_KRS + HEAD_PAD

E_PAD = 16
SEG_ALIGN = 16
XS_COLS = D_MODEL + LANES

VMEM_LIMIT = 56 * 1024 * 1024

F32 = jnp.float32
BF16 = jnp.bfloat16


def _cparams(*sem):
    return pltpu.CompilerParams(dimension_semantics=sem, vmem_limit_bytes=VMEM_LIMIT)


def _rms(x, g):
    return x * lax.rsqrt(jnp.mean(x * x, axis=-1, keepdims=True) + EPS) * g


def _full(shape):
    nd = len(shape)
    return pl.BlockSpec(shape, lambda *_: (0,) * nd)


def _rope_kernel(pos_ref, inv_ref, sgn_ref, c_ref, s_ref):
    ang = pos_ref[...].astype(F32) * inv_ref[...]
    lane = lax.broadcasted_iota(jnp.int32, ang.shape, 1)
    c_ref[...] = jnp.where(lane < QK_NOPE, 1.0,
                           jnp.where(lane < QK_NOPE + QK_ROPE, jnp.cos(ang), 0.0))
    s_ref[...] = jnp.sin(ang) * sgn_ref[...]


def _rope_tables(positions):
    T = positions.size
    tt = min(T, 2048)
    half = QK_ROPE // 2
    inv = 1.0 / (ROPE_THETA ** (jnp.arange(half, dtype=F32) / half))
    zeros = jnp.zeros((QK_NOPE,), F32)
    inv128 = jnp.concatenate([zeros, inv, inv, jnp.zeros((32,), F32)])[None, :]
    sgn128 = jnp.concatenate([zeros, -jnp.ones((half,), F32), jnp.ones((half,), F32),
                              jnp.zeros((32,), F32)])[None, :]
    return pl.pallas_call(
        _rope_kernel,
        grid=(T // tt,),
        in_specs=[pl.BlockSpec((tt, 1), lambda i: (i, 0)), _full((1, LANES)), _full((1, LANES))],
        out_specs=[pl.BlockSpec((tt, LANES), lambda i: (i, 0))] * 2,
        out_shape=[jax.ShapeDtypeStruct((T, LANES), F32)] * 2,
        compiler_params=_cparams("arbitrary"),
        name="rope_tables",
    )(positions.reshape(T, 1), inv128, sgn128)


def _mix_in_proj(x_ref, gmix_ref, win_ref):
    h = _rms(x_ref[...], gmix_ref[...]).astype(BF16)
    return jnp.dot(h, win_ref[...], preferred_element_type=F32)


NLB = LRU_WIDTH // LANES


def _lru_conv(z, convw_ref, convb_ref, halo_ref, *, ts):
    nlb = NLB
    ng = ts // 8
    for c in range(nlb):
        halo_ref[c, 8:, :] = z[:, C_XLRU + c * LANES:C_XLRU + (c + 1) * LANES]

    def strided_rows(first):
        return jnp.concatenate([halo_ref[c, pl.ds(first, 8, stride=ng), :] for c in range(nlb)], axis=1)

    xc_parts = []
    for g in range(ng):
        acc = convb_ref[...]
        for kk in range(CONV_W):
            acc = acc + convw_ref[kk:kk + 1, :] * strided_rows(8 + g - (CONV_W - 1 - kk))
        xc_parts.append(acc)
    xc = jnp.concatenate(xc_parts, axis=0)
    halo_ref[:, 0:8, :] = halo_ref[:, ts:ts + 8, :]
    return xc


def _lru_scan(z, xc, gates, lam_ref, lru_ref, hcarry_ref, hbuf_ref, *, ts):
    si = pl.program_id(1)
    nlb = NLB
    ng = ts // 8
    r = jax.nn.sigmoid(gates[:, :LRU_WIDTH])
    ig = jax.nn.sigmoid(gates[:, LRU_WIDTH:])
    nlam = -lam_ref[...]
    softplus = jnp.maximum(nlam, 0.0) + jnp.log1p(jnp.exp(-jnp.abs(nlam)))
    log_a = -LRU_C * r * softplus
    a = jnp.exp(log_a)
    mult = jnp.sqrt(-jnp.tanh(log_a) * (a * a + 1.0))
    row = lax.broadcasted_iota(jnp.int32, (ts, LRU_WIDTH), 0)
    mult = jnp.where(row + si * ts == 0, 1.0, mult)
    b = mult * (ig * xc)

    hs, ps = [], []
    hrun = jnp.zeros((8, LRU_WIDTH), F32)
    prun = jnp.ones((8, LRU_WIDTH), F32)
    for g in range(ng):
        ag = a[g * 8:(g + 1) * 8, :]
        hrun = ag * hrun + b[g * 8:(g + 1) * 8, :]
        prun = ag * prun
        hs.append(hrun)
        ps.append(prun)
    sub = lax.broadcasted_iota(jnp.int32, (8, LRU_WIDTH), 0)
    d = 1
    while d < 8:
        keep = sub >= d
        hrun = prun * jnp.where(keep, pltpu.roll(hrun, d, 0), 0.0) + hrun
        prun = prun * jnp.where(keep, pltpu.roll(prun, d, 0), 1.0)
        d *= 2
    block_end = hrun + prun * hcarry_ref[...]
    carry_in = jnp.where(sub >= 1, pltpu.roll(block_end, 1, 0), hcarry_ref[...])
    hcarry_ref[...] = block_end[7:8, :]
    for g in range(ng):
        hg = hs[g] + ps[g] * carry_in
        for c in range(nlb):
            hbuf_ref[c, pl.ds(g, 8, stride=ng), :] = hg[:, c * LANES:(c + 1) * LANES]
    hseq = jnp.concatenate([hbuf_ref[c] for c in range(nlb)], axis=1)
    lru_ref[...] = (hseq * jax.nn.gelu(z[:, C_GATE:C_GATE + LRU_WIDTH])).astype(BF16)


def _qkv_project(z, gq_ref, wq_ref, gkv_ref, wkv_ref):
    hq = _rms(z[:, C_CQ:C_CQ + Q_LORA], gq_ref[...]).astype(BF16)
    qq = jnp.dot(hq, wq_ref[...], preferred_element_type=F32)
    hkv = _rms(z[:, C_CKV:C_CKV + KV_LORA], gkv_ref[...]).astype(BF16)
    kv = jnp.dot(hkv, wkv_ref[...], preferred_element_type=F32)
    return qq, kv


def _qkv_rotary_store(z, qq, kv, vones_ref, c_ref, s_ref, q_ref, k_ref, v_ref):
    cmul = c_ref[...]
    smul = s_ref[...]
    kr = z[:, C_KR:C_KR + HEAD_PAD] * cmul + z[:, C_KRS:C_KRS + HEAD_PAD] * smul
    nq = N_HEADS * HEAD_PAD
    for hh in range(N_HEADS):
        lo = hh * HEAD_PAD
        q_ref[:, lo:lo + HEAD_PAD] = (qq[:, lo:lo + HEAD_PAD] * cmul
                                      + qq[:, nq + lo:nq + lo + HEAD_PAD] * smul).astype(BF16)
        k_ref[lo:lo + HEAD_PAD, :] = (kv[:, lo:lo + HEAD_PAD] + kr).T.astype(BF16)
    v_ref[...] = (kv[:, nq:] + vones_ref[...]).astype(BF16)


def _mix_in_kernel(x_ref, gmix_ref, win_ref, convw_ref, convb_ref, wgate_ref, bgate_ref,
                   lam_ref, gq_ref, wq_ref, gkv_ref, wkv_ref, vones_ref, c_ref, s_ref,
                   lru_ref, q_ref, k_ref, v_ref, halo_ref, hcarry_ref, hbuf_ref, *, ts, nb):
    @pl.when(pl.program_id(1) == 0)
    def _():
        halo_ref[:, :, 0:8, :] = jnp.zeros((nb, halo_ref.shape[1], 8, LANES), F32)
        hcarry_ref[...] = jnp.zeros_like(hcarry_ref)

    seqs = range(nb)
    zs = [_mix_in_proj(x_ref.at[bb], gmix_ref, win_ref) for bb in seqs]
    xcs = [_lru_conv(zs[bb], convw_ref, convb_ref, halo_ref.at[bb], ts=ts) for bb in seqs]
    gates = [jnp.dot(xcs[bb].astype(BF16), wgate_ref[...], preferred_element_type=F32) + bgate_ref[...]
             for bb in seqs]
    qkvs = [_qkv_project(zs[bb], gq_ref, wq_ref, gkv_ref, wkv_ref) for bb in seqs]
    for bb in seqs:
        _lru_scan(zs[bb], xcs[bb], gates[bb], lam_ref, lru_ref.at[bb], hcarry_ref.at[bb],
                  hbuf_ref.at[bb], ts=ts)
    for bb in seqs:
        _qkv_rotary_store(zs[bb], *qkvs[bb], vones_ref, c_ref.at[bb], s_ref.at[bb],
                          q_ref.at[bb], k_ref.at[bb], v_ref.at[bb])


def _mix_in(x2, B, S, p, ctab, stab, ts, nb):
    T = B * S
    ns = S // ts
    blk = lambda b, s: (b, s, 0)
    kern = functools.partial(_mix_in_kernel, ts=ts, nb=nb)
    nq = N_HEADS * HEAD_PAD
    nlb = LRU_WIDTH // LANES
    outs = pl.pallas_call(
        kern,
        grid=(B // nb, ns),
        in_specs=[
            pl.BlockSpec((nb, ts, D_MODEL), blk),
            _full((1, D_MODEL)), _full((D_MODEL, IN_COLS_PAD)),
            _full((CONV_W, LRU_WIDTH)), _full((1, LRU_WIDTH)),
            _full((LRU_WIDTH, 2 * LRU_WIDTH)), _full((1, 2 * LRU_WIDTH)),
            _full((1, LRU_WIDTH)),
            _full((1, Q_LORA)), _full((Q_LORA, 2 * nq)),
            _full((1, KV_LORA)), _full((KV_LORA, 2 * nq)), _full((1, nq)),
            pl.BlockSpec((nb, ts, LANES), blk), pl.BlockSpec((nb, ts, LANES), blk),
        ],
        out_specs=[
            pl.BlockSpec((nb, ts, LRU_WIDTH), blk),
            pl.BlockSpec((nb, ts, nq), blk),
            pl.BlockSpec((nb, nq, ts), lambda b, s: (b, 0, s)),
            pl.BlockSpec((nb, ts, nq), blk),
        ],
        out_shape=[
            jax.ShapeDtypeStruct((B, S, LRU_WIDTH), BF16),
            jax.ShapeDtypeStruct((B, S, nq), BF16),
            jax.ShapeDtypeStruct((B, nq, S), BF16),
            jax.ShapeDtypeStruct((B, S, nq), BF16),
        ],
        scratch_shapes=[pltpu.VMEM((nb, nlb, ts + 8, LANES), F32),
                        pltpu.VMEM((nb, 1, LRU_WIDTH), F32),
                        pltpu.VMEM((nb, nlb, ts, LANES), F32)],
        compiler_params=_cparams("arbitrary", "arbitrary"),
        name="mix_in",
    )(x2.reshape(B, S, D_MODEL), p["g_mix"], p["w_in"], p["conv_w"], p["conv_b"], p["w_gate"],
      p["b_gate"], p["lam"], p["g_q"], p["w_q"], p["g_kv"], p["w_kv"], p["v_ones"],
      ctab.reshape(B, S, LANES), stab.reshape(B, S, LANES))
    lru, q, kt, v = outs
    return lru.reshape(T, LRU_WIDTH), q.reshape(T, nq), kt, v.reshape(T, nq)


def _attn_kernel(q_ref, k_ref, v_ref, o_ref, p_ref, m_ref, al_ref, acc_ref, *, tq, hpb):
    i = pl.program_id(2)
    row = lax.broadcasted_iota(jnp.int32, (tq, tq), 0)
    col = lax.broadcasted_iota(jnp.int32, (tq, tq), 1)

    def scores(j, slot, masked):
        start = pl.multiple_of(j * tq, tq)
        for hh in range(hpb):
            lo = hh * HEAD_PAD
            s = jnp.dot(q_ref[:, lo:lo + HEAD_PAD], k_ref[lo:lo + HEAD_PAD, pl.ds(start, tq)],
                        preferred_element_type=F32)
            if masked:
                s = jnp.where(col <= row, s, -jnp.inf)
            m_old = m_ref[hh]
            m_new = jnp.maximum(m_old, jnp.max(s, axis=1, keepdims=True))
            al_ref[slot, hh] = jnp.exp2(m_old - m_new)
            m_ref[hh] = m_new
            p_ref[slot, hh] = jnp.exp2((s - jnp.tile(m_new, (1, tq // LANES))).astype(BF16))

    def weigh(j, slot):
        start = pl.multiple_of(j * tq, tq)
        for hh in range(hpb):
            acc_ref[hh] = al_ref[slot, hh] * acc_ref[hh] + jnp.dot(
                p_ref[slot, hh], v_ref[pl.ds(start, tq), hh * HEAD_PAD:(hh + 1) * HEAD_PAD],
                preferred_element_type=F32)

    m_ref[...] = jnp.full(m_ref.shape, -jnp.inf, F32)
    acc_ref[...] = jnp.zeros_like(acc_ref)
    odd = lax.rem(i, 2)
    first_pending = jnp.where(odd == 1, 0, i)

    @pl.when(odd == 1)
    def _():
        scores(i, 1, True)
        scores(0, 0, False)
        weigh(i, 1)

    @pl.when(odd == 0)
    def _():
        scores(i, 0, True)

    def pair(t, carry):
        a = 2 * t + odd
        scores(a, 1, False)
        weigh(jnp.where(t == 0, first_pending, a - 1), 0)
        scores(a + 1, 0, False)
        weigh(a, 1)
        return carry

    lax.fori_loop(0, i // 2, pair, 0)
    weigh(jnp.where(i < 2, first_pending, i - 1), 0)

    lane = lax.broadcasted_iota(jnp.int32, (tq, HEAD_PAD), 1)
    for pr in range(hpb // 2):
        even, odd = acc_ref[2 * pr], acc_ref[2 * pr + 1]
        out = jnp.where(lane < V_HEAD, even / pltpu.roll(even, V_HEAD, 1),
                        odd / pltpu.roll(odd, V_HEAD, 1))
        o_ref[:, pr * HEAD_PAD:(pr + 1) * HEAD_PAD] = out.astype(BF16)


def _attention(q, k, v, B, S, tq, hpb):
    T = B * S
    nq = S // tq
    return pl.pallas_call(
        functools.partial(_attn_kernel, tq=tq, hpb=hpb),
        grid=(B, N_HEADS // hpb, nq),
        in_specs=[
            pl.BlockSpec((tq, hpb * HEAD_PAD), lambda b, h, i: (b * nq + i, h)),
            pl.BlockSpec((None, hpb * HEAD_PAD, S), lambda b, h, i: (b, h, 0)),
            pl.BlockSpec((S, hpb * HEAD_PAD), lambda b, h, i: (b, h)),
        ],
        out_specs=pl.BlockSpec((tq, hpb * V_HEAD), lambda b, h, i: (b * nq + i, h)),
        out_shape=jax.ShapeDtypeStruct((T, N_HEADS * V_HEAD), BF16),
        scratch_shapes=[pltpu.VMEM((2, hpb, tq, tq), BF16),
                        pltpu.VMEM((hpb, tq, LANES), F32), pltpu.VMEM((2, hpb, tq, LANES), F32),
                        pltpu.VMEM((hpb, tq, HEAD_PAD), F32)],
        compiler_params=_cparams("arbitrary", "arbitrary", "arbitrary"),
        name="mla_attention",
    )(q, k, v)


def _split3(w):
    hi = w.astype(BF16)
    r1 = w - hi.astype(F32)
    mid = r1.astype(BF16)
    lo = (r1 - mid.astype(F32)).astype(BF16)
    return hi, mid, lo


def _mix_out_route_kernel(x_ref, lru_ref, att_ref, wo_ref, gffn_ref, wrh_ref, wrl_ref,
                          xo_ref, xs_ref, tab_ref, seg_ref, cnt_ref, *, tm):
    mixed = jnp.concatenate([lru_ref[...], att_ref[...]], axis=1)
    xn = x_ref[...] + jnp.dot(mixed, wo_ref[...], preferred_element_type=F32)
    xo_ref[...] = xn
    h2 = _rms(xn, gffn_ref[...])
    h2_hi = h2.astype(BF16)
    h2_lo = (h2 - h2_hi.astype(F32)).astype(BF16)

    nt = (((1,), (1,)), ((), ()))
    logits = (lax.dot_general(wrh_ref[...], h2_hi, nt, preferred_element_type=F32)
              + lax.dot_general(wrh_ref[...], h2_lo, nt, preferred_element_type=F32)
              + lax.dot_general(wrl_ref[...], h2_hi, nt, preferred_element_type=F32))
    eidx = lax.broadcasted_iota(jnp.int32, (E_PAD, tm), 0)
    logits = jnp.where(eidx < N_EXPERTS, logits, -jnp.inf)
    m1 = jnp.max(logits, axis=0, keepdims=True)
    i1 = jnp.min(jnp.where(logits == m1, eidx, E_PAD), axis=0, keepdims=True)
    is0 = eidx == i1
    rest = jnp.where(is0, -jnp.inf, logits)
    m2 = jnp.max(rest, axis=0, keepdims=True)
    i2 = jnp.min(jnp.where(rest == m2, eidx, E_PAD), axis=0, keepdims=True)
    is1 = eidx == i2
    e2 = jnp.exp(m2 - m1)
    den = 1.0 + e2
    w0 = 1.0 / den
    w1 = e2 / den

    sel = jnp.where(is0, 1.0, jnp.where(is1, 1.0, 0.0))
    tr = lax.broadcasted_iota(jnp.int32, (tm, tm), 0)
    tc = lax.broadcasted_iota(jnp.int32, (tm, tm), 1)
    before = jnp.where(tr < tc, 1.0, 0.0).astype(BF16)
    rank = jnp.dot(sel.astype(BF16), before, preferred_element_type=F32).astype(jnp.int32)
    cnt = jnp.sum(sel, axis=1, keepdims=True).astype(jnp.int32)
    cpad = jnp.broadcast_to(((cnt + (SEG_ALIGN - 1)) // SEG_ALIGN) * SEG_ALIGN, (E_PAD, tm))
    inc = cpad
    d = 1
    while d < E_PAD:
        inc = inc + jnp.where(eidx >= d, pltpu.roll(inc, d, 0), 0)
        d *= 2
    segstart = inc - cpad
    dest = segstart + rank
    d0 = jnp.sum(jnp.where(is0, dest, 0), axis=0, keepdims=True)
    d1 = jnp.sum(jnp.where(is1, dest, 0), axis=0, keepdims=True)
    seg_ref[0] = segstart[:, :LANES]
    cnt_ref[0] = cpad[:, :LANES]

    srow = lax.broadcasted_iota(jnp.int32, (LANES, tm), 0)
    stack = jnp.where(srow == 0, w0, jnp.where(srow == 1, w1, jnp.where(
        srow == 2, d0.astype(F32), jnp.where(srow == 3, d1.astype(F32), 0.0))))
    tab = stack.T
    tab_ref[...] = tab

    rr = lax.broadcasted_iota(jnp.int32, (xs_ref.shape[1], tm), 0)
    p0 = jnp.where(rr == d0, 1.0, 0.0).astype(BF16)
    p1 = jnp.where(rr == d1, 1.0, 0.0).astype(BF16)
    xs_ref[0, :, :D_MODEL] = jnp.dot(p0 + p1, h2_hi, preferred_element_type=F32).astype(BF16)
    lane = lax.broadcasted_iota(jnp.int32, (tm, LANES), 1)

    def gate_cols(col):
        hi, mid, lo = (v.astype(F32) for v in
                       _split3(jnp.broadcast_to(tab[:, col:col + 1], (tm, LANES))))
        return jnp.where(lane == 0, hi, jnp.where(lane == 1, mid, jnp.where(
            lane == 2, lo, 0.0))).astype(BF16)

    grows = (jnp.dot(p0, gate_cols(0), preferred_element_type=F32)
             + jnp.dot(p1, gate_cols(1), preferred_element_type=F32))
    xs_ref[0, :, D_MODEL:] = grows.astype(BF16)


def _mix_out_route(x2, lru, att, p, tm):
    T = x2.shape[0]
    nt = T // tm
    rows = 2 * tm + N_EXPERTS * SEG_ALIGN
    row = lambda i: (i, 0)
    blk3 = lambda i: (i, 0, 0)
    return pl.pallas_call(
        functools.partial(_mix_out_route_kernel, tm=tm),
        grid=(nt,),
        in_specs=[
            pl.BlockSpec((tm, D_MODEL), row),
            pl.BlockSpec((tm, LRU_WIDTH), row),
            pl.BlockSpec((tm, N_HEADS * V_HEAD), row),
            _full((D_MODEL, D_MODEL)), _full((1, D_MODEL)),
            _full((E_PAD, D_MODEL)), _full((E_PAD, D_MODEL)),
        ],
        out_specs=[
            pl.BlockSpec((tm, D_MODEL), row),
            pl.BlockSpec((1, rows, XS_COLS), blk3),
            pl.BlockSpec((tm, LANES), row),
            pl.BlockSpec((1, E_PAD, LANES), blk3),
            pl.BlockSpec((1, E_PAD, LANES), blk3),
        ],
        out_shape=[
            jax.ShapeDtypeStruct((T, D_MODEL), F32),
            jax.ShapeDtypeStruct((nt, rows, XS_COLS), BF16),
            jax.ShapeDtypeStruct((T, LANES), F32),
            jax.ShapeDtypeStruct((nt, E_PAD, LANES), jnp.int32),
            jax.ShapeDtypeStruct((nt, E_PAD, LANES), jnp.int32),
        ],
        compiler_params=_cparams("arbitrary"),
        name="mix_out_route",
    )(x2, lru, att, p["w_out"], p["g_ffn"], p["w_router_hi"], p["w_router_lo"])


def _ffn_dense_kernel(*refs, fc, n_cast):
    x_ref, lru_ref, att_ref, wo_ref, gffn_ref, wg_ref, wu_ref, wd_ref = refs[:8]
    srcs = refs[8:8 + n_cast]
    o_ref = refs[8 + n_cast]
    dsts = refs[9 + n_cast:9 + 2 * n_cast]
    in_bufs = refs[9 + 2 * n_cast:9 + 3 * n_cast]
    out_bufs = refs[9 + 3 * n_cast:9 + 4 * n_cast]
    in_sem, out_sem = refs[9 + 4 * n_cast:]
    step = pl.program_id(0)
    nsteps = pl.num_programs(0)
    slot = lax.rem(step, 2)

    def chunk_rows(c, kk):
        r = in_bufs[kk].shape[1]
        return pl.ds(pl.multiple_of(c * r, r), r)

    def in_copy(c, sl, kk):
        return pltpu.make_async_copy(srcs[kk].at[chunk_rows(c, kk), :], in_bufs[kk].at[sl], in_sem.at[sl])

    def out_copy(c, sl, kk):
        return pltpu.make_async_copy(out_bufs[kk].at[sl], dsts[kk].at[chunk_rows(c, kk), :], out_sem.at[sl])

    if n_cast:
        @pl.when(step == 0)
        def _():
            for kk in range(n_cast):
                in_copy(step, slot, kk).start()

        for kk in range(n_cast):
            in_copy(step, slot, kk).wait()

        @pl.when(step + 1 < nsteps)
        def _():
            for kk in range(n_cast):
                in_copy(step + 1, 1 - slot, kk).start()

        @pl.when(step >= 2)
        def _():
            for kk in range(n_cast):
                out_copy(step - 2, slot, kk).wait()

        for kk in range(n_cast):
            out_bufs[kk][slot] = in_bufs[kk][slot].astype(BF16)

    mixed = jnp.concatenate([lru_ref[...], att_ref[...]], axis=1)
    acc = x_ref[...] + jnp.dot(mixed, wo_ref[...], preferred_element_type=F32)
    h2 = _rms(acc, gffn_ref[...]).astype(BF16)
    for c in range(D_FF_DENSE // fc):
        g = jnp.dot(h2, wg_ref[:, c * fc:(c + 1) * fc], preferred_element_type=F32)
        u = jnp.dot(h2, wu_ref[:, c * fc:(c + 1) * fc], preferred_element_type=F32)
        act = (g * jax.nn.sigmoid(g) * u).astype(BF16)
        acc = acc + jnp.dot(act, wd_ref[c * fc:(c + 1) * fc, :], preferred_element_type=F32)
    o_ref[...] = acc

    if n_cast:
        for kk in range(n_cast):
            out_copy(step, slot, kk).start()

        @pl.when(step == nsteps - 1)
        def _():
            for kk in range(n_cast):
                out_copy(step - 1, 1 - slot, kk).wait()
                out_copy(step, slot, kk).wait()


def _ffn_dense(x2, lru, att, p, tm, to_cast):
    T = x2.shape[0]
    nsteps = T // tm
    assert nsteps >= 2 or not to_cast
    row = lambda i: (i, 0)
    once = pl.Buffered(1)
    n_cast = len(to_cast)
    chunks = [(a.shape[0] // nsteps, a.shape[1]) for a in to_cast]
    assert all(a.shape[0] % (nsteps * SEG_ALIGN) == 0 for a in to_cast)
    anyspec = pl.BlockSpec(memory_space=pl.ANY)
    outs = pl.pallas_call(
        functools.partial(_ffn_dense_kernel, fc=256, n_cast=n_cast),
        grid=(nsteps,),
        in_specs=[
            pl.BlockSpec((tm, D_MODEL), row),
            pl.BlockSpec((tm, LRU_WIDTH), row),
            pl.BlockSpec((tm, N_HEADS * V_HEAD), row),
            pl.BlockSpec((D_MODEL, D_MODEL), lambda i: (0, 0), pipeline_mode=once),
            _full((1, D_MODEL)),
            pl.BlockSpec((D_MODEL, D_FF_DENSE), lambda i: (0, 0), pipeline_mode=once),
            pl.BlockSpec((D_MODEL, D_FF_DENSE), lambda i: (0, 0), pipeline_mode=once),
            pl.BlockSpec((D_FF_DENSE, D_MODEL), lambda i: (0, 0), pipeline_mode=once),
        ] + [anyspec] * n_cast,
        out_specs=[pl.BlockSpec((tm, D_MODEL), row)] + [anyspec] * n_cast,
        out_shape=[jax.ShapeDtypeStruct((T, D_MODEL), F32)]
        + [jax.ShapeDtypeStruct(a.shape, BF16) for a in to_cast],
        scratch_shapes=[pltpu.VMEM((2,) + c, F32) for c in chunks]
        + [pltpu.VMEM((2,) + c, BF16) for c in chunks]
        + [pltpu.SemaphoreType.DMA((2,)), pltpu.SemaphoreType.DMA((2,))],
        compiler_params=_cparams("arbitrary"),
        name="ffn_dense",
    )(x2, lru, att, p["w_out"], p["g_ffn"], p["w_gate_d"], p["w_up_d"], p["w_down_d"], *to_cast)
    return outs[0], outs[1:]


SEG_BITS = (512, 256, 128, 64, 32, 16)
TAIL_BITS = (128, 64, 32, 16)


def _moe_expert_kernel(seg_ref, cnt_ref, xs_hbm, wg_ref, wu_ref, wd_ref, y_hbm,
                       lhs_ref, ost_ref, zero_ref, in_sem, out_sem, zero_sem, *, gj, big, ch, fc, n_groups):
    k = pl.program_id(0) * n_groups + pl.program_id(1)
    nsteps = N_EXPERTS * n_groups
    slot = lax.rem(k, 2)

    def for_each_piece(step, fn):
        e_s = step // n_groups
        g_s = lax.rem(step, n_groups)
        off = jnp.int32(0)
        for s in range(gj):
            j = g_s * gj + s
            start = seg_ref[j * N_EXPERTS + e_s]
            c = cnt_ref[j * N_EXPERTS + e_s]
            pos = jnp.int32(0)
            for b in SEG_BITS:
                has = (c & b) != 0

                @pl.when(has)
                def _(j=j, src=start + pos, dst=off + pos, b=b):
                    fn(j, pl.multiple_of(src, SEG_ALIGN), pl.multiple_of(dst, SEG_ALIGN), b)

                pos = pos + jnp.where(has, b, 0)
            off = off + c
        return off

    def in_copy(sl):
        def fn(j, src, dst, b):
            return pltpu.make_async_copy(xs_hbm.at[j, pl.ds(src, b), :],
                                         lhs_ref.at[sl, pl.ds(dst, b), :], in_sem.at[sl])
        return fn

    def out_copy(j, src, dst, b):
        return pltpu.make_async_copy(ost_ref.at[pl.ds(dst, b), :],
                                     y_hbm.at[j, pl.ds(src, b), :], out_sem.at[0])

    def start_all(step, mk):
        return for_each_piece(step, lambda *a: mk(*a).start())

    def wait_all(step, mk):
        return for_each_piece(step, lambda *a: mk(*a).wait())

    def for_each_tail_piece(step, fn):
        g_s = lax.rem(step, n_groups)
        for s in range(gj):
            j = g_s * gj + s
            last = j * N_EXPERTS + N_EXPERTS - 1
            tail = seg_ref[last] + cnt_ref[last]
            c = y_hbm.shape[1] - tail
            pos = jnp.int32(0)
            for b in TAIL_BITS:
                has = (c & b) != 0

                @pl.when(has)
                def _(j=j, dst=tail + pos, b=b):
                    fn(j, pl.multiple_of(dst, SEG_ALIGN), b)

                pos = pos + jnp.where(has, b, 0)

    def zero_copy(j, dst, b):
        return pltpu.make_async_copy(zero_ref.at[pl.ds(0, b), :], y_hbm.at[j, pl.ds(dst, b), :],
                                     zero_sem.at[0])

    @pl.when(k == 0)
    def _():
        lhs_ref[...] = jnp.zeros_like(lhs_ref)
        zero_ref[...] = jnp.zeros_like(zero_ref)
        start_all(k, in_copy(slot))

    @pl.when(k < n_groups)
    def _():
        for_each_tail_piece(k, lambda *a: zero_copy(*a).start())

    n_rows = wait_all(k, in_copy(slot))

    @pl.when(k + 1 < nsteps)
    def _():
        start_all(k + 1, in_copy(1 - slot))

    @pl.when(k >= 1)
    def _():
        wait_all(k - 1, out_copy)

    def chunk(r0, size, fc):
        rows = lhs_ref[slot, pl.ds(r0, size), :]
        xrow = rows[:, :D_MODEL]
        gate = (rows[:, D_MODEL:D_MODEL + 1].astype(F32) + rows[:, D_MODEL + 1:D_MODEL + 2].astype(F32)
                + rows[:, D_MODEL + 2:D_MODEL + 3].astype(F32))
        y = jnp.zeros((size, D_MODEL), F32)
        for c in range(D_FF_EXPERT // fc):
            g = jnp.dot(xrow, wg_ref[0, :, c * fc:(c + 1) * fc], preferred_element_type=F32)
            u = jnp.dot(xrow, wu_ref[0, :, c * fc:(c + 1) * fc], preferred_element_type=F32)
            act = (g * jax.nn.sigmoid(g) * u).astype(BF16)
            y = y + jnp.dot(act, wd_ref[0, c * fc:(c + 1) * fc, :], preferred_element_type=F32)
        ost_ref[pl.ds(r0, size), :] = (y * gate).astype(BF16)

    n_big = n_rows // big

    def big_chunk(ci, carry):
        chunk(pl.multiple_of(ci * big, big), big, fc)
        return carry

    def small_chunk(ci, carry):
        chunk(pl.multiple_of(n_big * big + ci * ch, ch), ch, D_FF_EXPERT)
        return carry

    lax.fori_loop(0, n_big, big_chunk, 0)
    lax.fori_loop(0, (n_rows - n_big * big + ch - 1) // ch, small_chunk, 0)
    start_all(k, out_copy)

    @pl.when(k < n_groups)
    def _():
        for_each_tail_piece(k, lambda *a: zero_copy(*a).wait())

    @pl.when(k == nsteps - 1)
    def _():
        wait_all(k, out_copy)


def _moe_experts(xs, seg, cnt, p, gj, big, ch):
    nt, rows, _ = xs.shape
    n_groups = nt // gj
    cap = gj * (rows - N_EXPERTS * SEG_ALIGN) // 2 + ch
    wmap = lambda e, g, seg_r, cnt_r: (e, 0, 0)
    grid_spec = pltpu.PrefetchScalarGridSpec(
        num_scalar_prefetch=2,
        grid=(N_EXPERTS, n_groups),
        in_specs=[
            pl.BlockSpec(memory_space=pl.ANY),
            pl.BlockSpec((1, D_MODEL, D_FF_EXPERT), wmap),
            pl.BlockSpec((1, D_MODEL, D_FF_EXPERT), wmap),
            pl.BlockSpec((1, D_FF_EXPERT, D_MODEL), wmap),
        ],
        out_specs=pl.BlockSpec(memory_space=pl.ANY),
        scratch_shapes=[
            pltpu.VMEM((2, cap, XS_COLS), BF16),
            pltpu.VMEM((cap, D_MODEL), BF16),
            pltpu.VMEM((sum(TAIL_BITS), D_MODEL), BF16),
            pltpu.SemaphoreType.DMA((2,)),
            pltpu.SemaphoreType.DMA((1,)),
            pltpu.SemaphoreType.DMA((1,)),
        ],
    )
    assert N_EXPERTS * SEG_ALIGN <= sum(TAIL_BITS)
    return pl.pallas_call(
        functools.partial(_moe_expert_kernel, gj=gj, big=big, ch=ch, fc=256, n_groups=n_groups),
        grid_spec=grid_spec,
        out_shape=jax.ShapeDtypeStruct((nt, rows, D_MODEL), BF16),
        compiler_params=_cparams("arbitrary", "arbitrary"),
        name="moe_experts",
    )(seg, cnt, xs, p["w_gate_e"], p["w_up_e"], p["w_down_e"])


def _moe_combine_kernel(x_ref, y_ref, tab_ref, gfin_ref, o_ref, *, final_norm):
    tm = x_ref.shape[0]
    rows = y_ref.shape[1]
    d0 = tab_ref[:, 2:3].astype(jnp.int32)
    d1 = tab_ref[:, 3:4].astype(jnp.int32)
    rr = lax.broadcasted_iota(jnp.int32, (tm, rows), 1)
    pt = jnp.where(rr == d0, 1.0, jnp.where(rr == d1, 1.0, 0.0)).astype(BF16)
    out = x_ref[...] + jnp.dot(pt, y_ref[0], preferred_element_type=F32)
    if final_norm:
        out = _rms(out, gfin_ref[...])
    o_ref[...] = out


def _moe_combine(xn, y, tab, g_final, tm, final_norm):
    T = xn.shape[0]
    rows = y.shape[1]
    row = lambda i: (i, 0)
    return pl.pallas_call(
        functools.partial(_moe_combine_kernel, final_norm=final_norm),
        grid=(T // tm,),
        in_specs=[
            pl.BlockSpec((tm, D_MODEL), row),
            pl.BlockSpec((1, rows, D_MODEL), lambda i: (i, 0, 0)),
            pl.BlockSpec((tm, LANES), row),
            _full((1, D_MODEL)),
        ],
        out_specs=pl.BlockSpec((tm, D_MODEL), row),
        out_shape=jax.ShapeDtypeStruct((T, D_MODEL), F32),
        compiler_params=_cparams("arbitrary"),
        name="moe_combine",
    )(xn, y, tab, g_final)


def _rope_block(w_rope, swap):
    half = QK_ROPE // 2
    if swap:
        w_rope = jnp.concatenate([w_rope[:, half:], w_rope[:, :half]], axis=1)
    k = w_rope.shape[0]
    return jnp.concatenate([jnp.zeros((k, QK_NOPE), w_rope.dtype), w_rope,
                            jnp.zeros((k, HEAD_PAD - QK_NOPE - QK_ROPE), w_rope.dtype)], axis=1)


def _block_diag(w):
    eye = jnp.eye(LRU_BLOCKS, dtype=w.dtype)
    return jnp.einsum("ncd,nm->ncmd", w, eye).reshape(LRU_WIDTH, LRU_WIDTH)


def _layer_params(l, norm_mix, w_in, conv_w, conv_b, lru_wa, lru_ba, lru_wx, lru_bx, lru_lambda,
                  q_norm, w_uq, kv_norm, w_ukv, w_out, norm_ffn):
    scale = math.log2(math.e) / math.sqrt(QK_NOPE + QK_ROPE)
    wi = w_in[l]
    w_kr = wi[:, C_CKV + KV_LORA:]
    w_in_pad = jnp.concatenate([wi[:, :C_KR], _rope_block(w_kr, False), _rope_block(w_kr, True)],
                               axis=1).astype(BF16)
    wq = (w_uq[l] * scale).reshape(Q_LORA, N_HEADS, QK_NOPE + QK_ROPE)
    zpad = jnp.zeros((Q_LORA, N_HEADS, HEAD_PAD - QK_NOPE - QK_ROPE), F32)
    zn = jnp.zeros((Q_LORA, N_HEADS, QK_NOPE), F32)
    half = QK_ROPE // 2
    wq_a = jnp.concatenate([wq, zpad], axis=2).reshape(Q_LORA, N_HEADS * HEAD_PAD)
    wq_sw = jnp.concatenate([wq[:, :, QK_NOPE + half:], wq[:, :, QK_NOPE:QK_NOPE + half]], axis=2)
    wq_b = jnp.concatenate([zn, wq_sw, zpad], axis=2).reshape(Q_LORA, N_HEADS * HEAD_PAD)
    wkv = w_ukv[l].reshape(KV_LORA, N_HEADS, QK_NOPE + V_HEAD)
    wk = jnp.concatenate([wkv[:, :, :QK_NOPE], jnp.zeros((KV_LORA, N_HEADS, HEAD_PAD - QK_NOPE), F32)],
                         axis=2).reshape(KV_LORA, N_HEADS * HEAD_PAD)
    zv = jnp.zeros((KV_LORA, N_HEADS // 2, V_HEAD), F32)
    wv4 = wkv[:, :, QK_NOPE:].reshape(KV_LORA, N_HEADS // 2, 2, V_HEAD)
    wv = jnp.stack([wv4[:, :, 0], zv, zv, wv4[:, :, 1]], axis=2).reshape(KV_LORA, N_HEADS * HEAD_PAD)
    ones_pat = jnp.tile(jnp.concatenate([jnp.zeros((V_HEAD,), F32), jnp.ones((2 * V_HEAD,), F32),
                                         jnp.zeros((V_HEAD,), F32)]), N_HEADS // 2)[None, :]
    return {
        "g_mix": norm_mix[l][None, :],
        "w_in": w_in_pad,
        "conv_w": conv_w[l],
        "conv_b": conv_b[l][None, :],
        "w_gate": jnp.concatenate([_block_diag(lru_wa[l]), _block_diag(lru_wx[l])], axis=1).astype(BF16),
        "b_gate": jnp.concatenate([lru_ba[l].reshape(1, -1), lru_bx[l].reshape(1, -1)], axis=1),
        "lam": lru_lambda[l][None, :],
        "g_q": q_norm[l][None, :],
        "w_q": jnp.concatenate([wq_a, wq_b], axis=1).astype(BF16),
        "g_kv": kv_norm[l][None, :],
        "w_kv": jnp.concatenate([wk, wv], axis=1).astype(BF16),
        "v_ones": ones_pat,
        "w_out": w_out[l].astype(BF16),
        "g_ffn": norm_ffn[l][None, :],
    }


def kernel(x, positions, norm_mix, w_in, conv_w, conv_b, lru_wa, lru_ba, lru_wx, lru_bx, lru_lambda,
           q_norm, w_uq, kv_norm, w_ukv, w_out, norm_ffn, dense_w_gate, dense_w_up, dense_w_down,
           router_w, expert_w_gate, expert_w_up, expert_w_down, norm_final):
    B, S, _ = x.shape
    T = B * S
    depth = norm_mix.shape[0]
    ts = min(S, 256)
    tq = min(S, 512)
    tm = min(T, 512)
    ctab, stab = _rope_tables(positions)
    x2 = x.reshape(T, D_MODEL)
    experts_bf16 = {}
    for l in range(depth):
        p = _layer_params(l, norm_mix, w_in, conv_w, conv_b, lru_wa, lru_ba, lru_wx, lru_bx,
                          lru_lambda, q_norm, w_uq, kv_norm, w_ukv, w_out, norm_ffn)
        lru, q, k, v = _mix_in(x2, B, S, p, ctab, stab, ts, nb=2 if B % 2 == 0 else 1)
        att = _attention(q, k, v, B, S, tq, hpb=8)
        j = l // 2
        last = l == depth - 1
        if l % 2 == 0:
            p["w_gate_d"] = dense_w_gate[j].astype(BF16)
            p["w_up_d"] = dense_w_up[j].astype(BF16)
            p["w_down_d"] = dense_w_down[j].astype(BF16)
            to_cast = []
            if l + 1 < depth and T // tm >= 2:
                jn = (l + 1) // 2
                to_cast = [expert_w_gate[jn].reshape(-1, D_FF_EXPERT), expert_w_up[jn].reshape(-1, D_FF_EXPERT),
                           expert_w_down[jn].reshape(-1, D_MODEL)]
            x2, cast = _ffn_dense(x2, lru, att, p, tm, to_cast)
            if cast:
                experts_bf16[l + 1] = (cast[0].reshape(N_EXPERTS, D_MODEL, D_FF_EXPERT),
                                       cast[1].reshape(N_EXPERTS, D_MODEL, D_FF_EXPERT),
                                       cast[2].reshape(N_EXPERTS, D_FF_EXPERT, D_MODEL))
        else:
            wr = jnp.pad(router_w[j].T, ((0, E_PAD - N_EXPERTS), (0, 0)))
            p["w_router_hi"] = wr.astype(BF16)
            p["w_router_lo"] = (wr - p["w_router_hi"].astype(F32)).astype(BF16)
            xn, xs, tab, seg, cnt = _mix_out_route(x2, lru, att, p, tm)
            if l in experts_bf16:
                p["w_gate_e"], p["w_up_e"], p["w_down_e"] = experts_bf16[l]
            else:
                p["w_gate_e"] = expert_w_gate[j].astype(BF16)
                p["w_up_e"] = expert_w_up[j].astype(BF16)
                p["w_down_e"] = expert_w_down[j].astype(BF16)
            seg = seg[:, :N_EXPERTS, 0].reshape(-1)
            cnt = cnt[:, :N_EXPERTS, 0].reshape(-1)
            y = _moe_experts(xs, seg, cnt, p, gj=min(8, T // tm), big=512, ch=128)
            x2 = _moe_combine(xn, y, tab, norm_final[None, :], tm, final_norm=last)
    return x2.reshape(B, S, D_MODEL)
```

```python
import functools
import math

import jax
import jax.numpy as jnp
from jax import lax
from jax.experimental import pallas as pl
from jax.experimental.pallas import tpu as pltpu

D_MODEL = 1024
EPS = 1e-6
LRU_WIDTH = 512
LRU_BLOCKS = 8
LRU_BLOCK_W = 64
LRU_C = 8.0
CONV_W = 4
N_HEADS = 8
QK_NOPE = 64
QK_ROPE = 32
V_HEAD = 64
Q_LORA = 256
KV_LORA = 128
ROPE_THETA = 10000.0
N_EXPERTS = 8
D_FF_DENSE = 2816
D_FF_EXPERT = 1792

LANES = 128
HEAD_PAD = 128
C_XLRU = 0
C_GATE = LRU_WIDTH
C_CQ = 2 * LRU_WIDTH
C_CKV = C_CQ + Q_LORA
C_KR = C_CKV + KV_LORA
C_KRS = C_KR + HEAD_PAD
IN_COLS_PAD = C_KRS + HEAD_PAD

E_PAD = 16
SEG_ALIGN = 16
XS_COLS = D_MODEL + LANES

VMEM_LIMIT = 56 * 1024 * 1024

F32 = jnp.float32
BF16 = jnp.bfloat16


def _cparams(*sem):
    return pltpu.CompilerParams(dimension_semantics=sem, vmem_limit_bytes=VMEM_LIMIT)


def _rms(x, g):
    return x * lax.rsqrt(jnp.mean(x * x, axis=-1, keepdims=True) + EPS) * g


def _full(shape):
    nd = len(shape)
    return pl.BlockSpec(shape, lambda *_: (0,) * nd)


def _rope_kernel(pos_ref, inv_ref, sgn_ref, c_ref, s_ref):
    ang = pos_ref[...].astype(F32) * inv_ref[...]
    lane = lax.broadcasted_iota(jnp.int32, ang.shape, 1)
    c_ref[...] = jnp.where(lane < QK_NOPE, 1.0,
                           jnp.where(lane < QK_NOPE + QK_ROPE, jnp.cos(ang), 0.0))
    s_ref[...] = jnp.sin(ang) * sgn_ref[...]


def _rope_tables(positions):
    T = positions.size
    tt = min(T, 2048)
    half = QK_ROPE // 2
    inv = 1.0 / (ROPE_THETA ** (jnp.arange(half, dtype=F32) / half))
    zeros = jnp.zeros((QK_NOPE,), F32)
    inv128 = jnp.concatenate([zeros, inv, inv, jnp.zeros((32,), F32)])[None, :]
    sgn128 = jnp.concatenate([zeros, -jnp.ones((half,), F32), jnp.ones((half,), F32),
                              jnp.zeros((32,), F32)])[None, :]
    return pl.pallas_call(
        _rope_kernel,
        grid=(T // tt,),
        in_specs=[pl.BlockSpec((tt, 1), lambda i: (i, 0)), _full((1, LANES)), _full((1, LANES))],
        out_specs=[pl.BlockSpec((tt, LANES), lambda i: (i, 0))] * 2,
        out_shape=[jax.ShapeDtypeStruct((T, LANES), F32)] * 2,
        compiler_params=_cparams("arbitrary"),
        name="rope_tables",
    )(positions.reshape(T, 1), inv128, sgn128)


def _mix_in_proj(x_ref, gmix_ref, win_ref):
    h = _rms(x_ref[...], gmix_ref[...]).astype(BF16)
    return jnp.dot(h, win_ref[...], preferred_element_type=F32)


NLB = LRU_WIDTH // LANES


def _lru_conv(z, convw_ref, convb_ref, halo_ref, *, ts):
    nlb = NLB
    ng = ts // 8
    for c in range(nlb):
        halo_ref[c, 8:, :] = z[:, C_XLRU + c * LANES:C_XLRU + (c + 1) * LANES]

    def strided_rows(first):
        return jnp.concatenate([halo_ref[c, pl.ds(first, 8, stride=ng), :] for c in range(nlb)], axis=1)

    xc_parts = []
    for g in range(ng):
        acc = convb_ref[...]
        for kk in range(CONV_W):
            acc = acc + convw_ref[kk:kk + 1, :] * strided_rows(8 + g - (CONV_W - 1 - kk))
        xc_parts.append(acc)
    xc = jnp.concatenate(xc_parts, axis=0)
    halo_ref[:, 0:8, :] = halo_ref[:, ts:ts + 8, :]
    return xc


def _lru_scan(z, xc, gates, lam_ref, lru_ref, hcarry_ref, hbuf_ref, ab_ref, *, ts):
    si = pl.program_id(1)
    nlb = NLB
    ng = ts // 8
    nlam = -lam_ref[...]
    softplus = jnp.maximum(nlam, 0.0) + jnp.log1p(jnp.exp(-jnp.abs(nlam)))

    hrun = jnp.zeros((8, LRU_WIDTH), F32)
    prun = jnp.ones((8, LRU_WIDTH), F32)
    rb = 64
    for r0 in range(0, ts, rb):
        gts = gates[r0:r0 + rb, :]
        r = jax.nn.sigmoid(gts[:, :LRU_WIDTH])
        ig = jax.nn.sigmoid(gts[:, LRU_WIDTH:])
        log_a = -LRU_C * r * softplus
        a = jnp.exp(log_a)
        mult = jnp.sqrt(-jnp.tanh(log_a) * (a * a + 1.0))
        if r0 == 0:
            row = lax.broadcasted_iota(jnp.int32, (rb, LRU_WIDTH), 0)
            mult = jnp.where(row + si * ts == 0, 1.0, mult)
        b = mult * (ig * xc[r0:r0 + rb, :])
        ab_ref[0, r0:r0 + rb, :] = a
        ab_ref[1, r0:r0 + rb, :] = b
        for g in range(rb // 8):
            ag = a[g * 8:(g + 1) * 8, :]
            hrun = ag * hrun + b[g * 8:(g + 1) * 8, :]
            prun = ag * prun
    sub = lax.broadcasted_iota(jnp.int32, (8, LRU_WIDTH), 0)
    d = 1
    while d < 8:
        keep = sub >= d
        hrun = prun * jnp.where(keep, pltpu.roll(hrun, d, 0), 0.0) + hrun
        prun = prun * jnp.where(keep, pltpu.roll(prun, d, 0), 1.0)
        d *= 2
    block_end = hrun + prun * hcarry_ref[...]
    carry_in = jnp.where(sub >= 1, pltpu.roll(block_end, 1, 0), hcarry_ref[...])
    hcarry_ref[...] = block_end[7:8, :]
    hg = carry_in
    for g in range(ng):
        hg = ab_ref[0, g * 8:(g + 1) * 8, :] * hg + ab_ref[1, g * 8:(g + 1) * 8, :]
        for c in range(nlb):
            hbuf_ref[c, pl.ds(g, 8, stride=ng), :] = hg[:, c * LANES:(c + 1) * LANES]
    hseq = jnp.concatenate([hbuf_ref[c] for c in range(nlb)], axis=1)
    lru_ref[...] = (hseq * jax.nn.gelu(z[:, C_GATE:C_GATE + LRU_WIDTH])).astype(BF16)


def _qkv_project(z, gq_ref, wq_ref, gkv_ref, wkv_ref):
    hq = _rms(z[:, C_CQ:C_CQ + Q_LORA], gq_ref[...]).astype(BF16)
    qq = jnp.dot(hq, wq_ref[...], preferred_element_type=F32)
    hkv = _rms(z[:, C_CKV:C_CKV + KV_LORA], gkv_ref[...]).astype(BF16)
    kv = jnp.dot(hkv, wkv_ref[...], preferred_element_type=F32)
    return qq, kv


def _qkv_rotary_store(z, qq, kv, vones_ref, c_ref, s_ref, q_ref, k_ref, v_ref):
    cmul = c_ref[...]
    smul = s_ref[...]
    kr = z[:, C_KR:C_KR + HEAD_PAD] * cmul + z[:, C_KRS:C_KRS + HEAD_PAD] * smul
    nq = N_HEADS * HEAD_PAD
    for hh in range(N_HEADS):
        lo = hh * HEAD_PAD
        q_ref[:, lo:lo + HEAD_PAD] = (qq[:, lo:lo + HEAD_PAD] * cmul
                                      + qq[:, nq + lo:nq + lo + HEAD_PAD] * smul).astype(BF16)
        k_ref[lo:lo + HEAD_PAD, :] = (kv[:, lo:lo + HEAD_PAD] + kr).T.astype(BF16)
    v_ref[...] = (kv[:, nq:] + vones_ref[...]).astype(BF16)


def _mix_in_kernel(x_ref, gmix_ref, win_ref, convw_ref, convb_ref, wgate_ref, bgate_ref,
                   lam_ref, gq_ref, wq_ref, gkv_ref, wkv_ref, vones_ref, c_ref, s_ref,
                   lru_ref, q_ref, k_ref, v_ref, halo_ref, hcarry_ref, hbuf_ref, ab_ref, *, ts, nb):
    @pl.when(pl.program_id(1) == 0)
    def _():
        halo_ref[:, :, 0:8, :] = jnp.zeros((nb, halo_ref.shape[1], 8, LANES), F32)
        hcarry_ref[...] = jnp.zeros_like(hcarry_ref)

    seqs = range(nb)
    zs = [_mix_in_proj(x_ref.at[bb], gmix_ref, win_ref) for bb in seqs]
    xcs = [_lru_conv(zs[bb], convw_ref, convb_ref, halo_ref.at[bb], ts=ts) for bb in seqs]
    gates = [jnp.dot(xcs[bb].astype(BF16), wgate_ref[...], preferred_element_type=F32) + bgate_ref[...]
             for bb in seqs]
    qkvs = [_qkv_project(zs[bb], gq_ref, wq_ref, gkv_ref, wkv_ref) for bb in seqs]
    for bb in seqs:
        _lru_scan(zs[bb], xcs[bb], gates[bb], lam_ref, lru_ref.at[bb], hcarry_ref.at[bb],
                  hbuf_ref.at[bb], ab_ref.at[bb], ts=ts)
    for bb in seqs:
        _qkv_rotary_store(zs[bb], *qkvs[bb], vones_ref, c_ref.at[bb], s_ref.at[bb],
                          q_ref.at[bb], k_ref.at[bb], v_ref.at[bb])


def _mix_in(x2, B, S, p, ctab, stab, ts, nb):
    T = B * S
    ns = S // ts
    blk = lambda b, s: (b, s, 0)
    kern = functools.partial(_mix_in_kernel, ts=ts, nb=nb)
    nq = N_HEADS * HEAD_PAD
    nlb = LRU_WIDTH // LANES
    outs = pl.pallas_call(
        kern,
        grid=(B // nb, ns),
        in_specs=[
            pl.BlockSpec((nb, ts, D_MODEL), blk),
            _full((1, D_MODEL)), _full((D_MODEL, IN_COLS_PAD)),
            _full((CONV_W, LRU_WIDTH)), _full((1, LRU_WIDTH)),
            _full((LRU_WIDTH, 2 * LRU_WIDTH)), _full((1, 2 * LRU_WIDTH)),
            _full((1, LRU_WIDTH)),
            _full((1, Q_LORA)), _full((Q_LORA, 2 * nq)),
            _full((1, KV_LORA)), _full((KV_LORA, 2 * nq)), _full((1, nq)),
            pl.BlockSpec((nb, ts, LANES), blk), pl.BlockSpec((nb, ts, LANES), blk),
        ],
        out_specs=[
            pl.BlockSpec((nb, ts, LRU_WIDTH), blk),
            pl.BlockSpec((nb, ts, nq), blk),
            pl.BlockSpec((nb, nq, ts), lambda b, s: (b, 0, s)),
            pl.BlockSpec((nb, ts, nq), blk),
        ],
        out_shape=[
            jax.ShapeDtypeStruct((B, S, LRU_WIDTH), BF16),
            jax.ShapeDtypeStruct((B, S, nq), BF16),
            jax.ShapeDtypeStruct((B, nq, S), BF16),
            jax.ShapeDtypeStruct((B, S, nq), BF16),
        ],
        scratch_shapes=[pltpu.VMEM((nb, nlb, ts + 8, LANES), F32),
                        pltpu.VMEM((nb, 1, LRU_WIDTH), F32),
                        pltpu.VMEM((nb, nlb, ts, LANES), F32),
                        pltpu.VMEM((nb, 2, ts, LRU_WIDTH), F32)],
        compiler_params=_cparams("arbitrary", "arbitrary"),
        name="mix_in",
    )(x2.reshape(B, S, D_MODEL), p["g_mix"], p["w_in"], p["conv_w"], p["conv_b"], p["w_gate"],
      p["b_gate"], p["lam"], p["g_q"], p["w_q"], p["g_kv"], p["w_kv"], p["v_ones"],
      ctab.reshape(B, S, LANES), stab.reshape(B, S, LANES))
    lru, q, kt, v = outs
    return lru.reshape(T, LRU_WIDTH), q.reshape(T, nq), kt, v.reshape(T, nq)


def _attn_kernel(*refs, tq, hpb, n_cast):
    q_ref, k_ref, v_ref = refs[:3]
    srcs = refs[3:3 + n_cast]
    o_ref = refs[3 + n_cast]
    dsts = refs[4 + n_cast:4 + 2 * n_cast]
    p_ref, m_ref, al_ref, acc_ref = refs[4 + 2 * n_cast:8 + 2 * n_cast]
    in_bufs = refs[8 + 2 * n_cast:8 + 3 * n_cast]
    out_bufs = refs[8 + 3 * n_cast:8 + 4 * n_cast]
    in_sem, out_sem = refs[8 + 4 * n_cast:]
    grid_step = ((pl.program_id(0) * pl.num_programs(1) + pl.program_id(1)) * pl.num_programs(2)
                 + pl.program_id(2))
    grid_steps = pl.num_programs(0) * pl.num_programs(1) * pl.num_programs(2)
    finish_cast = _side_cast(grid_step, grid_steps, srcs, dsts, in_bufs, out_bufs, in_sem, out_sem)
    i = pl.program_id(2)
    row = lax.broadcasted_iota(jnp.int32, (tq, tq), 0)
    col = lax.broadcasted_iota(jnp.int32, (tq, tq), 1)

    def scores(j, slot, masked):
        start = pl.multiple_of(j * tq, tq)
        for hh in range(hpb):
            lo = hh * HEAD_PAD
            s = jnp.dot(q_ref[:, lo:lo + HEAD_PAD], k_ref[lo:lo + HEAD_PAD, pl.ds(start, tq)],
                        preferred_element_type=F32)
            if masked:
                s = jnp.where(col <= row, s, -jnp.inf)
            m_old = m_ref[hh]
            m_new = jnp.maximum(m_old, jnp.max(s, axis=1, keepdims=True))
            al_ref[slot, hh] = jnp.exp2(m_old - m_new)
            m_ref[hh] = m_new
            p_ref[slot, hh] = jnp.exp2((s - jnp.tile(m_new, (1, tq // LANES))).astype(BF16))

    def weigh(j, slot):
        start = pl.multiple_of(j * tq, tq)
        for hh in range(hpb):
            acc_ref[hh] = al_ref[slot, hh] * acc_ref[hh] + jnp.dot(
                p_ref[slot, hh], v_ref[pl.ds(start, tq), hh * HEAD_PAD:(hh + 1) * HEAD_PAD],
                preferred_element_type=F32)

    m_ref[...] = jnp.full(m_ref.shape, -jnp.inf, F32)
    acc_ref[...] = jnp.zeros_like(acc_ref)
    odd = lax.rem(i, 2)
    first_pending = jnp.where(odd == 1, 0, i)

    @pl.when(odd == 1)
    def _():
        scores(i, 1, True)
        scores(0, 0, False)
        weigh(i, 1)

    @pl.when(odd == 0)
    def _():
        scores(i, 0, True)

    def pair(t, carry):
        a = 2 * t + odd
        scores(a, 1, False)
        weigh(jnp.where(t == 0, first_pending, a - 1), 0)
        scores(a + 1, 0, False)
        weigh(a, 1)
        return carry

    lax.fori_loop(0, i // 2, pair, 0)
    weigh(jnp.where(i < 2, first_pending, i - 1), 0)

    lane = lax.broadcasted_iota(jnp.int32, (tq, HEAD_PAD), 1)
    for pr in range(hpb // 2):
        even, odd = acc_ref[2 * pr], acc_ref[2 * pr + 1]
        out = jnp.where(lane < V_HEAD, even / pltpu.roll(even, V_HEAD, 1),
                        odd / pltpu.roll(odd, V_HEAD, 1))
        o_ref[:, pr * HEAD_PAD:(pr + 1) * HEAD_PAD] = out.astype(BF16)
    finish_cast()


def _attention(q, k, v, B, S, tq, hpb, to_cast):
    T = B * S
    nq = S // tq
    nsteps = B * (N_HEADS // hpb) * nq
    cast_in, cast_out, cast_shapes, cast_scratch = _side_cast_specs(to_cast, nsteps)
    outs = pl.pallas_call(
        functools.partial(_attn_kernel, tq=tq, hpb=hpb, n_cast=len(to_cast)),
        grid=(B, N_HEADS // hpb, nq),
        in_specs=[
            pl.BlockSpec((tq, hpb * HEAD_PAD), lambda b, h, i: (b * nq + i, h)),
            pl.BlockSpec((None, hpb * HEAD_PAD, S), lambda b, h, i: (b, h, 0)),
            pl.BlockSpec((S, hpb * HEAD_PAD), lambda b, h, i: (b, h)),
        ] + cast_in,
        out_specs=[pl.BlockSpec((tq, hpb * V_HEAD), lambda b, h, i: (b * nq + i, h))] + cast_out,
        out_shape=[jax.ShapeDtypeStruct((T, N_HEADS * V_HEAD), BF16)] + cast_shapes,
        scratch_shapes=[pltpu.VMEM((2, hpb, tq, tq), BF16),
                        pltpu.VMEM((hpb, tq, LANES), F32), pltpu.VMEM((2, hpb, tq, LANES), F32),
                        pltpu.VMEM((hpb, tq, HEAD_PAD), F32)] + cast_scratch,
        compiler_params=_cparams("arbitrary", "arbitrary", "arbitrary"),
        name="mla_attention",
    )(q, k, v, *to_cast)
    return outs[0], outs[1:]


def _split3(w):
    hi = w.astype(BF16)
    r1 = w - hi.astype(F32)
    mid = r1.astype(BF16)
    lo = (r1 - mid.astype(F32)).astype(BF16)
    return hi, mid, lo


def _mix_out_route_kernel(x_ref, lru_ref, att_ref, wo_ref, gffn_ref, wrh_ref, wrl_ref,
                          xo_ref, xs_ref, tab_ref, seg_ref, cnt_ref, *, tm):
    mixed = jnp.concatenate([lru_ref[...], att_ref[...]], axis=1)
    xn = x_ref[...] + jnp.dot(mixed, wo_ref[...], preferred_element_type=F32)
    xo_ref[...] = xn
    h2 = _rms(xn, gffn_ref[...])
    h2_hi = h2.astype(BF16)
    h2_lo = (h2 - h2_hi.astype(F32)).astype(BF16)

    nt = (((1,), (1,)), ((), ()))
    logits = (lax.dot_general(wrh_ref[...], h2_hi, nt, preferred_element_type=F32)
              + lax.dot_general(wrh_ref[...], h2_lo, nt, preferred_element_type=F32)
              + lax.dot_general(wrl_ref[...], h2_hi, nt, preferred_element_type=F32))
    eidx = lax.broadcasted_iota(jnp.int32, (E_PAD, tm), 0)
    logits = jnp.where(eidx < N_EXPERTS, logits, -jnp.inf)
    m1 = jnp.max(logits, axis=0, keepdims=True)
    i1 = jnp.min(jnp.where(logits == m1, eidx, E_PAD), axis=0, keepdims=True)
    is0 = eidx == i1
    rest = jnp.where(is0, -jnp.inf, logits)
    m2 = jnp.max(rest, axis=0, keepdims=True)
    i2 = jnp.min(jnp.where(rest == m2, eidx, E_PAD), axis=0, keepdims=True)
    is1 = eidx == i2
    e2 = jnp.exp(m2 - m1)
    den = 1.0 + e2
    w0 = 1.0 / den
    w1 = e2 / den

    sel = jnp.where(is0, 1.0, jnp.where(is1, 1.0, 0.0))
    tr = lax.broadcasted_iota(jnp.int32, (tm, tm), 0)
    tc = lax.broadcasted_iota(jnp.int32, (tm, tm), 1)
    before = jnp.where(tr < tc, 1.0, 0.0).astype(BF16)
    rank = jnp.dot(sel.astype(BF16), before, preferred_element_type=F32).astype(jnp.int32)
    cnt = jnp.sum(sel, axis=1, keepdims=True).astype(jnp.int32)
    cpad = jnp.broadcast_to(((cnt + (SEG_ALIGN - 1)) // SEG_ALIGN) * SEG_ALIGN, (E_PAD, tm))
    inc = cpad
    d = 1
    while d < E_PAD:
        inc = inc + jnp.where(eidx >= d, pltpu.roll(inc, d, 0), 0)
        d *= 2
    segstart = inc - cpad
    dest = segstart + rank
    d0 = jnp.sum(jnp.where(is0, dest, 0), axis=0, keepdims=True)
    d1 = jnp.sum(jnp.where(is1, dest, 0), axis=0, keepdims=True)
    seg_ref[0] = segstart[:, :LANES]
    cnt_ref[0] = cpad[:, :LANES]

    srow = lax.broadcasted_iota(jnp.int32, (LANES, tm), 0)
    stack = jnp.where(srow == 0, w0, jnp.where(srow == 1, w1, jnp.where(
        srow == 2, d0.astype(F32), jnp.where(srow == 3, d1.astype(F32), 0.0))))
    tab = stack.T
    tab_ref[...] = tab

    rr = lax.broadcasted_iota(jnp.int32, (xs_ref.shape[1], tm), 0)
    p0 = jnp.where(rr == d0, 1.0, 0.0).astype(BF16)
    p1 = jnp.where(rr == d1, 1.0, 0.0).astype(BF16)
    xs_ref[0, :, :D_MODEL] = jnp.dot(p0 + p1, h2_hi, preferred_element_type=F32).astype(BF16)
    lane = lax.broadcasted_iota(jnp.int32, (tm, LANES), 1)

    def gate_cols(col):
        hi, mid, lo = (v.astype(F32) for v in
                       _split3(jnp.broadcast_to(tab[:, col:col + 1], (tm, LANES))))
        return jnp.where(lane == 0, hi, jnp.where(lane == 1, mid, jnp.where(
            lane == 2, lo, 0.0))).astype(BF16)

    grows = (jnp.dot(p0, gate_cols(0), preferred_element_type=F32)
             + jnp.dot(p1, gate_cols(1), preferred_element_type=F32))
    xs_ref[0, :, D_MODEL:] = grows.astype(BF16)


def _mix_out_route(x2, lru, att, p, tm):
    T = x2.shape[0]
    nt = T // tm
    rows = 2 * tm + N_EXPERTS * SEG_ALIGN
    row = lambda i: (i, 0)
    blk3 = lambda i: (i, 0, 0)
    return pl.pallas_call(
        functools.partial(_mix_out_route_kernel, tm=tm),
        grid=(nt,),
        in_specs=[
            pl.BlockSpec((tm, D_MODEL), row),
            pl.BlockSpec((tm, LRU_WIDTH), row),
            pl.BlockSpec((tm, N_HEADS * V_HEAD), row),
            _full((D_MODEL, D_MODEL)), _full((1, D_MODEL)),
            _full((E_PAD, D_MODEL)), _full((E_PAD, D_MODEL)),
        ],
        out_specs=[
            pl.BlockSpec((tm, D_MODEL), row),
            pl.BlockSpec((1, rows, XS_COLS), blk3),
            pl.BlockSpec((tm, LANES), row),
            pl.BlockSpec((1, E_PAD, LANES), blk3),
            pl.BlockSpec((1, E_PAD, LANES), blk3),
        ],
        out_shape=[
            jax.ShapeDtypeStruct((T, D_MODEL), F32),
            jax.ShapeDtypeStruct((nt, rows, XS_COLS), BF16),
            jax.ShapeDtypeStruct((T, LANES), F32),
            jax.ShapeDtypeStruct((nt, E_PAD, LANES), jnp.int32),
            jax.ShapeDtypeStruct((nt, E_PAD, LANES), jnp.int32),
        ],
        compiler_params=_cparams("arbitrary"),
        name="mix_out_route",
    )(x2, lru, att, p["w_out"], p["g_ffn"], p["w_router_hi"], p["w_router_lo"])


def _side_cast(step, nsteps, srcs, dsts, in_bufs, out_bufs, in_sem, out_sem):
    n_cast = len(srcs)
    slot = lax.rem(step, 2)

    def chunk_rows(c, kk):
        r = in_bufs[kk].shape[1]
        return pl.ds(pl.multiple_of(c * r, r), r)

    def in_copy(c, sl, kk):
        return pltpu.make_async_copy(srcs[kk].at[chunk_rows(c, kk), :], in_bufs[kk].at[sl], in_sem.at[sl])

    def out_copy(c, sl, kk):
        return pltpu.make_async_copy(out_bufs[kk].at[sl], dsts[kk].at[chunk_rows(c, kk), :], out_sem.at[sl])

    def finish():
        for kk in range(n_cast):
            out_copy(step, slot, kk).start()

        @pl.when(step == nsteps - 1)
        def _():
            for kk in range(n_cast):
                out_copy(step - 1, 1 - slot, kk).wait()
                out_copy(step, slot, kk).wait()

    if n_cast:
        @pl.when(step == 0)
        def _():
            for kk in range(n_cast):
                in_copy(step, slot, kk).start()

        for kk in range(n_cast):
            in_copy(step, slot, kk).wait()

        @pl.when(step + 1 < nsteps)
        def _():
            for kk in range(n_cast):
                in_copy(step + 1, 1 - slot, kk).start()

        @pl.when(step >= 2)
        def _():
            for kk in range(n_cast):
                out_copy(step - 2, slot, kk).wait()

        for kk in range(n_cast):
            out_bufs[kk][slot] = in_bufs[kk][slot].astype(BF16)
    return finish


def _side_cast_specs(to_cast, nsteps):
    assert nsteps >= 2 or not to_cast
    assert all(a.shape[0] % (nsteps * SEG_ALIGN) == 0 for a in to_cast)
    chunks = [(a.shape[0] // nsteps, a.shape[1]) for a in to_cast]
    anyspec = pl.BlockSpec(memory_space=pl.ANY)
    scratch = ([pltpu.VMEM((2,) + c, F32) for c in chunks] + [pltpu.VMEM((2,) + c, BF16) for c in chunks]
               + [pltpu.SemaphoreType.DMA((2,)), pltpu.SemaphoreType.DMA((2,))])
    return ([anyspec] * len(to_cast), [anyspec] * len(to_cast),
            [jax.ShapeDtypeStruct(a.shape, BF16) for a in to_cast], scratch)


def _ffn_dense_kernel(*refs, fc, n_cast):
    x_ref, lru_ref, att_ref, wo_ref, gffn_ref, wg_ref, wu_ref, wd_ref = refs[:8]
    srcs = refs[8:8 + n_cast]
    o_ref = refs[8 + n_cast]
    dsts = refs[9 + n_cast:9 + 2 * n_cast]
    in_bufs = refs[9 + 2 * n_cast:9 + 3 * n_cast]
    out_bufs = refs[9 + 3 * n_cast:9 + 4 * n_cast]
    in_sem, out_sem = refs[9 + 4 * n_cast:]
    finish_cast = _side_cast(pl.program_id(0), pl.num_programs(0), srcs, dsts, in_bufs, out_bufs,
                             in_sem, out_sem)

    mixed = jnp.concatenate([lru_ref[...], att_ref[...]], axis=1)
    acc = x_ref[...] + jnp.dot(mixed, wo_ref[...], preferred_element_type=F32)
    h2 = _rms(acc, gffn_ref[...]).astype(BF16)
    for c in range(D_FF_DENSE // fc):
        g = jnp.dot(h2, wg_ref[:, c * fc:(c + 1) * fc], preferred_element_type=F32)
        u = jnp.dot(h2, wu_ref[:, c * fc:(c + 1) * fc], preferred_element_type=F32)
        act = (g * jax.nn.sigmoid(g) * u).astype(BF16)
        acc = acc + jnp.dot(act, wd_ref[c * fc:(c + 1) * fc, :], preferred_element_type=F32)
    o_ref[...] = acc
    finish_cast()


def _ffn_dense(x2, lru, att, p, tm, to_cast):
    T = x2.shape[0]
    nsteps = T // tm
    row = lambda i: (i, 0)
    once = pl.Buffered(1)
    n_cast = len(to_cast)
    cast_in, cast_out, cast_shapes, cast_scratch = _side_cast_specs(to_cast, nsteps)
    outs = pl.pallas_call(
        functools.partial(_ffn_dense_kernel, fc=256, n_cast=n_cast),
        grid=(nsteps,),
        in_specs=[
            pl.BlockSpec((tm, D_MODEL), row),
            pl.BlockSpec((tm, LRU_WIDTH), row),
            pl.BlockSpec((tm, N_HEADS * V_HEAD), row),
            pl.BlockSpec((D_MODEL, D_MODEL), lambda i: (0, 0), pipeline_mode=once),
            _full((1, D_MODEL)),
            pl.BlockSpec((D_MODEL, D_FF_DENSE), lambda i: (0, 0), pipeline_mode=once),
            pl.BlockSpec((D_MODEL, D_FF_DENSE), lambda i: (0, 0), pipeline_mode=once),
            pl.BlockSpec((D_FF_DENSE, D_MODEL), lambda i: (0, 0), pipeline_mode=once),
        ] + cast_in,
        out_specs=[pl.BlockSpec((tm, D_MODEL), row)] + cast_out,
        out_shape=[jax.ShapeDtypeStruct((T, D_MODEL), F32)] + cast_shapes,
        scratch_shapes=cast_scratch,
        compiler_params=_cparams("arbitrary"),
        name="ffn_dense",
    )(x2, lru, att, p["w_out"], p["g_ffn"], p["w_gate_d"], p["w_up_d"], p["w_down_d"], *to_cast)
    return outs[0], outs[1:]


SEG_BITS = (512, 256, 128, 64, 32, 16)
TAIL_BITS = (128, 64, 32, 16)


def _moe_expert_kernel(seg_ref, cnt_ref, xs_hbm, wg_ref, wu_ref, wd_ref, y_hbm,
                       lhs_ref, ost_ref, zero_ref, in_sem, out_sem, zero_sem, *, gj, big, ch, fc, n_groups):
    k = pl.program_id(0) * n_groups + pl.program_id(1)
    nsteps = N_EXPERTS * n_groups
    slot = lax.rem(k, 2)

    def for_each_piece(step, fn):
        e_s = step // n_groups
        g_s = lax.rem(step, n_groups)
        off = jnp.int32(0)
        for s in range(gj):
            j = g_s * gj + s
            start = seg_ref[j * N_EXPERTS + e_s]
            c = cnt_ref[j * N_EXPERTS + e_s]
            pos = jnp.int32(0)
            for b in SEG_BITS:
                has = (c & b) != 0

                @pl.when(has)
                def _(j=j, src=start + pos, dst=off + pos, b=b):
                    fn(j, pl.multiple_of(src, SEG_ALIGN), pl.multiple_of(dst, SEG_ALIGN), b)

                pos = pos + jnp.where(has, b, 0)
            off = off + c
        return off

    def in_copy(sl):
        def fn(j, src, dst, b):
            return pltpu.make_async_copy(xs_hbm.at[j, pl.ds(src, b), :],
                                         lhs_ref.at[sl, pl.ds(dst, b), :], in_sem.at[sl])
        return fn

    def out_copy(j, src, dst, b):
        return pltpu.make_async_copy(ost_ref.at[pl.ds(dst, b), :],
                                     y_hbm.at[j, pl.ds(src, b), :], out_sem.at[0])

    def start_all(step, mk):
        return for_each_piece(step, lambda *a: mk(*a).start())

    def wait_all(step, mk):
        return for_each_piece(step, lambda *a: mk(*a).wait())

    def for_each_tail_piece(step, fn):
        g_s = lax.rem(step, n_groups)
        for s in range(gj):
            j = g_s * gj + s
            last = j * N_EXPERTS + N_EXPERTS - 1
            tail = seg_ref[last] + cnt_ref[last]
            c = y_hbm.shape[1] - tail
            pos = jnp.int32(0)
            for b in TAIL_BITS:
                has = (c & b) != 0

                @pl.when(has)
                def _(j=j, dst=tail + pos, b=b):
                    fn(j, pl.multiple_of(dst, SEG_ALIGN), b)

                pos = pos + jnp.where(has, b, 0)

    def zero_copy(j, dst, b):
        return pltpu.make_async_copy(zero_ref.at[pl.ds(0, b), :], y_hbm.at[j, pl.ds(dst, b), :],
                                     zero_sem.at[0])

    @pl.when(k == 0)
    def _():
        lhs_ref[...] = jnp.zeros_like(lhs_ref)
        zero_ref[...] = jnp.zeros_like(zero_ref)
        start_all(k, in_copy(slot))

    @pl.when(k < n_groups)
    def _():
        for_each_tail_piece(k, lambda *a: zero_copy(*a).start())

    n_rows = wait_all(k, in_copy(slot))

    @pl.when(k + 1 < nsteps)
    def _():
        start_all(k + 1, in_copy(1 - slot))

    @pl.when(k >= 1)
    def _():
        wait_all(k - 1, out_copy)

    def chunk(r0, size, fc):
        rows = lhs_ref[slot, pl.ds(r0, size), :]
        xrow = rows[:, :D_MODEL]
        gate = (rows[:, D_MODEL:D_MODEL + 1].astype(F32) + rows[:, D_MODEL + 1:D_MODEL + 2].astype(F32)
                + rows[:, D_MODEL + 2:D_MODEL + 3].astype(F32))
        y = jnp.zeros((size, D_MODEL), F32)
        for c in range(D_FF_EXPERT // fc):
            g = jnp.dot(xrow, wg_ref[0, :, c * fc:(c + 1) * fc], preferred_element_type=F32)
            u = jnp.dot(xrow, wu_ref[0, :, c * fc:(c + 1) * fc], preferred_element_type=F32)
            act = (g * jax.nn.sigmoid(g) * u).astype(BF16)
            y = y + jnp.dot(act, wd_ref[0, c * fc:(c + 1) * fc, :], preferred_element_type=F32)
        ost_ref[pl.ds(r0, size), :] = (y * gate).astype(BF16)

    n_big = n_rows // big

    def big_chunk(ci, carry):
        chunk(pl.multiple_of(ci * big, big), big, fc)
        return carry

    def small_chunk(ci, carry):
        chunk(pl.multiple_of(n_big * big + ci * ch, ch), ch, D_FF_EXPERT)
        return carry

    lax.fori_loop(0, n_big, big_chunk, 0)
    lax.fori_loop(0, (n_rows - n_big * big + ch - 1) // ch, small_chunk, 0)
    start_all(k, out_copy)

    @pl.when(k < n_groups)
    def _():
        for_each_tail_piece(k, lambda *a: zero_copy(*a).wait())

    @pl.when(k == nsteps - 1)
    def _():
        wait_all(k, out_copy)


def _moe_experts(xs, seg, cnt, p, gj, big, ch):
    nt, rows, _ = xs.shape
    n_groups = nt // gj
    cap = gj * (rows - N_EXPERTS * SEG_ALIGN) // 2 + ch
    wmap = lambda e, g, seg_r, cnt_r: (e, 0, 0)
    grid_spec = pltpu.PrefetchScalarGridSpec(
        num_scalar_prefetch=2,
        grid=(N_EXPERTS, n_groups),
        in_specs=[
            pl.BlockSpec(memory_space=pl.ANY),
            pl.BlockSpec((1, D_MODEL, D_FF_EXPERT), wmap),
            pl.BlockSpec((1, D_MODEL, D_FF_EXPERT), wmap),
            pl.BlockSpec((1, D_FF_EXPERT, D_MODEL), wmap),
        ],
        out_specs=pl.BlockSpec(memory_space=pl.ANY),
        scratch_shapes=[
            pltpu.VMEM((2, cap, XS_COLS), BF16),
            pltpu.VMEM((cap, D_MODEL), BF16),
            pltpu.VMEM((sum(TAIL_BITS), D_MODEL), BF16),
            pltpu.SemaphoreType.DMA((2,)),
            pltpu.SemaphoreType.DMA((1,)),
            pltpu.SemaphoreType.DMA((1,)),
        ],
    )
    assert N_EXPERTS * SEG_ALIGN <= sum(TAIL_BITS)
    return pl.pallas_call(
        functools.partial(_moe_expert_kernel, gj=gj, big=big, ch=ch, fc=256, n_groups=n_groups),
        grid_spec=grid_spec,
        out_shape=jax.ShapeDtypeStruct((nt, rows, D_MODEL), BF16),
        compiler_params=_cparams("arbitrary", "arbitrary"),
        name="moe_experts",
    )(seg, cnt, xs, p["w_gate_e"], p["w_up_e"], p["w_down_e"])


def _moe_combine_kernel(x_ref, y_ref, tab_ref, gfin_ref, o_ref, *, final_norm):
    tm = x_ref.shape[0]
    rows = y_ref.shape[1]
    d0 = tab_ref[:, 2:3].astype(jnp.int32)
    d1 = tab_ref[:, 3:4].astype(jnp.int32)
    rr = lax.broadcasted_iota(jnp.int32, (tm, rows), 1)
    pt = jnp.where(rr == d0, 1.0, jnp.where(rr == d1, 1.0, 0.0)).astype(BF16)
    out = x_ref[...] + jnp.dot(pt, y_ref[0], preferred_element_type=F32)
    if final_norm:
        out = _rms(out, gfin_ref[...])
    o_ref[...] = out


def _moe_combine(xn, y, tab, g_final, tm, final_norm):
    T = xn.shape[0]
    rows = y.shape[1]
    row = lambda i: (i, 0)
    return pl.pallas_call(
        functools.partial(_moe_combine_kernel, final_norm=final_norm),
        grid=(T // tm,),
        in_specs=[
            pl.BlockSpec((tm, D_MODEL), row),
            pl.BlockSpec((1, rows, D_MODEL), lambda i: (i, 0, 0)),
            pl.BlockSpec((tm, LANES), row),
            _full((1, D_MODEL)),
        ],
        out_specs=pl.BlockSpec((tm, D_MODEL), row),
        out_shape=jax.ShapeDtypeStruct((T, D_MODEL), F32),
        compiler_params=_cparams("arbitrary"),
        name="moe_combine",
    )(xn, y, tab, g_final)


def _rope_block(w_rope, swap):
    half = QK_ROPE // 2
    if swap:
        w_rope = jnp.concatenate([w_rope[:, half:], w_rope[:, :half]], axis=1)
    k = w_rope.shape[0]
    return jnp.concatenate([jnp.zeros((k, QK_NOPE), w_rope.dtype), w_rope,
                            jnp.zeros((k, HEAD_PAD - QK_NOPE - QK_ROPE), w_rope.dtype)], axis=1)


def _block_diag(w):
    eye = jnp.eye(LRU_BLOCKS, dtype=w.dtype)
    return jnp.einsum("ncd,nm->ncmd", w, eye).reshape(LRU_WIDTH, LRU_WIDTH)


def _layer_params(l, norm_mix, w_in, conv_w, conv_b, lru_wa, lru_ba, lru_wx, lru_bx, lru_lambda,
                  q_norm, w_uq, kv_norm, w_ukv, w_out, norm_ffn):
    scale = math.log2(math.e) / math.sqrt(QK_NOPE + QK_ROPE)
    wi = w_in[l]
    w_kr = wi[:, C_CKV + KV_LORA:]
    w_in_pad = jnp.concatenate([wi[:, :C_KR], _rope_block(w_kr, False), _rope_block(w_kr, True)],
                               axis=1).astype(BF16)
    wq = (w_uq[l] * scale).reshape(Q_LORA, N_HEADS, QK_NOPE + QK_ROPE)
    zpad = jnp.zeros((Q_LORA, N_HEADS, HEAD_PAD - QK_NOPE - QK_ROPE), F32)
    zn = jnp.zeros((Q_LORA, N_HEADS, QK_NOPE), F32)
    half = QK_ROPE // 2
    wq_a = jnp.concatenate([wq, zpad], axis=2).reshape(Q_LORA, N_HEADS * HEAD_PAD)
    wq_sw = jnp.concatenate([wq[:, :, QK_NOPE + half:], wq[:, :, QK_NOPE:QK_NOPE + half]], axis=2)
    wq_b = jnp.concatenate([zn, wq_sw, zpad], axis=2).reshape(Q_LORA, N_HEADS * HEAD_PAD)
    wkv = w_ukv[l].reshape(KV_LORA, N_HEADS, QK_NOPE + V_HEAD)
    wk = jnp.concatenate([wkv[:, :, :QK_NOPE], jnp.zeros((KV_LORA, N_HEADS, HEAD_PAD - QK_NOPE), F32)],
                         axis=2).reshape(KV_LORA, N_HEADS * HEAD_PAD)
    zv = jnp.zeros((KV_LORA, N_HEADS // 2, V_HEAD), F32)
    wv4 = wkv[:, :, QK_NOPE:].reshape(KV_LORA, N_HEADS // 2, 2, V_HEAD)
    wv = jnp.stack([wv4[:, :, 0], zv, zv, wv4[:, :, 1]], axis=2).reshape(KV_LORA, N_HEADS * HEAD_PAD)
    ones_pat = jnp.tile(jnp.concatenate([jnp.zeros((V_HEAD,), F32), jnp.ones((2 * V_HEAD,), F32),
                                         jnp.zeros((V_HEAD,), F32)]), N_HEADS // 2)[None, :]
    return {
        "g_mix": norm_mix[l][None, :],
        "w_in": w_in_pad,
        "conv_w": conv_w[l],
        "conv_b": conv_b[l][None, :],
        "w_gate": jnp.concatenate([_block_diag(lru_wa[l]), _block_diag(lru_wx[l])], axis=1).astype(BF16),
        "b_gate": jnp.concatenate([lru_ba[l].reshape(1, -1), lru_bx[l].reshape(1, -1)], axis=1),
        "lam": lru_lambda[l][None, :],
        "g_q": q_norm[l][None, :],
        "w_q": jnp.concatenate([wq_a, wq_b], axis=1).astype(BF16),
        "g_kv": kv_norm[l][None, :],
        "w_kv": jnp.concatenate([wk, wv], axis=1).astype(BF16),
        "v_ones": ones_pat,
        "w_out": w_out[l].astype(BF16),
        "g_ffn": norm_ffn[l][None, :],
    }


def kernel(x, positions, norm_mix, w_in, conv_w, conv_b, lru_wa, lru_ba, lru_wx, lru_bx, lru_lambda,
           q_norm, w_uq, kv_norm, w_ukv, w_out, norm_ffn, dense_w_gate, dense_w_up, dense_w_down,
           router_w, expert_w_gate, expert_w_up, expert_w_down, norm_final):
    B, S, _ = x.shape
    T = B * S
    depth = norm_mix.shape[0]
    ts = min(S, 512)
    tq = min(S, 512)
    tm = min(T, 512)
    ctab, stab = _rope_tables(positions)
    x2 = x.reshape(T, D_MODEL)
    experts_bf16 = {}
    for l in range(depth):
        p = _layer_params(l, norm_mix, w_in, conv_w, conv_b, lru_wa, lru_ba, lru_wx, lru_bx,
                          lru_lambda, q_norm, w_uq, kv_norm, w_ukv, w_out, norm_ffn)
        lru, q, k, v = _mix_in(x2, B, S, p, ctab, stab, ts, nb=2 if B % 2 == 0 else 1)
        j = l // 2
        last = l == depth - 1
        dense = l % 2 == 0
        hpb = N_HEADS
        attn_steps = B * (S // tq) * (N_HEADS // hpb)
        cast_dense = dense and attn_steps >= 2 and D_MODEL % (attn_steps * SEG_ALIGN) == 0
        dense_f32 = [dense_w_gate[j], dense_w_up[j]] if cast_dense else []
        att, dense_bf16 = _attention(q, k, v, B, S, tq, hpb, dense_f32)
        if dense:
            if cast_dense:
                p["w_gate_d"], p["w_up_d"] = dense_bf16
            else:
                p["w_gate_d"] = dense_w_gate[j].astype(BF16)
                p["w_up_d"] = dense_w_up[j].astype(BF16)
            p["w_down_d"] = dense_w_down[j].astype(BF16)
            to_cast = []
            if l + 1 < depth and T // tm >= 2:
                jn = (l + 1) // 2
                to_cast = [expert_w_gate[jn].reshape(-1, D_FF_EXPERT), expert_w_up[jn].reshape(-1, D_FF_EXPERT),
                           expert_w_down[jn].reshape(-1, D_MODEL)]
            x2, cast = _ffn_dense(x2, lru, att, p, tm, to_cast)
            if cast:
                experts_bf16[l + 1] = (cast[0].reshape(N_EXPERTS, D_MODEL, D_FF_EXPERT),
                                       cast[1].reshape(N_EXPERTS, D_MODEL, D_FF_EXPERT),
                                       cast[2].reshape(N_EXPERTS, D_FF_EXPERT, D_MODEL))
        else:
            wr = jnp.pad(router_w[j].T, ((0, E_PAD - N_EXPERTS), (0, 0)))
            p["w_router_hi"] = wr.astype(BF16)
            p["w_router_lo"] = (wr - p["w_router_hi"].astype(F32)).astype(BF16)
            xn, xs, tab, seg, cnt = _mix_out_route(x2, lru, att, p, tm)
            if l in experts_bf16:
                p["w_gate_e"], p["w_up_e"], p["w_down_e"] = experts_bf16[l]
            else:
                p["w_gate_e"] = expert_w_gate[j].astype(BF16)
                p["w_up_e"] = expert_w_up[j].astype(BF16)
                p["w_down_e"] = expert_w_down[j].astype(BF16)
            seg = seg[:, :N_EXPERTS, 0].reshape(-1)
            cnt = cnt[:, :N_EXPERTS, 0].reshape(-1)
            y = _moe_experts(xs, seg, cnt, p, gj=min(8, T // tm), big=512, ch=128)
            x2 = _moe_combine(xn, y, tab, norm_final[None, :], tm, final_norm=last)
    return x2.reshape(B, S, D_MODEL)
```

```python
import functools
import math

import jax
import jax.numpy as jnp
from jax import lax
from jax.experimental import pallas as pl
from jax.experimental.pallas import tpu as pltpu

D_MODEL = 1024
EPS = 1e-6
LRU_WIDTH = 512
LRU_BLOCKS = 8
LRU_BLOCK_W = 64
LRU_C = 8.0
CONV_W = 4
N_HEADS = 8
QK_NOPE = 64
QK_ROPE = 32
V_HEAD = 64
Q_LORA = 256
KV_LORA = 128
ROPE_THETA = 10000.0
N_EXPERTS = 8
D_FF_DENSE = 2816
D_FF_EXPERT = 1792

LANES = 128
HEAD_PAD = 128
C_XLRU = 0
C_GATE = LRU_WIDTH
C_CQ = 2 * LRU_WIDTH
C_CKV = C_CQ + Q_LORA
C_KR = C_CKV + KV_LORA
C_KRS = C_KR + HEAD_PAD
IN_COLS_PAD = C_KRS + HEAD_PAD

E_PAD = 16
SEG_ALIGN = 16
XS_COLS = D_MODEL + LANES

VMEM_LIMIT = 56 * 1024 * 1024

F32 = jnp.float32
BF16 = jnp.bfloat16


def _cparams(*sem):
    return pltpu.CompilerParams(dimension_semantics=sem, vmem_limit_bytes=VMEM_LIMIT)


def _rms(x, g):
    return x * lax.rsqrt(jnp.mean(x * x, axis=-1, keepdims=True) + EPS) * g


def _full(shape):
    nd = len(shape)
    return pl.BlockSpec(shape, lambda *_: (0,) * nd)


def _rope_kernel(pos_ref, inv_ref, sgn_ref, c_ref, s_ref):
    ang = pos_ref[...].astype(F32) * inv_ref[...]
    lane = lax.broadcasted_iota(jnp.int32, ang.shape, 1)
    c_ref[...] = jnp.where(lane < QK_NOPE, 1.0,
                           jnp.where(lane < QK_NOPE + QK_ROPE, jnp.cos(ang), 0.0))
    s_ref[...] = jnp.sin(ang) * sgn_ref[...]


def _rope_tables(positions):
    T = positions.size
    tt = min(T, 2048)
    half = QK_ROPE // 2
    inv = 1.0 / (ROPE_THETA ** (jnp.arange(half, dtype=F32) / half))
    zeros = jnp.zeros((QK_NOPE,), F32)
    inv128 = jnp.concatenate([zeros, inv, inv, jnp.zeros((32,), F32)])[None, :]
    sgn128 = jnp.concatenate([zeros, -jnp.ones((half,), F32), jnp.ones((half,), F32),
                              jnp.zeros((32,), F32)])[None, :]
    return pl.pallas_call(
        _rope_kernel,
        grid=(T // tt,),
        in_specs=[pl.BlockSpec((tt, 1), lambda i: (i, 0)), _full((1, LANES)), _full((1, LANES))],
        out_specs=[pl.BlockSpec((tt, LANES), lambda i: (i, 0))] * 2,
        out_shape=[jax.ShapeDtypeStruct((T, LANES), F32)] * 2,
        compiler_params=_cparams("arbitrary"),
        name="rope_tables",
    )(positions.reshape(T, 1), inv128, sgn128)


def _mix_in_proj(x_ref, gmix_ref, win_ref):
    h = _rms(x_ref[...], gmix_ref[...]).astype(BF16)
    return jnp.dot(h, win_ref[...], preferred_element_type=F32)


NLB = LRU_WIDTH // LANES


def _lru_conv(z, convw_ref, convb_ref, halo_ref, *, ts):
    nlb = NLB
    ng = ts // 8
    for c in range(nlb):
        halo_ref[c, 8:, :] = z[:, C_XLRU + c * LANES:C_XLRU + (c + 1) * LANES]

    def strided_rows(first):
        return jnp.concatenate([halo_ref[c, pl.ds(first, 8, stride=ng), :] for c in range(nlb)], axis=1)

    xc_parts = []
    for g in range(ng):
        acc = convb_ref[...]
        for kk in range(CONV_W):
            acc = acc + convw_ref[kk:kk + 1, :] * strided_rows(8 + g - (CONV_W - 1 - kk))
        xc_parts.append(acc)
    xc = jnp.concatenate(xc_parts, axis=0)
    halo_ref[:, 0:8, :] = halo_ref[:, ts:ts + 8, :]
    return xc


def _lru_scan(z, xc, gates, lam_ref, lru_ref, hcarry_ref, hbuf_ref, ab_ref, *, ts):
    si = pl.program_id(1)
    nlb = NLB
    ng = ts // 8
    nlam = -lam_ref[...]
    softplus = jnp.maximum(nlam, 0.0) + jnp.log1p(jnp.exp(-jnp.abs(nlam)))

    hrun = jnp.zeros((8, LRU_WIDTH), F32)
    prun = jnp.ones((8, LRU_WIDTH), F32)
    rb = 64
    for r0 in range(0, ts, rb):
        gts = gates[r0:r0 + rb, :]
        r = jax.nn.sigmoid(gts[:, :LRU_WIDTH])
        ig = jax.nn.sigmoid(gts[:, LRU_WIDTH:])
        log_a = -LRU_C * r * softplus
        a = jnp.exp(log_a)
        mult = jnp.sqrt(-jnp.tanh(log_a) * (a * a + 1.0))
        if r0 == 0:
            row = lax.broadcasted_iota(jnp.int32, (rb, LRU_WIDTH), 0)
            mult = jnp.where(row + si * ts == 0, 1.0, mult)
        b = mult * (ig * xc[r0:r0 + rb, :])
        ab_ref[0, r0:r0 + rb, :] = a
        ab_ref[1, r0:r0 + rb, :] = b
        for g in range(rb // 8):
            ag = a[g * 8:(g + 1) * 8, :]
            hrun = ag * hrun + b[g * 8:(g + 1) * 8, :]
            prun = ag * prun
    sub = lax.broadcasted_iota(jnp.int32, (8, LRU_WIDTH), 0)
    d = 1
    while d < 8:
        keep = sub >= d
        hrun = prun * jnp.where(keep, pltpu.roll(hrun, d, 0), 0.0) + hrun
        prun = prun * jnp.where(keep, pltpu.roll(prun, d, 0), 1.0)
        d *= 2
    block_end = hrun + prun * hcarry_ref[...]
    carry_in = jnp.where(sub >= 1, pltpu.roll(block_end, 1, 0), hcarry_ref[...])
    hcarry_ref[...] = block_end[7:8, :]
    hg = carry_in
    for g in range(ng):
        hg = ab_ref[0, g * 8:(g + 1) * 8, :] * hg + ab_ref[1, g * 8:(g + 1) * 8, :]
        for c in range(nlb):
            hbuf_ref[c, pl.ds(g, 8, stride=ng), :] = hg[:, c * LANES:(c + 1) * LANES]
    hseq = jnp.concatenate([hbuf_ref[c] for c in range(nlb)], axis=1)
    lru_ref[...] = (hseq * jax.nn.gelu(z[:, C_GATE:C_GATE + LRU_WIDTH])).astype(BF16)


def _qkv_project(z, gq_ref, wq_ref, gkv_ref, wkv_ref):
    hq = _rms(z[:, C_CQ:C_CQ + Q_LORA], gq_ref[...]).astype(BF16)
    qq = jnp.dot(hq, wq_ref[...], preferred_element_type=F32)
    hkv = _rms(z[:, C_CKV:C_CKV + KV_LORA], gkv_ref[...]).astype(BF16)
    kv = jnp.dot(hkv, wkv_ref[...], preferred_element_type=F32)
    return qq, kv


def _qkv_rotary_store(z, qq, kv, vones_ref, c_ref, s_ref, q_ref, k_ref, v_ref):
    cmul = c_ref[...]
    smul = s_ref[...]
    kr = z[:, C_KR:C_KR + HEAD_PAD] * cmul + z[:, C_KRS:C_KRS + HEAD_PAD] * smul
    nq = N_HEADS * HEAD_PAD
    for hh in range(N_HEADS):
        lo = hh * HEAD_PAD
        q_ref[:, lo:lo + HEAD_PAD] = (qq[:, lo:lo + HEAD_PAD] * cmul
                                      + qq[:, nq + lo:nq + lo + HEAD_PAD] * smul).astype(BF16)
        k_ref[lo:lo + HEAD_PAD, :] = (kv[:, lo:lo + HEAD_PAD] + kr).T.astype(BF16)
    v_ref[...] = (kv[:, nq:] + vones_ref[...]).astype(BF16)


def _mix_in_kernel(x_ref, gmix_ref, win_ref, convw_ref, convb_ref, wgate_ref, bgate_ref,
                   lam_ref, gq_ref, wq_ref, gkv_ref, wkv_ref, vones_ref, c_ref, s_ref,
                   lru_ref, q_ref, k_ref, v_ref, halo_ref, hcarry_ref, hbuf_ref, ab_ref, *, ts, nb):
    @pl.when(pl.program_id(1) == 0)
    def _():
        halo_ref[:, :, 0:8, :] = jnp.zeros((nb, halo_ref.shape[1], 8, LANES), F32)
        hcarry_ref[...] = jnp.zeros_like(hcarry_ref)

    seqs = range(nb)
    zs = [_mix_in_proj(x_ref.at[bb], gmix_ref, win_ref) for bb in seqs]
    xcs = [_lru_conv(zs[bb], convw_ref, convb_ref, halo_ref.at[bb], ts=ts) for bb in seqs]
    gates = [jnp.dot(xcs[bb].astype(BF16), wgate_ref[...], preferred_element_type=F32) + bgate_ref[...]
             for bb in seqs]
    qkvs = [_qkv_project(zs[bb], gq_ref, wq_ref, gkv_ref, wkv_ref) for bb in seqs]
    for bb in seqs:
        _lru_scan(zs[bb], xcs[bb], gates[bb], lam_ref, lru_ref.at[bb], hcarry_ref.at[bb],
                  hbuf_ref.at[bb], ab_ref.at[bb], ts=ts)
    for bb in seqs:
        _qkv_rotary_store(zs[bb], *qkvs[bb], vones_ref, c_ref.at[bb], s_ref.at[bb],
                          q_ref.at[bb], k_ref.at[bb], v_ref.at[bb])


def _mix_in(x2, B, S, p, l, ctab, stab, ts, nb):
    T = B * S
    ns = S // ts
    blk = lambda b, s: (b, s, 0)
    kern = functools.partial(_mix_in_kernel, ts=ts, nb=nb)
    nq = N_HEADS * HEAD_PAD
    nlb = LRU_WIDTH // LANES
    layer_names = ("g_mix", "w_in", "conv_w", "conv_b", "w_gate", "b_gate", "lam", "g_q", "w_q", "g_kv", "w_kv")
    outs = pl.pallas_call(
        kern,
        grid=(B // nb, ns),
        in_specs=[pl.BlockSpec((nb, ts, D_MODEL), blk)]
        + [_layer_spec(p[name], l) for name in layer_names]
        + [_full((1, nq)), pl.BlockSpec((nb, ts, LANES), blk), pl.BlockSpec((nb, ts, LANES), blk)],
        out_specs=[
            pl.BlockSpec((nb, ts, LRU_WIDTH), blk),
            pl.BlockSpec((nb, ts, nq), blk),
            pl.BlockSpec((nb, nq, ts), lambda b, s: (b, 0, s)),
            pl.BlockSpec((nb, ts, nq), blk),
        ],
        out_shape=[
            jax.ShapeDtypeStruct((B, S, LRU_WIDTH), BF16),
            jax.ShapeDtypeStruct((B, S, nq), BF16),
            jax.ShapeDtypeStruct((B, nq, S), BF16),
            jax.ShapeDtypeStruct((B, S, nq), BF16),
        ],
        scratch_shapes=[pltpu.VMEM((nb, nlb, ts + 8, LANES), F32),
                        pltpu.VMEM((nb, 1, LRU_WIDTH), F32),
                        pltpu.VMEM((nb, nlb, ts, LANES), F32),
                        pltpu.VMEM((nb, 2, ts, LRU_WIDTH), F32)],
        compiler_params=_cparams("arbitrary", "arbitrary"),
        name="mix_in",
    )(x2.reshape(B, S, D_MODEL), *[p[name] for name in layer_names], p["v_ones"],
      ctab.reshape(B, S, LANES), stab.reshape(B, S, LANES))
    lru, q, kt, v = outs
    return lru.reshape(T, LRU_WIDTH), q.reshape(T, nq), kt, v.reshape(T, nq)


def _attn_kernel(*refs, tq, hpb, n_cast):
    q_ref, k_ref, v_ref = refs[:3]
    srcs = refs[3:3 + n_cast]
    o_ref = refs[3 + n_cast]
    dsts = refs[4 + n_cast:4 + 2 * n_cast]
    p_ref, m_ref, al_ref, acc_ref = refs[4 + 2 * n_cast:8 + 2 * n_cast]
    in_bufs = refs[8 + 2 * n_cast:8 + 3 * n_cast]
    out_bufs = refs[8 + 3 * n_cast:8 + 4 * n_cast]
    in_sem, out_sem = refs[8 + 4 * n_cast:]
    grid_step = ((pl.program_id(0) * pl.num_programs(1) + pl.program_id(1)) * pl.num_programs(2)
                 + pl.program_id(2))
    grid_steps = pl.num_programs(0) * pl.num_programs(1) * pl.num_programs(2)
    finish_cast = _side_cast(grid_step, grid_steps, srcs, dsts, in_bufs, out_bufs, in_sem, out_sem)
    i = pl.program_id(2)
    row = lax.broadcasted_iota(jnp.int32, (tq, tq), 0)
    col = lax.broadcasted_iota(jnp.int32, (tq, tq), 1)

    def scores(j, slot, masked):
        start = pl.multiple_of(j * tq, tq)
        for hh in range(hpb):
            lo = hh * HEAD_PAD
            s = jnp.dot(q_ref[:, lo:lo + HEAD_PAD], k_ref[lo:lo + HEAD_PAD, pl.ds(start, tq)],
                        preferred_element_type=F32)
            if masked:
                s = jnp.where(col <= row, s, -jnp.inf)
            m_old = m_ref[hh]
            m_new = jnp.maximum(m_old, jnp.max(s, axis=1, keepdims=True))
            al_ref[slot, hh] = jnp.exp2(m_old - m_new)
            m_ref[hh] = m_new
            p_ref[slot, hh] = jnp.exp2((s - jnp.tile(m_new, (1, tq // LANES))).astype(BF16))

    def weigh(j, slot):
        start = pl.multiple_of(j * tq, tq)
        for hh in range(hpb):
            acc_ref[hh] = al_ref[slot, hh] * acc_ref[hh] + jnp.dot(
                p_ref[slot, hh], v_ref[pl.ds(start, tq), hh * HEAD_PAD:(hh + 1) * HEAD_PAD],
                preferred_element_type=F32)

    m_ref[...] = jnp.full(m_ref.shape, -jnp.inf, F32)
    acc_ref[...] = jnp.zeros_like(acc_ref)
    odd = lax.rem(i, 2)
    first_pending = jnp.where(odd == 1, 0, i)

    @pl.when(odd == 1)
    def _():
        scores(i, 1, True)
        scores(0, 0, False)
        weigh(i, 1)

    @pl.when(odd == 0)
    def _():
        scores(i, 0, True)

    def pair(t, carry):
        a = 2 * t + odd
        scores(a, 1, False)
        weigh(jnp.where(t == 0, first_pending, a - 1), 0)
        scores(a + 1, 0, False)
        weigh(a, 1)
        return carry

    lax.fori_loop(0, i // 2, pair, 0)
    weigh(jnp.where(i < 2, first_pending, i - 1), 0)

    lane = lax.broadcasted_iota(jnp.int32, (tq, HEAD_PAD), 1)
    for pr in range(hpb // 2):
        even, odd = acc_ref[2 * pr], acc_ref[2 * pr + 1]
        out = jnp.where(lane < V_HEAD, even / pltpu.roll(even, V_HEAD, 1),
                        odd / pltpu.roll(odd, V_HEAD, 1))
        o_ref[:, pr * HEAD_PAD:(pr + 1) * HEAD_PAD] = out.astype(BF16)
    finish_cast()


def _attention(q, k, v, B, S, tq, hpb, to_cast):
    T = B * S
    nq = S // tq
    nsteps = B * (N_HEADS // hpb) * nq
    cast_in, cast_out, cast_shapes, cast_scratch = _side_cast_specs(to_cast, nsteps)
    outs = pl.pallas_call(
        functools.partial(_attn_kernel, tq=tq, hpb=hpb, n_cast=len(to_cast)),
        grid=(B, N_HEADS // hpb, nq),
        in_specs=[
            pl.BlockSpec((tq, hpb * HEAD_PAD), lambda b, h, i: (b * nq + i, h)),
            pl.BlockSpec((None, hpb * HEAD_PAD, S), lambda b, h, i: (b, h, 0)),
            pl.BlockSpec((S, hpb * HEAD_PAD), lambda b, h, i: (b, h)),
        ] + cast_in,
        out_specs=[pl.BlockSpec((tq, hpb * V_HEAD), lambda b, h, i: (b * nq + i, h))] + cast_out,
        out_shape=[jax.ShapeDtypeStruct((T, N_HEADS * V_HEAD), BF16)] + cast_shapes,
        scratch_shapes=[pltpu.VMEM((2, hpb, tq, tq), BF16),
                        pltpu.VMEM((hpb, tq, LANES), F32), pltpu.VMEM((2, hpb, tq, LANES), F32),
                        pltpu.VMEM((hpb, tq, HEAD_PAD), F32)] + cast_scratch,
        compiler_params=_cparams("arbitrary", "arbitrary", "arbitrary"),
        name="mla_attention",
    )(q, k, v, *to_cast)
    return outs[0], outs[1:]


def _split3(w):
    hi = w.astype(BF16)
    r1 = w - hi.astype(F32)
    mid = r1.astype(BF16)
    lo = (r1 - mid.astype(F32)).astype(BF16)
    return hi, mid, lo


def _mix_out_route_kernel(x_ref, lru_ref, att_ref, wo_ref, gffn_ref, wrh_ref, wrl_ref,
                          xo_ref, xs_ref, tab_ref, seg_ref, cnt_ref, *, tm):
    mixed = jnp.concatenate([lru_ref[...], att_ref[...]], axis=1)
    xn = x_ref[...] + jnp.dot(mixed, wo_ref[...], preferred_element_type=F32)
    xo_ref[...] = xn
    h2 = _rms(xn, gffn_ref[...])
    h2_hi = h2.astype(BF16)
    h2_lo = (h2 - h2_hi.astype(F32)).astype(BF16)

    nt = (((1,), (1,)), ((), ()))
    logits = (lax.dot_general(wrh_ref[...], h2_hi, nt, preferred_element_type=F32)
              + lax.dot_general(wrh_ref[...], h2_lo, nt, preferred_element_type=F32)
              + lax.dot_general(wrl_ref[...], h2_hi, nt, preferred_element_type=F32))
    eidx = lax.broadcasted_iota(jnp.int32, (E_PAD, tm), 0)
    logits = jnp.where(eidx < N_EXPERTS, logits, -jnp.inf)
    m1 = jnp.max(logits, axis=0, keepdims=True)
    i1 = jnp.min(jnp.where(logits == m1, eidx, E_PAD), axis=0, keepdims=True)
    is0 = eidx == i1
    rest = jnp.where(is0, -jnp.inf, logits)
    m2 = jnp.max(rest, axis=0, keepdims=True)
    i2 = jnp.min(jnp.where(rest == m2, eidx, E_PAD), axis=0, keepdims=True)
    is1 = eidx == i2
    e2 = jnp.exp(m2 - m1)
    den = 1.0 + e2
    w0 = 1.0 / den
    w1 = e2 / den

    sel = jnp.where(is0, 1.0, jnp.where(is1, 1.0, 0.0))
    tr = lax.broadcasted_iota(jnp.int32, (tm, tm), 0)
    tc = lax.broadcasted_iota(jnp.int32, (tm, tm), 1)
    before = jnp.where(tr < tc, 1.0, 0.0).astype(BF16)
    rank = jnp.dot(sel.astype(BF16), before, preferred_element_type=F32).astype(jnp.int32)
    cnt = jnp.sum(sel, axis=1, keepdims=True).astype(jnp.int32)
    cpad = jnp.broadcast_to(((cnt + (SEG_ALIGN - 1)) // SEG_ALIGN) * SEG_ALIGN, (E_PAD, tm))
    inc = cpad
    d = 1
    while d < E_PAD:
        inc = inc + jnp.where(eidx >= d, pltpu.roll(inc, d, 0), 0)
        d *= 2
    segstart = inc - cpad
    dest = segstart + rank
    d0 = jnp.sum(jnp.where(is0, dest, 0), axis=0, keepdims=True)
    d1 = jnp.sum(jnp.where(is1, dest, 0), axis=0, keepdims=True)
    seg_ref[0] = segstart[:, :LANES]
    cnt_ref[0] = cpad[:, :LANES]

    srow = lax.broadcasted_iota(jnp.int32, (LANES, tm), 0)
    stack = jnp.where(srow == 0, w0, jnp.where(srow == 1, w1, jnp.where(
        srow == 2, d0.astype(F32), jnp.where(srow == 3, d1.astype(F32), 0.0))))
    tab = stack.T
    tab_ref[...] = tab

    rr = lax.broadcasted_iota(jnp.int32, (xs_ref.shape[1], tm), 0)
    p0 = jnp.where(rr == d0, 1.0, 0.0).astype(BF16)
    p1 = jnp.where(rr == d1, 1.0, 0.0).astype(BF16)
    xs_ref[0, :, :D_MODEL] = jnp.dot(p0 + p1, h2_hi, preferred_element_type=F32).astype(BF16)
    lane = lax.broadcasted_iota(jnp.int32, (tm, LANES), 1)

    def gate_cols(col):
        hi, mid, lo = (v.astype(F32) for v in
                       _split3(jnp.broadcast_to(tab[:, col:col + 1], (tm, LANES))))
        return jnp.where(lane == 0, hi, jnp.where(lane == 1, mid, jnp.where(
            lane == 2, lo, 0.0))).astype(BF16)

    grows = (jnp.dot(p0, gate_cols(0), preferred_element_type=F32)
             + jnp.dot(p1, gate_cols(1), preferred_element_type=F32))
    xs_ref[0, :, D_MODEL:] = grows.astype(BF16)


def _mix_out_route(x2, lru, att, p, l, tm):
    T = x2.shape[0]
    nt = T // tm
    rows = 2 * tm + N_EXPERTS * SEG_ALIGN
    row = lambda i: (i, 0)
    blk3 = lambda i: (i, 0, 0)
    return pl.pallas_call(
        functools.partial(_mix_out_route_kernel, tm=tm),
        grid=(nt,),
        in_specs=[
            pl.BlockSpec((tm, D_MODEL), row),
            pl.BlockSpec((tm, LRU_WIDTH), row),
            pl.BlockSpec((tm, N_HEADS * V_HEAD), row),
            _layer_spec(p["w_out"], l), _layer_spec(p["g_ffn"], l),
            _full((E_PAD, D_MODEL)), _full((E_PAD, D_MODEL)),
        ],
        out_specs=[
            pl.BlockSpec((tm, D_MODEL), row),
            pl.BlockSpec((1, rows, XS_COLS), blk3),
            pl.BlockSpec((tm, LANES), row),
            pl.BlockSpec((1, E_PAD, LANES), blk3),
            pl.BlockSpec((1, E_PAD, LANES), blk3),
        ],
        out_shape=[
            jax.ShapeDtypeStruct((T, D_MODEL), F32),
            jax.ShapeDtypeStruct((nt, rows, XS_COLS), BF16),
            jax.ShapeDtypeStruct((T, LANES), F32),
            jax.ShapeDtypeStruct((nt, E_PAD, LANES), jnp.int32),
            jax.ShapeDtypeStruct((nt, E_PAD, LANES), jnp.int32),
        ],
        compiler_params=_cparams("arbitrary"),
        name="mix_out_route",
    )(x2, lru, att, p["w_out"], p["g_ffn"], p["w_router_hi"], p["w_router_lo"])


def _side_cast(step, nsteps, srcs, dsts, in_bufs, out_bufs, in_sem, out_sem):
    n_cast = len(srcs)
    slot = lax.rem(step, 2)

    def chunk_rows(c, kk):
        r = in_bufs[kk].shape[1]
        return pl.ds(pl.multiple_of(c * r, r), r)

    def in_copy(c, sl, kk):
        return pltpu.make_async_copy(srcs[kk].at[chunk_rows(c, kk), :], in_bufs[kk].at[sl], in_sem.at[sl])

    def out_copy(c, sl, kk):
        return pltpu.make_async_copy(out_bufs[kk].at[sl], dsts[kk].at[chunk_rows(c, kk), :], out_sem.at[sl])

    def finish():
        for kk in range(n_cast):
            out_copy(step, slot, kk).start()

        @pl.when(step == nsteps - 1)
        def _():
            for kk in range(n_cast):
                out_copy(step - 1, 1 - slot, kk).wait()
                out_copy(step, slot, kk).wait()

    if n_cast:
        @pl.when(step == 0)
        def _():
            for kk in range(n_cast):
                in_copy(step, slot, kk).start()

        for kk in range(n_cast):
            in_copy(step, slot, kk).wait()

        @pl.when(step + 1 < nsteps)
        def _():
            for kk in range(n_cast):
                in_copy(step + 1, 1 - slot, kk).start()

        @pl.when(step >= 2)
        def _():
            for kk in range(n_cast):
                out_copy(step - 2, slot, kk).wait()

        for kk in range(n_cast):
            out_bufs[kk][slot] = in_bufs[kk][slot].astype(BF16)
    return finish


def _side_cast_specs(to_cast, nsteps):
    assert nsteps >= 2 or not to_cast
    assert all(a.shape[0] % (nsteps * SEG_ALIGN) == 0 for a in to_cast)
    chunks = [(a.shape[0] // nsteps, a.shape[1]) for a in to_cast]
    anyspec = pl.BlockSpec(memory_space=pl.ANY)
    scratch = ([pltpu.VMEM((2,) + c, F32) for c in chunks] + [pltpu.VMEM((2,) + c, BF16) for c in chunks]
               + [pltpu.SemaphoreType.DMA((2,)), pltpu.SemaphoreType.DMA((2,))])
    return ([anyspec] * len(to_cast), [anyspec] * len(to_cast),
            [jax.ShapeDtypeStruct(a.shape, BF16) for a in to_cast], scratch)


def _ffn_dense_kernel(*refs, fc, n_cast):
    x_ref, lru_ref, att_ref, wo_ref, gffn_ref, wg_ref, wu_ref, wd_ref = refs[:8]
    srcs = refs[8:8 + n_cast]
    o_ref = refs[8 + n_cast]
    dsts = refs[9 + n_cast:9 + 2 * n_cast]
    in_bufs = refs[9 + 2 * n_cast:9 + 3 * n_cast]
    out_bufs = refs[9 + 3 * n_cast:9 + 4 * n_cast]
    in_sem, out_sem = refs[9 + 4 * n_cast:]
    finish_cast = _side_cast(pl.program_id(0), pl.num_programs(0), srcs, dsts, in_bufs, out_bufs,
                             in_sem, out_sem)

    mixed = jnp.concatenate([lru_ref[...], att_ref[...]], axis=1)
    acc = x_ref[...] + jnp.dot(mixed, wo_ref[...], preferred_element_type=F32)
    h2 = _rms(acc, gffn_ref[...]).astype(BF16)
    for c in range(D_FF_DENSE // fc):
        g = jnp.dot(h2, wg_ref[:, c * fc:(c + 1) * fc], preferred_element_type=F32)
        u = jnp.dot(h2, wu_ref[:, c * fc:(c + 1) * fc], preferred_element_type=F32)
        act = (g * jax.nn.sigmoid(g) * u).astype(BF16)
        acc = acc + jnp.dot(act, wd_ref[c * fc:(c + 1) * fc, :], preferred_element_type=F32)
    o_ref[...] = acc
    finish_cast()


def _ffn_dense(x2, lru, att, p, l, tm, to_cast):
    T = x2.shape[0]
    nsteps = T // tm
    row = lambda i: (i, 0)
    once = pl.Buffered(1)
    n_cast = len(to_cast)
    cast_in, cast_out, cast_shapes, cast_scratch = _side_cast_specs(to_cast, nsteps)
    outs = pl.pallas_call(
        functools.partial(_ffn_dense_kernel, fc=256, n_cast=n_cast),
        grid=(nsteps,),
        in_specs=[
            pl.BlockSpec((tm, D_MODEL), row),
            pl.BlockSpec((tm, LRU_WIDTH), row),
            pl.BlockSpec((tm, N_HEADS * V_HEAD), row),
            _layer_spec(p["w_out"], l, pipeline_mode=once),
            _layer_spec(p["g_ffn"], l),
            pl.BlockSpec((D_MODEL, D_FF_DENSE), lambda i: (0, 0), pipeline_mode=once),
            pl.BlockSpec((D_MODEL, D_FF_DENSE), lambda i: (0, 0), pipeline_mode=once),
            pl.BlockSpec((D_FF_DENSE, D_MODEL), lambda i: (0, 0), pipeline_mode=once),
        ] + cast_in,
        out_specs=[pl.BlockSpec((tm, D_MODEL), row)] + cast_out,
        out_shape=[jax.ShapeDtypeStruct((T, D_MODEL), F32)] + cast_shapes,
        scratch_shapes=cast_scratch,
        compiler_params=_cparams("arbitrary"),
        name="ffn_dense",
    )(x2, lru, att, p["w_out"], p["g_ffn"], p["w_gate_d"], p["w_up_d"], p["w_down_d"], *to_cast)
    return outs[0], outs[1:]


SEG_BITS = (512, 256, 128, 64, 32, 16)
TAIL_BITS = (128, 64, 32, 16)


def _moe_expert_kernel(seg_ref, cnt_ref, xs_hbm, wg_ref, wu_ref, wd_ref, y_hbm,
                       lhs_ref, ost_ref, zero_ref, in_sem, out_sem, zero_sem, *, gj, big, ch, fc, n_groups):
    k = pl.program_id(0) * n_groups + pl.program_id(1)
    nsteps = N_EXPERTS * n_groups
    slot = lax.rem(k, 2)

    def for_each_piece(step, fn):
        e_s = step // n_groups
        g_s = lax.rem(step, n_groups)
        off = jnp.int32(0)
        for s in range(gj):
            j = g_s * gj + s
            start = seg_ref[j * N_EXPERTS + e_s]
            c = cnt_ref[j * N_EXPERTS + e_s]
            pos = jnp.int32(0)
            for b in SEG_BITS:
                has = (c & b) != 0

                @pl.when(has)
                def _(j=j, src=start + pos, dst=off + pos, b=b):
                    fn(j, pl.multiple_of(src, SEG_ALIGN), pl.multiple_of(dst, SEG_ALIGN), b)

                pos = pos + jnp.where(has, b, 0)
            off = off + c
        return off

    def in_copy(sl):
        def fn(j, src, dst, b):
            return pltpu.make_async_copy(xs_hbm.at[j, pl.ds(src, b), :],
                                         lhs_ref.at[sl, pl.ds(dst, b), :], in_sem.at[sl])
        return fn

    def out_copy(j, src, dst, b):
        return pltpu.make_async_copy(ost_ref.at[pl.ds(dst, b), :],
                                     y_hbm.at[j, pl.ds(src, b), :], out_sem.at[0])

    def start_all(step, mk):
        return for_each_piece(step, lambda *a: mk(*a).start())

    def wait_all(step, mk):
        return for_each_piece(step, lambda *a: mk(*a).wait())

    def for_each_tail_piece(step, fn):
        g_s = lax.rem(step, n_groups)
        for s in range(gj):
            j = g_s * gj + s
            last = j * N_EXPERTS + N_EXPERTS - 1
            tail = seg_ref[last] + cnt_ref[last]
            c = y_hbm.shape[1] - tail
            pos = jnp.int32(0)
            for b in TAIL_BITS:
                has = (c & b) != 0

                @pl.when(has)
                def _(j=j, dst=tail + pos, b=b):
                    fn(j, pl.multiple_of(dst, SEG_ALIGN), b)

                pos = pos + jnp.where(has, b, 0)

    def zero_copy(j, dst, b):
        return pltpu.make_async_copy(zero_ref.at[pl.ds(0, b), :], y_hbm.at[j, pl.ds(dst, b), :],
                                     zero_sem.at[0])

    @pl.when(k == 0)
    def _():
        lhs_ref[...] = jnp.zeros_like(lhs_ref)
        zero_ref[...] = jnp.zeros_like(zero_ref)
        start_all(k, in_copy(slot))

    @pl.when(k < n_groups)
    def _():
        for_each_tail_piece(k, lambda *a: zero_copy(*a).start())

    n_rows = wait_all(k, in_copy(slot))

    @pl.when(k + 1 < nsteps)
    def _():
        start_all(k + 1, in_copy(1 - slot))

    @pl.when(k >= 1)
    def _():
        wait_all(k - 1, out_copy)

    def chunk(r0, size, fc):
        rows = lhs_ref[slot, pl.ds(r0, size), :]
        xrow = rows[:, :D_MODEL]
        gate = (rows[:, D_MODEL:D_MODEL + 1].astype(F32) + rows[:, D_MODEL + 1:D_MODEL + 2].astype(F32)
                + rows[:, D_MODEL + 2:D_MODEL + 3].astype(F32))
        y = jnp.zeros((size, D_MODEL), F32)
        for c in range(D_FF_EXPERT // fc):
            g = jnp.dot(xrow, wg_ref[0, :, c * fc:(c + 1) * fc], preferred_element_type=F32)
            u = jnp.dot(xrow, wu_ref[0, :, c * fc:(c + 1) * fc], preferred_element_type=F32)
            act = (g * jax.nn.sigmoid(g) * u).astype(BF16)
            y = y + jnp.dot(act, wd_ref[0, c * fc:(c + 1) * fc, :], preferred_element_type=F32)
        ost_ref[pl.ds(r0, size), :] = (y * gate).astype(BF16)

    n_big = n_rows // big

    def big_chunk(ci, carry):
        chunk(pl.multiple_of(ci * big, big), big, fc)
        return carry

    def small_chunk(ci, carry):
        chunk(pl.multiple_of(n_big * big + ci * ch, ch), ch, D_FF_EXPERT)
        return carry

    lax.fori_loop(0, n_big, big_chunk, 0)
    lax.fori_loop(0, (n_rows - n_big * big + ch - 1) // ch, small_chunk, 0)
    start_all(k, out_copy)

    @pl.when(k < n_groups)
    def _():
        for_each_tail_piece(k, lambda *a: zero_copy(*a).wait())

    @pl.when(k == nsteps - 1)
    def _():
        wait_all(k, out_copy)


def _moe_experts(xs, seg, cnt, p, gj, big, ch):
    nt, rows, _ = xs.shape
    n_groups = nt // gj
    cap = gj * (rows - N_EXPERTS * SEG_ALIGN) // 2 + ch
    wmap = lambda e, g, seg_r, cnt_r: (e, 0, 0)
    grid_spec = pltpu.PrefetchScalarGridSpec(
        num_scalar_prefetch=2,
        grid=(N_EXPERTS, n_groups),
        in_specs=[
            pl.BlockSpec(memory_space=pl.ANY),
            pl.BlockSpec((1, D_MODEL, D_FF_EXPERT), wmap),
            pl.BlockSpec((1, D_MODEL, D_FF_EXPERT), wmap),
            pl.BlockSpec((1, D_FF_EXPERT, D_MODEL), wmap),
        ],
        out_specs=pl.BlockSpec(memory_space=pl.ANY),
        scratch_shapes=[
            pltpu.VMEM((2, cap, XS_COLS), BF16),
            pltpu.VMEM((cap, D_MODEL), BF16),
            pltpu.VMEM((sum(TAIL_BITS), D_MODEL), BF16),
            pltpu.SemaphoreType.DMA((2,)),
            pltpu.SemaphoreType.DMA((1,)),
            pltpu.SemaphoreType.DMA((1,)),
        ],
    )
    assert N_EXPERTS * SEG_ALIGN <= sum(TAIL_BITS)
    return pl.pallas_call(
        functools.partial(_moe_expert_kernel, gj=gj, big=big, ch=ch, fc=256, n_groups=n_groups),
        grid_spec=grid_spec,
        out_shape=jax.ShapeDtypeStruct((nt, rows, D_MODEL), BF16),
        compiler_params=_cparams("arbitrary", "arbitrary"),
        name="moe_experts",
    )(seg, cnt, xs, p["w_gate_e"], p["w_up_e"], p["w_down_e"])


def _moe_combine_kernel(x_ref, y_ref, tab_ref, gfin_ref, o_ref, *, final_norm):
    tm = x_ref.shape[0]
    rows = y_ref.shape[1]
    d0 = tab_ref[:, 2:3].astype(jnp.int32)
    d1 = tab_ref[:, 3:4].astype(jnp.int32)
    rr = lax.broadcasted_iota(jnp.int32, (tm, rows), 1)
    pt = jnp.where(rr == d0, 1.0, jnp.where(rr == d1, 1.0, 0.0)).astype(BF16)
    out = x_ref[...] + jnp.dot(pt, y_ref[0], preferred_element_type=F32)
    if final_norm:
        out = _rms(out, gfin_ref[...])
    o_ref[...] = out


def _moe_combine(xn, y, tab, g_final, tm, final_norm):
    T = xn.shape[0]
    rows = y.shape[1]
    row = lambda i: (i, 0)
    return pl.pallas_call(
        functools.partial(_moe_combine_kernel, final_norm=final_norm),
        grid=(T // tm,),
        in_specs=[
            pl.BlockSpec((tm, D_MODEL), row),
            pl.BlockSpec((1, rows, D_MODEL), lambda i: (i, 0, 0)),
            pl.BlockSpec((tm, LANES), row),
            _full((1, D_MODEL)),
        ],
        out_specs=pl.BlockSpec((tm, D_MODEL), row),
        out_shape=jax.ShapeDtypeStruct((T, D_MODEL), F32),
        compiler_params=_cparams("arbitrary"),
        name="moe_combine",
    )(xn, y, tab, g_final)


def _rope_block(w_rope, swap):
    half = QK_ROPE // 2
    if swap:
        w_rope = jnp.concatenate([w_rope[..., half:], w_rope[..., :half]], axis=-1)
    lead = w_rope.shape[:-1]
    return jnp.concatenate([jnp.zeros(lead + (QK_NOPE,), w_rope.dtype), w_rope,
                            jnp.zeros(lead + (HEAD_PAD - QK_NOPE - QK_ROPE,), w_rope.dtype)], axis=-1)


def _block_diag(w):
    eye = jnp.eye(LRU_BLOCKS, dtype=w.dtype)
    return jnp.einsum("lncd,nm->lncmd", w, eye).reshape(w.shape[0], LRU_WIDTH, LRU_WIDTH)


def _stacked_params(norm_mix, w_in, conv_w, conv_b, lru_wa, lru_ba, lru_wx, lru_bx, lru_lambda,
                    q_norm, w_uq, kv_norm, w_ukv, w_out, norm_ffn):
    L = norm_mix.shape[0]
    scale = math.log2(math.e) / math.sqrt(QK_NOPE + QK_ROPE)
    w_kr = w_in[:, :, C_CKV + KV_LORA:]
    w_in_pad = jnp.concatenate([w_in[:, :, :C_KR], _rope_block(w_kr, False), _rope_block(w_kr, True)],
                               axis=2).astype(BF16)
    wq = (w_uq * scale).reshape(L, Q_LORA, N_HEADS, QK_NOPE + QK_ROPE)
    zpad = jnp.zeros((L, Q_LORA, N_HEADS, HEAD_PAD - QK_NOPE - QK_ROPE), F32)
    zn = jnp.zeros((L, Q_LORA, N_HEADS, QK_NOPE), F32)
    half = QK_ROPE // 2
    wq_a = jnp.concatenate([wq, zpad], axis=3).reshape(L, Q_LORA, N_HEADS * HEAD_PAD)
    wq_sw = jnp.concatenate([wq[..., QK_NOPE + half:], wq[..., QK_NOPE:QK_NOPE + half]], axis=3)
    wq_b = jnp.concatenate([zn, wq_sw, zpad], axis=3).reshape(L, Q_LORA, N_HEADS * HEAD_PAD)
    wkv = w_ukv.reshape(L, KV_LORA, N_HEADS, QK_NOPE + V_HEAD)
    wk = jnp.concatenate([wkv[..., :QK_NOPE], jnp.zeros((L, KV_LORA, N_HEADS, HEAD_PAD - QK_NOPE), F32)],
                         axis=3).reshape(L, KV_LORA, N_HEADS * HEAD_PAD)
    zv = jnp.zeros((L, KV_LORA, N_HEADS // 2, V_HEAD), F32)
    wv4 = wkv[..., QK_NOPE:].reshape(L, KV_LORA, N_HEADS // 2, 2, V_HEAD)
    wv = jnp.stack([wv4[:, :, :, 0], zv, zv, wv4[:, :, :, 1]], axis=3).reshape(L, KV_LORA, N_HEADS * HEAD_PAD)
    ones_pat = jnp.tile(jnp.concatenate([jnp.zeros((V_HEAD,), F32), jnp.ones((2 * V_HEAD,), F32),
                                         jnp.zeros((V_HEAD,), F32)]), N_HEADS // 2)[None, :]
    row = lambda a: a.reshape(L, 1, -1)
    return {
        "g_mix": row(norm_mix),
        "w_in": w_in_pad,
        "conv_w": conv_w,
        "conv_b": row(conv_b),
        "w_gate": jnp.concatenate([_block_diag(lru_wa), _block_diag(lru_wx)], axis=2).astype(BF16),
        "b_gate": jnp.concatenate([row(lru_ba), row(lru_bx)], axis=2),
        "lam": row(lru_lambda),
        "g_q": row(q_norm),
        "w_q": jnp.concatenate([wq_a, wq_b], axis=2).astype(BF16),
        "g_kv": row(kv_norm),
        "w_kv": jnp.concatenate([wk, wv], axis=2).astype(BF16),
        "v_ones": ones_pat,
        "w_out": w_out.astype(BF16),
        "g_ffn": row(norm_ffn),
    }


def _layer_spec(arr, l, **kw):
    return pl.BlockSpec((None,) + arr.shape[1:], lambda *_: (l, 0, 0), **kw)


def kernel(x, positions, norm_mix, w_in, conv_w, conv_b, lru_wa, lru_ba, lru_wx, lru_bx, lru_lambda,
           q_norm, w_uq, kv_norm, w_ukv, w_out, norm_ffn, dense_w_gate, dense_w_up, dense_w_down,
           router_w, expert_w_gate, expert_w_up, expert_w_down, norm_final):
    B, S, _ = x.shape
    T = B * S
    depth = norm_mix.shape[0]
    ts = min(S, 512)
    tq = min(S, 512)
    tm = min(T, 512)
    ctab, stab = _rope_tables(positions)
    x2 = x.reshape(T, D_MODEL)
    experts_bf16 = {}
    stacked = _stacked_params(norm_mix, w_in, conv_w, conv_b, lru_wa, lru_ba, lru_wx, lru_bx,
                              lru_lambda, q_norm, w_uq, kv_norm, w_ukv, w_out, norm_ffn)
    for l in range(depth):
        p = dict(stacked)
        lru, q, k, v = _mix_in(x2, B, S, p, l, ctab, stab, ts, nb=2 if B % 2 == 0 else 1)
        j = l // 2
        last = l == depth - 1
        dense = l % 2 == 0
        hpb = N_HEADS
        attn_steps = B * (S // tq) * (N_HEADS // hpb)
        cast_dense = dense and attn_steps >= 2 and D_MODEL % (attn_steps * SEG_ALIGN) == 0
        dense_f32 = [dense_w_gate[j], dense_w_up[j]] if cast_dense else []
        att, dense_bf16 = _attention(q, k, v, B, S, tq, hpb, dense_f32)
        if dense:
            if cast_dense:
                p["w_gate_d"], p["w_up_d"] = dense_bf16
            else:
                p["w_gate_d"] = dense_w_gate[j].astype(BF16)
                p["w_up_d"] = dense_w_up[j].astype(BF16)
            p["w_down_d"] = dense_w_down[j].astype(BF16)
            to_cast = []
            if l + 1 < depth and T // tm >= 2:
                jn = (l + 1) // 2
                to_cast = [expert_w_gate[jn].reshape(-1, D_FF_EXPERT), expert_w_up[jn].reshape(-1, D_FF_EXPERT),
                           expert_w_down[jn].reshape(-1, D_MODEL)]
            x2, cast = _ffn_dense(x2, lru, att, p, l, tm, to_cast)
            if cast:
                experts_bf16[l + 1] = (cast[0].reshape(N_EXPERTS, D_MODEL, D_FF_EXPERT),
                                       cast[1].reshape(N_EXPERTS, D_MODEL, D_FF_EXPERT),
                                       cast[2].reshape(N_EXPERTS, D_FF_EXPERT, D_MODEL))
        else:
            wr = jnp.pad(router_w[j].T, ((0, E_PAD - N_EXPERTS), (0, 0)))
            p["w_router_hi"] = wr.astype(BF16)
            p["w_router_lo"] = (wr - p["w_router_hi"].astype(F32)).astype(BF16)
            xn, xs, tab, seg, cnt = _mix_out_route(x2, lru, att, p, l, tm)
            if l in experts_bf16:
                p["w_gate_e"], p["w_up_e"], p["w_down_e"] = experts_bf16[l]
            else:
                p["w_gate_e"] = expert_w_gate[j].astype(BF16)
                p["w_up_e"] = expert_w_up[j].astype(BF16)
                p["w_down_e"] = expert_w_down[j].astype(BF16)
            seg = seg[:, :N_EXPERTS, 0].reshape(-1)
            cnt = cnt[:, :N_EXPERTS, 0].reshape(-1)
            y = _moe_experts(xs, seg, cnt, p, gj=min(8, T // tm), big=512, ch=128)
            x2 = _moe_combine(xn, y, tab, norm_final[None, :], tm, final_norm=last)
    return x2.reshape(B, S, D_MODEL)
```

```python
import functools
import math

import jax
import jax.numpy as jnp
from jax import lax
from jax.experimental import pallas as pl
from jax.experimental.pallas import tpu as pltpu

D_MODEL = 1024
EPS = 1e-6
LRU_WIDTH = 512
LRU_BLOCKS = 8
LRU_BLOCK_W = 64
LRU_C = 8.0
CONV_W = 4
N_HEADS = 8
QK_NOPE = 64
QK_ROPE = 32
V_HEAD = 64
Q_LORA = 256
KV_LORA = 128
ROPE_THETA = 10000.0
N_EXPERTS = 8
D_FF_DENSE = 2816
D_FF_EXPERT = 1792

LANES = 128
HEAD_PAD = 128
C_XLRU = 0
C_GATE = LRU_WIDTH
C_CQ = 2 * LRU_WIDTH
C_CKV = C_CQ + Q_LORA
C_KR = C_CKV + KV_LORA
C_KRS = C_KR + HEAD_PAD
IN_COLS_PAD = C_KRS + HEAD_PAD

E_PAD = 16
SEG_ALIGN = 16
XS_COLS = D_MODEL + LANES

VMEM_LIMIT = 56 * 1024 * 1024

F32 = jnp.float32
BF16 = jnp.bfloat16


def _cparams(*sem):
    return pltpu.CompilerParams(dimension_semantics=sem, vmem_limit_bytes=VMEM_LIMIT)


def _rms(x, g):
    return x * lax.rsqrt(jnp.mean(x * x, axis=-1, keepdims=True) + EPS) * g


def _full(shape):
    nd = len(shape)
    return pl.BlockSpec(shape, lambda *_: (0,) * nd)


ROPE_PACK = LANES // (QK_ROPE // 2)


def _rope_kernel(pos_ref, inv_ref, c_ref, s_ref):
    half = QK_ROPE // 2
    pos = pos_ref[...].astype(F32)
    lane = lax.broadcasted_iota(jnp.int32, (pos.shape[0], LANES), 1)
    spread = jnp.zeros((pos.shape[0], LANES), F32)
    for kk in range(ROPE_PACK):
        spread = jnp.where(lane // half == kk, pos[:, kk:kk + 1], spread)
    ang = spread * inv_ref[...]
    c_ref[...] = jnp.cos(ang)
    s_ref[...] = jnp.sin(ang)


def _rope_tables(positions):
    T = positions.size
    half = QK_ROPE // 2
    rows = T // ROPE_PACK
    inv = 1.0 / (ROPE_THETA ** (jnp.arange(half, dtype=F32) / half))
    cos_p, sin_p = pl.pallas_call(
        _rope_kernel,
        grid=(1,),
        in_specs=[_full((rows, ROPE_PACK)), _full((1, LANES))],
        out_specs=[_full((rows, LANES))] * 2,
        out_shape=[jax.ShapeDtypeStruct((rows, LANES), F32)] * 2,
        compiler_params=_cparams("arbitrary"),
        name="rope_tables",
    )(positions.reshape(rows, ROPE_PACK), jnp.tile(inv, ROPE_PACK)[None, :])
    cos16 = cos_p.reshape(T, half)
    sin16 = sin_p.reshape(T, half)
    ones = jnp.ones((T, QK_NOPE), F32)
    zeros = jnp.zeros((T, QK_NOPE), F32)
    pad = jnp.zeros((T, HEAD_PAD - QK_NOPE - QK_ROPE), F32)
    return (jnp.concatenate([ones, cos16, cos16, pad], axis=1),
            jnp.concatenate([zeros, -sin16, sin16, pad], axis=1))


def _mix_in_proj(x_ref, gmix_ref, win_ref):
    h = _rms(x_ref[...], gmix_ref[...]).astype(BF16)
    return jnp.dot(h, win_ref[...], preferred_element_type=F32)


NLB = LRU_WIDTH // LANES


def _lru_conv(z, convw_ref, convb_ref, halo_ref, *, ts):
    nlb = NLB
    ng = ts // 8
    for c in range(nlb):
        halo_ref[c, 8:, :] = z[:, C_XLRU + c * LANES:C_XLRU + (c + 1) * LANES]

    def strided_rows(first):
        return jnp.concatenate([halo_ref[c, pl.ds(first, 8, stride=ng), :] for c in range(nlb)], axis=1)

    xc_parts = []
    for g in range(ng):
        acc = convb_ref[...]
        for kk in range(CONV_W):
            acc = acc + convw_ref[kk:kk + 1, :] * strided_rows(8 + g - (CONV_W - 1 - kk))
        xc_parts.append(acc)
    xc = jnp.concatenate(xc_parts, axis=0)
    halo_ref[:, 0:8, :] = halo_ref[:, ts:ts + 8, :]
    return xc


def _lru_scan(z, xc, gates, lam_ref, lru_ref, hcarry_ref, hbuf_ref, ab_ref, *, ts):
    si = pl.program_id(1)
    nlb = NLB
    ng = ts // 8
    nlam = -lam_ref[...]
    softplus = jnp.maximum(nlam, 0.0) + jnp.log1p(jnp.exp(-jnp.abs(nlam)))

    hrun = jnp.zeros((8, LRU_WIDTH), F32)
    prun = jnp.ones((8, LRU_WIDTH), F32)
    rb = 64
    for r0 in range(0, ts, rb):
        gts = gates[r0:r0 + rb, :]
        r = jax.nn.sigmoid(gts[:, :LRU_WIDTH])
        ig = jax.nn.sigmoid(gts[:, LRU_WIDTH:])
        log_a = -LRU_C * r * softplus
        a = jnp.exp(log_a)
        mult = jnp.sqrt(-jnp.tanh(log_a) * (a * a + 1.0))
        if r0 == 0:
            row = lax.broadcasted_iota(jnp.int32, (rb, LRU_WIDTH), 0)
            mult = jnp.where(row + si * ts == 0, 1.0, mult)
        b = mult * (ig * xc[r0:r0 + rb, :])
        ab_ref[0, r0:r0 + rb, :] = a
        ab_ref[1, r0:r0 + rb, :] = b
        for g in range(rb // 8):
            ag = a[g * 8:(g + 1) * 8, :]
            hrun = ag * hrun + b[g * 8:(g + 1) * 8, :]
            prun = ag * prun
    sub = lax.broadcasted_iota(jnp.int32, (8, LRU_WIDTH), 0)
    d = 1
    while d < 8:
        keep = sub >= d
        hrun = prun * jnp.where(keep, pltpu.roll(hrun, d, 0), 0.0) + hrun
        prun = prun * jnp.where(keep, pltpu.roll(prun, d, 0), 1.0)
        d *= 2
    block_end = hrun + prun * hcarry_ref[...]
    carry_in = jnp.where(sub >= 1, pltpu.roll(block_end, 1, 0), hcarry_ref[...])
    hcarry_ref[...] = block_end[7:8, :]
    hg = carry_in
    for g in range(ng):
        hg = ab_ref[0, g * 8:(g + 1) * 8, :] * hg + ab_ref[1, g * 8:(g + 1) * 8, :]
        for c in range(nlb):
            hbuf_ref[c, pl.ds(g, 8, stride=ng), :] = hg[:, c * LANES:(c + 1) * LANES]
    hseq = jnp.concatenate([hbuf_ref[c] for c in range(nlb)], axis=1)
    lru_ref[...] = (hseq * jax.nn.gelu(z[:, C_GATE:C_GATE + LRU_WIDTH])).astype(BF16)


def _qkv_project(z, gq_ref, wq_ref, gkv_ref, wkv_ref):
    hq = _rms(z[:, C_CQ:C_CQ + Q_LORA], gq_ref[...]).astype(BF16)
    qq = jnp.dot(hq, wq_ref[...], preferred_element_type=F32)
    hkv = _rms(z[:, C_CKV:C_CKV + KV_LORA], gkv_ref[...]).astype(BF16)
    kv = jnp.dot(hkv, wkv_ref[...], preferred_element_type=F32)
    return qq, kv


def _qkv_rotary_store(z, qq, kv, vones_ref, c_ref, s_ref, q_ref, k_ref, v_ref):
    cmul = c_ref[...]
    smul = s_ref[...]
    kr = z[:, C_KR:C_KR + HEAD_PAD] * cmul + z[:, C_KRS:C_KRS + HEAD_PAD] * smul
    nq = N_HEADS * HEAD_PAD
    for hh in range(N_HEADS):
        lo = hh * HEAD_PAD
        q_ref[:, lo:lo + HEAD_PAD] = (qq[:, lo:lo + HEAD_PAD] * cmul
                                      + qq[:, nq + lo:nq + lo + HEAD_PAD] * smul).astype(BF16)
        k_ref[lo:lo + HEAD_PAD, :] = (kv[:, lo:lo + HEAD_PAD] + kr).T.astype(BF16)
    v_ref[...] = (kv[:, nq:] + vones_ref[...]).astype(BF16)


def _mix_in_kernel(x_ref, gmix_ref, win_ref, convw_ref, convb_ref, wgate_ref, bgate_ref,
                   lam_ref, gq_ref, wq_ref, gkv_ref, wkv_ref, vones_ref, c_ref, s_ref,
                   lru_ref, q_ref, k_ref, v_ref, halo_ref, hcarry_ref, hbuf_ref, ab_ref, *, ts, nb):
    @pl.when(pl.program_id(1) == 0)
    def _():
        halo_ref[:, :, 0:8, :] = jnp.zeros((nb, halo_ref.shape[1], 8, LANES), F32)
        hcarry_ref[...] = jnp.zeros_like(hcarry_ref)

    seqs = range(nb)
    zs = [_mix_in_proj(x_ref.at[bb], gmix_ref, win_ref) for bb in seqs]
    xcs = [_lru_conv(zs[bb], convw_ref, convb_ref, halo_ref.at[bb], ts=ts) for bb in seqs]
    gates = [jnp.dot(xcs[bb].astype(BF16), wgate_ref[...], preferred_element_type=F32) + bgate_ref[...]
             for bb in seqs]
    qkvs = [_qkv_project(zs[bb], gq_ref, wq_ref, gkv_ref, wkv_ref) for bb in seqs]
    for bb in seqs:
        _lru_scan(zs[bb], xcs[bb], gates[bb], lam_ref, lru_ref.at[bb], hcarry_ref.at[bb],
                  hbuf_ref.at[bb], ab_ref.at[bb], ts=ts)
    for bb in seqs:
        _qkv_rotary_store(zs[bb], *qkvs[bb], vones_ref, c_ref.at[bb], s_ref.at[bb],
                          q_ref.at[bb], k_ref.at[bb], v_ref.at[bb])


def _mix_in(x2, B, S, p, l, ctab, stab, ts, nb):
    T = B * S
    ns = S // ts
    blk = lambda b, s: (b, s, 0)
    kern = functools.partial(_mix_in_kernel, ts=ts, nb=nb)
    nq = N_HEADS * HEAD_PAD
    nlb = LRU_WIDTH // LANES
    layer_names = ("g_mix", "w_in", "conv_w", "conv_b", "w_gate", "b_gate", "lam", "g_q", "w_q", "g_kv", "w_kv")
    outs = pl.pallas_call(
        kern,
        grid=(B // nb, ns),
        in_specs=[pl.BlockSpec((nb, ts, D_MODEL), blk)]
        + [_layer_spec(p[name], l) for name in layer_names]
        + [_full((1, nq)), pl.BlockSpec((nb, ts, LANES), blk), pl.BlockSpec((nb, ts, LANES), blk)],
        out_specs=[
            pl.BlockSpec((nb, ts, LRU_WIDTH), blk),
            pl.BlockSpec((nb, ts, nq), blk),
            pl.BlockSpec((nb, nq, ts), lambda b, s: (b, 0, s)),
            pl.BlockSpec((nb, ts, nq), blk),
        ],
        out_shape=[
            jax.ShapeDtypeStruct((B, S, LRU_WIDTH), BF16),
            jax.ShapeDtypeStruct((B, S, nq), BF16),
            jax.ShapeDtypeStruct((B, nq, S), BF16),
            jax.ShapeDtypeStruct((B, S, nq), BF16),
        ],
        scratch_shapes=[pltpu.VMEM((nb, nlb, ts + 8, LANES), F32),
                        pltpu.VMEM((nb, 1, LRU_WIDTH), F32),
                        pltpu.VMEM((nb, nlb, ts, LANES), F32),
                        pltpu.VMEM((nb, 2, ts, LRU_WIDTH), F32)],
        compiler_params=_cparams("arbitrary", "arbitrary"),
        name="mix_in",
    )(x2.reshape(B, S, D_MODEL), *[p[name] for name in layer_names], p["v_ones"],
      ctab.reshape(B, S, LANES), stab.reshape(B, S, LANES))
    lru, q, kt, v = outs
    return lru.reshape(T, LRU_WIDTH), q.reshape(T, nq), kt, v.reshape(T, nq)


def _attn_kernel(*refs, tq, hpb, n_cast):
    q_ref, k_ref, v_ref = refs[:3]
    srcs = refs[3:3 + n_cast]
    o_ref = refs[3 + n_cast]
    dsts = refs[4 + n_cast:4 + 2 * n_cast]
    p_ref, m_ref, al_ref, acc_ref = refs[4 + 2 * n_cast:8 + 2 * n_cast]
    in_bufs = refs[8 + 2 * n_cast:8 + 3 * n_cast]
    out_bufs = refs[8 + 3 * n_cast:8 + 4 * n_cast]
    in_sem, out_sem = refs[8 + 4 * n_cast:]
    grid_step = ((pl.program_id(0) * pl.num_programs(1) + pl.program_id(1)) * pl.num_programs(2)
                 + pl.program_id(2))
    grid_steps = pl.num_programs(0) * pl.num_programs(1) * pl.num_programs(2)
    finish_cast = _side_cast(grid_step, grid_steps, srcs, dsts, in_bufs, out_bufs, in_sem, out_sem)
    i = pl.program_id(2)
    row = lax.broadcasted_iota(jnp.int32, (tq, tq), 0)
    col = lax.broadcasted_iota(jnp.int32, (tq, tq), 1)

    def scores(j, slot, masked):
        start = pl.multiple_of(j * tq, tq)
        for hh in range(hpb):
            lo = hh * HEAD_PAD
            s = jnp.dot(q_ref[:, lo:lo + HEAD_PAD], k_ref[lo:lo + HEAD_PAD, pl.ds(start, tq)],
                        preferred_element_type=F32)
            if masked:
                s = jnp.where(col <= row, s, -jnp.inf)
            m_old = m_ref[hh]
            m_new = jnp.maximum(m_old, jnp.max(s, axis=1, keepdims=True))
            al_ref[slot, hh] = jnp.exp2(m_old - m_new)
            m_ref[hh] = m_new
            p_ref[slot, hh] = jnp.exp2((s - jnp.tile(m_new, (1, tq // LANES))).astype(BF16))

    def weigh(j, slot):
        start = pl.multiple_of(j * tq, tq)
        for hh in range(hpb):
            acc_ref[hh] = al_ref[slot, hh] * acc_ref[hh] + jnp.dot(
                p_ref[slot, hh], v_ref[pl.ds(start, tq), hh * HEAD_PAD:(hh + 1) * HEAD_PAD],
                preferred_element_type=F32)

    m_ref[...] = jnp.full(m_ref.shape, -jnp.inf, F32)
    acc_ref[...] = jnp.zeros_like(acc_ref)
    odd = lax.rem(i, 2)
    first_pending = jnp.where(odd == 1, 0, i)

    @pl.when(odd == 1)
    def _():
        scores(i, 1, True)
        scores(0, 0, False)
        weigh(i, 1)

    @pl.when(odd == 0)
    def _():
        scores(i, 0, True)

    def pair(t, carry):
        a = 2 * t + odd
        scores(a, 1, False)
        weigh(jnp.where(t == 0, first_pending, a - 1), 0)
        scores(a + 1, 0, False)
        weigh(a, 1)
        return carry

    lax.fori_loop(0, i // 2, pair, 0)
    weigh(jnp.where(i < 2, first_pending, i - 1), 0)

    lane = lax.broadcasted_iota(jnp.int32, (tq, HEAD_PAD), 1)
    for pr in range(hpb // 2):
        even, odd = acc_ref[2 * pr], acc_ref[2 * pr + 1]
        out = jnp.where(lane < V_HEAD, even / pltpu.roll(even, V_HEAD, 1),
                        odd / pltpu.roll(odd, V_HEAD, 1))
        o_ref[:, pr * HEAD_PAD:(pr + 1) * HEAD_PAD] = out.astype(BF16)
    finish_cast()


def _attention(q, k, v, B, S, tq, hpb, to_cast):
    T = B * S
    nq = S // tq
    nsteps = B * (N_HEADS // hpb) * nq
    cast_in, cast_out, cast_shapes, cast_scratch = _side_cast_specs(to_cast, nsteps)
    outs = pl.pallas_call(
        functools.partial(_attn_kernel, tq=tq, hpb=hpb, n_cast=len(to_cast)),
        grid=(B, N_HEADS // hpb, nq),
        in_specs=[
            pl.BlockSpec((tq, hpb * HEAD_PAD), lambda b, h, i: (b * nq + i, h)),
            pl.BlockSpec((None, hpb * HEAD_PAD, S), lambda b, h, i: (b, h, 0)),
            pl.BlockSpec((S, hpb * HEAD_PAD), lambda b, h, i: (b, h)),
        ] + cast_in,
        out_specs=[pl.BlockSpec((tq, hpb * V_HEAD), lambda b, h, i: (b * nq + i, h))] + cast_out,
        out_shape=[jax.ShapeDtypeStruct((T, N_HEADS * V_HEAD), BF16)] + cast_shapes,
        scratch_shapes=[pltpu.VMEM((2, hpb, tq, tq), BF16),
                        pltpu.VMEM((hpb, tq, LANES), F32), pltpu.VMEM((2, hpb, tq, LANES), F32),
                        pltpu.VMEM((hpb, tq, HEAD_PAD), F32)] + cast_scratch,
        compiler_params=_cparams("arbitrary", "arbitrary", "arbitrary"),
        name="mla_attention",
    )(q, k, v, *to_cast)
    return outs[0], outs[1:]


def _split3(w):
    hi = w.astype(BF16)
    r1 = w - hi.astype(F32)
    mid = r1.astype(BF16)
    lo = (r1 - mid.astype(F32)).astype(BF16)
    return hi, mid, lo


def _mix_out_route_kernel(x_ref, lru_ref, att_ref, wo_ref, gffn_ref, wrh_ref, wrl_ref,
                          xo_ref, xs_ref, tab_ref, seg_ref, cnt_ref, *, tm):
    mixed = jnp.concatenate([lru_ref[...], att_ref[...]], axis=1)
    xn = x_ref[...] + jnp.dot(mixed, wo_ref[...], preferred_element_type=F32)
    xo_ref[...] = xn
    h2 = _rms(xn, gffn_ref[...])
    h2_hi = h2.astype(BF16)
    h2_lo = (h2 - h2_hi.astype(F32)).astype(BF16)

    nt = (((1,), (1,)), ((), ()))
    logits = (lax.dot_general(wrh_ref[...], h2_hi, nt, preferred_element_type=F32)
              + lax.dot_general(wrh_ref[...], h2_lo, nt, preferred_element_type=F32)
              + lax.dot_general(wrl_ref[...], h2_hi, nt, preferred_element_type=F32))
    eidx = lax.broadcasted_iota(jnp.int32, (E_PAD, tm), 0)
    logits = jnp.where(eidx < N_EXPERTS, logits, -jnp.inf)
    m1 = jnp.max(logits, axis=0, keepdims=True)
    i1 = jnp.min(jnp.where(logits == m1, eidx, E_PAD), axis=0, keepdims=True)
    is0 = eidx == i1
    rest = jnp.where(is0, -jnp.inf, logits)
    m2 = jnp.max(rest, axis=0, keepdims=True)
    i2 = jnp.min(jnp.where(rest == m2, eidx, E_PAD), axis=0, keepdims=True)
    is1 = eidx == i2
    e2 = jnp.exp(m2 - m1)
    den = 1.0 + e2
    w0 = 1.0 / den
    w1 = e2 / den

    sel = jnp.where(is0, 1.0, jnp.where(is1, 1.0, 0.0))
    tr = lax.broadcasted_iota(jnp.int32, (tm, tm), 0)
    tc = lax.broadcasted_iota(jnp.int32, (tm, tm), 1)
    before = jnp.where(tr < tc, 1.0, 0.0).astype(BF16)
    rank = jnp.dot(sel.astype(BF16), before, preferred_element_type=F32).astype(jnp.int32)
    cnt = jnp.sum(sel, axis=1, keepdims=True).astype(jnp.int32)
    cpad = jnp.broadcast_to(((cnt + (SEG_ALIGN - 1)) // SEG_ALIGN) * SEG_ALIGN, (E_PAD, tm))
    inc = cpad
    d = 1
    while d < E_PAD:
        inc = inc + jnp.where(eidx >= d, pltpu.roll(inc, d, 0), 0)
        d *= 2
    segstart = inc - cpad
    dest = segstart + rank
    d0 = jnp.sum(jnp.where(is0, dest, 0), axis=0, keepdims=True)
    d1 = jnp.sum(jnp.where(is1, dest, 0), axis=0, keepdims=True)
    seg_ref[0] = segstart[:, :LANES]
    cnt_ref[0] = cpad[:, :LANES]

    srow = lax.broadcasted_iota(jnp.int32, (LANES, tm), 0)
    stack = jnp.where(srow == 0, w0, jnp.where(srow == 1, w1, jnp.where(
        srow == 2, d0.astype(F32), jnp.where(srow == 3, d1.astype(F32), 0.0))))
    tab = stack.T
    tab_ref[...] = tab

    rr = lax.broadcasted_iota(jnp.int32, (xs_ref.shape[1], tm), 0)
    p0 = jnp.where(rr == d0, 1.0, 0.0).astype(BF16)
    p1 = jnp.where(rr == d1, 1.0, 0.0).astype(BF16)
    xs_ref[0, :, :D_MODEL] = jnp.dot(p0 + p1, h2_hi, preferred_element_type=F32).astype(BF16)
    lane = lax.broadcasted_iota(jnp.int32, (tm, LANES), 1)

    def gate_cols(col):
        hi, mid, lo = (v.astype(F32) for v in
                       _split3(jnp.broadcast_to(tab[:, col:col + 1], (tm, LANES))))
        return jnp.where(lane == 0, hi, jnp.where(lane == 1, mid, jnp.where(
            lane == 2, lo, 0.0))).astype(BF16)

    grows = (jnp.dot(p0, gate_cols(0), preferred_element_type=F32)
             + jnp.dot(p1, gate_cols(1), preferred_element_type=F32))
    xs_ref[0, :, D_MODEL:] = grows.astype(BF16)


def _mix_out_route(x2, lru, att, p, l, tm):
    T = x2.shape[0]
    nt = T // tm
    rows = 2 * tm + N_EXPERTS * SEG_ALIGN
    row = lambda i: (i, 0)
    blk3 = lambda i: (i, 0, 0)
    return pl.pallas_call(
        functools.partial(_mix_out_route_kernel, tm=tm),
        grid=(nt,),
        in_specs=[
            pl.BlockSpec((tm, D_MODEL), row),
            pl.BlockSpec((tm, LRU_WIDTH), row),
            pl.BlockSpec((tm, N_HEADS * V_HEAD), row),
            _layer_spec(p["w_out"], l), _layer_spec(p["g_ffn"], l),
            _full((E_PAD, D_MODEL)), _full((E_PAD, D_MODEL)),
        ],
        out_specs=[
            pl.BlockSpec((tm, D_MODEL), row),
            pl.BlockSpec((1, rows, XS_COLS), blk3),
            pl.BlockSpec((tm, LANES), row),
            pl.BlockSpec((1, E_PAD, LANES), blk3),
            pl.BlockSpec((1, E_PAD, LANES), blk3),
        ],
        out_shape=[
            jax.ShapeDtypeStruct((T, D_MODEL), F32),
            jax.ShapeDtypeStruct((nt, rows, XS_COLS), BF16),
            jax.ShapeDtypeStruct((T, LANES), F32),
            jax.ShapeDtypeStruct((nt, E_PAD, LANES), jnp.int32),
            jax.ShapeDtypeStruct((nt, E_PAD, LANES), jnp.int32),
        ],
        compiler_params=_cparams("arbitrary"),
        name="mix_out_route",
    )(x2, lru, att, p["w_out"], p["g_ffn"], p["w_router_hi"], p["w_router_lo"])


def _side_cast(step, nsteps, srcs, dsts, in_bufs, out_bufs, in_sem, out_sem):
    n_cast = len(srcs)
    slot = lax.rem(step, 2)

    def chunk_rows(c, kk):
        r = in_bufs[kk].shape[1]
        return pl.ds(pl.multiple_of(c * r, r), r)

    def in_copy(c, sl, kk):
        return pltpu.make_async_copy(srcs[kk].at[chunk_rows(c, kk), :], in_bufs[kk].at[sl], in_sem.at[sl])

    def out_copy(c, sl, kk):
        return pltpu.make_async_copy(out_bufs[kk].at[sl], dsts[kk].at[chunk_rows(c, kk), :], out_sem.at[sl])

    def finish():
        for kk in range(n_cast):
            out_copy(step, slot, kk).start()

        @pl.when(step == nsteps - 1)
        def _():
            for kk in range(n_cast):
                out_copy(step - 1, 1 - slot, kk).wait()
                out_copy(step, slot, kk).wait()

    if n_cast:
        @pl.when(step == 0)
        def _():
            for kk in range(n_cast):
                in_copy(step, slot, kk).start()

        for kk in range(n_cast):
            in_copy(step, slot, kk).wait()

        @pl.when(step + 1 < nsteps)
        def _():
            for kk in range(n_cast):
                in_copy(step + 1, 1 - slot, kk).start()

        @pl.when(step >= 2)
        def _():
            for kk in range(n_cast):
                out_copy(step - 2, slot, kk).wait()

        for kk in range(n_cast):
            out_bufs[kk][slot] = in_bufs[kk][slot].astype(BF16)
    return finish


def _side_cast_specs(to_cast, nsteps):
    assert nsteps >= 2 or not to_cast
    assert all(a.shape[0] % (nsteps * SEG_ALIGN) == 0 for a in to_cast)
    chunks = [(a.shape[0] // nsteps, a.shape[1]) for a in to_cast]
    anyspec = pl.BlockSpec(memory_space=pl.ANY)
    scratch = ([pltpu.VMEM((2,) + c, F32) for c in chunks] + [pltpu.VMEM((2,) + c, BF16) for c in chunks]
               + [pltpu.SemaphoreType.DMA((2,)), pltpu.SemaphoreType.DMA((2,))])
    return ([anyspec] * len(to_cast), [anyspec] * len(to_cast),
            [jax.ShapeDtypeStruct(a.shape, BF16) for a in to_cast], scratch)


def _ffn_dense_kernel(*refs, fc, n_cast):
    x_ref, lru_ref, att_ref, wo_ref, gffn_ref, wg_ref, wu_ref, wd_ref = refs[:8]
    srcs = refs[8:8 + n_cast]
    o_ref = refs[8 + n_cast]
    dsts = refs[9 + n_cast:9 + 2 * n_cast]
    in_bufs = refs[9 + 2 * n_cast:9 + 3 * n_cast]
    out_bufs = refs[9 + 3 * n_cast:9 + 4 * n_cast]
    in_sem, out_sem = refs[9 + 4 * n_cast:]
    finish_cast = _side_cast(pl.program_id(0), pl.num_programs(0), srcs, dsts, in_bufs, out_bufs,
                             in_sem, out_sem)

    mixed = jnp.concatenate([lru_ref[...], att_ref[...]], axis=1)
    acc = x_ref[...] + jnp.dot(mixed, wo_ref[...], preferred_element_type=F32)
    h2 = _rms(acc, gffn_ref[...]).astype(BF16)
    for c in range(D_FF_DENSE // fc):
        g = jnp.dot(h2, wg_ref[:, c * fc:(c + 1) * fc], preferred_element_type=F32)
        u = jnp.dot(h2, wu_ref[:, c * fc:(c + 1) * fc], preferred_element_type=F32)
        act = (g * jax.nn.sigmoid(g) * u).astype(BF16)
        acc = acc + jnp.dot(act, wd_ref[c * fc:(c + 1) * fc, :], preferred_element_type=F32)
    o_ref[...] = acc
    finish_cast()


def _ffn_dense(x2, lru, att, p, l, tm, to_cast):
    T = x2.shape[0]
    nsteps = T // tm
    row = lambda i: (i, 0)
    once = pl.Buffered(1)
    n_cast = len(to_cast)
    cast_in, cast_out, cast_shapes, cast_scratch = _side_cast_specs(to_cast, nsteps)
    outs = pl.pallas_call(
        functools.partial(_ffn_dense_kernel, fc=256, n_cast=n_cast),
        grid=(nsteps,),
        in_specs=[
            pl.BlockSpec((tm, D_MODEL), row),
            pl.BlockSpec((tm, LRU_WIDTH), row),
            pl.BlockSpec((tm, N_HEADS * V_HEAD), row),
            _layer_spec(p["w_out"], l, pipeline_mode=once),
            _layer_spec(p["g_ffn"], l),
            pl.BlockSpec((D_MODEL, D_FF_DENSE), lambda i: (0, 0), pipeline_mode=once),
            pl.BlockSpec((D_MODEL, D_FF_DENSE), lambda i: (0, 0), pipeline_mode=once),
            pl.BlockSpec((D_FF_DENSE, D_MODEL), lambda i: (0, 0), pipeline_mode=once),
        ] + cast_in,
        out_specs=[pl.BlockSpec((tm, D_MODEL), row)] + cast_out,
        out_shape=[jax.ShapeDtypeStruct((T, D_MODEL), F32)] + cast_shapes,
        scratch_shapes=cast_scratch,
        compiler_params=_cparams("arbitrary"),
        name="ffn_dense",
    )(x2, lru, att, p["w_out"], p["g_ffn"], p["w_gate_d"], p["w_up_d"], p["w_down_d"], *to_cast)
    return outs[0], outs[1:]


SEG_BITS = (512, 256, 128, 64, 32, 16)
TAIL_BITS = (128, 64, 32, 16)


def _moe_expert_kernel(seg_ref, cnt_ref, xs_hbm, wg_ref, wu_ref, wd_ref, y_hbm,
                       lhs_ref, ost_ref, zero_ref, in_sem, out_sem, zero_sem, *, gj, big, ch, fc, n_groups):
    k = pl.program_id(0) * n_groups + pl.program_id(1)
    nsteps = N_EXPERTS * n_groups
    slot = lax.rem(k, 2)

    def for_each_piece(step, fn):
        e_s = step // n_groups
        g_s = lax.rem(step, n_groups)
        off = jnp.int32(0)
        for s in range(gj):
            j = g_s * gj + s
            start = seg_ref[j * N_EXPERTS + e_s]
            c = cnt_ref[j * N_EXPERTS + e_s]
            pos = jnp.int32(0)
            for b in SEG_BITS:
                has = (c & b) != 0

                @pl.when(has)
                def _(j=j, src=start + pos, dst=off + pos, b=b):
                    fn(j, pl.multiple_of(src, SEG_ALIGN), pl.multiple_of(dst, SEG_ALIGN), b)

                pos = pos + jnp.where(has, b, 0)
            off = off + c
        return off

    def in_copy(sl):
        def fn(j, src, dst, b):
            return pltpu.make_async_copy(xs_hbm.at[j, pl.ds(src, b), :],
                                         lhs_ref.at[sl, pl.ds(dst, b), :], in_sem.at[sl])
        return fn

    def out_copy(j, src, dst, b):
        return pltpu.make_async_copy(ost_ref.at[pl.ds(dst, b), :],
                                     y_hbm.at[j, pl.ds(src, b), :], out_sem.at[0])

    def start_all(step, mk):
        return for_each_piece(step, lambda *a: mk(*a).start())

    def wait_all(step, mk):
        return for_each_piece(step, lambda *a: mk(*a).wait())

    def for_each_tail_piece(step, fn):
        g_s = lax.rem(step, n_groups)
        for s in range(gj):
            j = g_s * gj + s
            last = j * N_EXPERTS + N_EXPERTS - 1
            tail = seg_ref[last] + cnt_ref[last]
            c = y_hbm.shape[1] - tail
            pos = jnp.int32(0)
            for b in TAIL_BITS:
                has = (c & b) != 0

                @pl.when(has)
                def _(j=j, dst=tail + pos, b=b):
                    fn(j, pl.multiple_of(dst, SEG_ALIGN), b)

                pos = pos + jnp.where(has, b, 0)

    def zero_copy(j, dst, b):
        return pltpu.make_async_copy(zero_ref.at[pl.ds(0, b), :], y_hbm.at[j, pl.ds(dst, b), :],
                                     zero_sem.at[0])

    @pl.when(k == 0)
    def _():
        lhs_ref[...] = jnp.zeros_like(lhs_ref)
        zero_ref[...] = jnp.zeros_like(zero_ref)
        start_all(k, in_copy(slot))

    @pl.when(k < n_groups)
    def _():
        for_each_tail_piece(k, lambda *a: zero_copy(*a).start())

    n_rows = wait_all(k, in_copy(slot))

    @pl.when(k + 1 < nsteps)
    def _():
        start_all(k + 1, in_copy(1 - slot))

    @pl.when(k >= 1)
    def _():
        wait_all(k - 1, out_copy)

    def chunk(r0, size, fc):
        rows = lhs_ref[slot, pl.ds(r0, size), :]
        xrow = rows[:, :D_MODEL]
        gate = (rows[:, D_MODEL:D_MODEL + 1].astype(F32) + rows[:, D_MODEL + 1:D_MODEL + 2].astype(F32)
                + rows[:, D_MODEL + 2:D_MODEL + 3].astype(F32))
        y = jnp.zeros((size, D_MODEL), F32)
        for c in range(D_FF_EXPERT // fc):
            g = jnp.dot(xrow, wg_ref[0, :, c * fc:(c + 1) * fc], preferred_element_type=F32)
            u = jnp.dot(xrow, wu_ref[0, :, c * fc:(c + 1) * fc], preferred_element_type=F32)
            act = (g * jax.nn.sigmoid(g) * u).astype(BF16)
            y = y + jnp.dot(act, wd_ref[0, c * fc:(c + 1) * fc, :], preferred_element_type=F32)
        ost_ref[pl.ds(r0, size), :] = (y * gate).astype(BF16)

    n_big = n_rows // big

    def big_chunk(ci, carry):
        chunk(pl.multiple_of(ci * big, big), big, fc)
        return carry

    def small_chunk(ci, carry):
        chunk(pl.multiple_of(n_big * big + ci * ch, ch), ch, D_FF_EXPERT)
        return carry

    lax.fori_loop(0, n_big, big_chunk, 0)
    lax.fori_loop(0, (n_rows - n_big * big + ch - 1) // ch, small_chunk, 0)
    start_all(k, out_copy)

    @pl.when(k < n_groups)
    def _():
        for_each_tail_piece(k, lambda *a: zero_copy(*a).wait())

    @pl.when(k == nsteps - 1)
    def _():
        wait_all(k, out_copy)


def _moe_experts(xs, seg, cnt, p, gj, big, ch):
    nt, rows, _ = xs.shape
    n_groups = nt // gj
    cap = gj * (rows - N_EXPERTS * SEG_ALIGN) // 2 + ch
    wmap = lambda e, g, seg_r, cnt_r: (e, 0, 0)
    grid_spec = pltpu.PrefetchScalarGridSpec(
        num_scalar_prefetch=2,
        grid=(N_EXPERTS, n_groups),
        in_specs=[
            pl.BlockSpec(memory_space=pl.ANY),
            pl.BlockSpec((1, D_MODEL, D_FF_EXPERT), wmap),
            pl.BlockSpec((1, D_MODEL, D_FF_EXPERT), wmap),
            pl.BlockSpec((1, D_FF_EXPERT, D_MODEL), wmap),
        ],
        out_specs=pl.BlockSpec(memory_space=pl.ANY),
        scratch_shapes=[
            pltpu.VMEM((2, cap, XS_COLS), BF16),
            pltpu.VMEM((cap, D_MODEL), BF16),
            pltpu.VMEM((sum(TAIL_BITS), D_MODEL), BF16),
            pltpu.SemaphoreType.DMA((2,)),
            pltpu.SemaphoreType.DMA((1,)),
            pltpu.SemaphoreType.DMA((1,)),
        ],
    )
    assert N_EXPERTS * SEG_ALIGN <= sum(TAIL_BITS)
    return pl.pallas_call(
        functools.partial(_moe_expert_kernel, gj=gj, big=big, ch=ch, fc=256, n_groups=n_groups),
        grid_spec=grid_spec,
        out_shape=jax.ShapeDtypeStruct((nt, rows, D_MODEL), BF16),
        compiler_params=_cparams("arbitrary", "arbitrary"),
        name="moe_experts",
    )(seg, cnt, xs, p["w_gate_e"], p["w_up_e"], p["w_down_e"])


def _moe_combine_kernel(x_ref, y_ref, tab_ref, gfin_ref, o_ref, *, final_norm):
    tm = x_ref.shape[0]
    rows = y_ref.shape[1]
    d0 = tab_ref[:, 2:3].astype(jnp.int32)
    d1 = tab_ref[:, 3:4].astype(jnp.int32)
    rr = lax.broadcasted_iota(jnp.int32, (tm, rows), 1)
    pt = jnp.where(rr == d0, 1.0, jnp.where(rr == d1, 1.0, 0.0)).astype(BF16)
    out = x_ref[...] + jnp.dot(pt, y_ref[0], preferred_element_type=F32)
    if final_norm:
        out = _rms(out, gfin_ref[...])
    o_ref[...] = out


def _moe_combine(xn, y, tab, g_final, tm, final_norm):
    T = xn.shape[0]
    rows = y.shape[1]
    row = lambda i: (i, 0)
    return pl.pallas_call(
        functools.partial(_moe_combine_kernel, final_norm=final_norm),
        grid=(T // tm,),
        in_specs=[
            pl.BlockSpec((tm, D_MODEL), row),
            pl.BlockSpec((1, rows, D_MODEL), lambda i: (i, 0, 0)),
            pl.BlockSpec((tm, LANES), row),
            _full((1, D_MODEL)),
        ],
        out_specs=pl.BlockSpec((tm, D_MODEL), row),
        out_shape=jax.ShapeDtypeStruct((T, D_MODEL), F32),
        compiler_params=_cparams("arbitrary"),
        name="moe_combine",
    )(xn, y, tab, g_final)


def _rope_block(w_rope, swap):
    half = QK_ROPE // 2
    if swap:
        w_rope = jnp.concatenate([w_rope[..., half:], w_rope[..., :half]], axis=-1)
    lead = w_rope.shape[:-1]
    return jnp.concatenate([jnp.zeros(lead + (QK_NOPE,), w_rope.dtype), w_rope,
                            jnp.zeros(lead + (HEAD_PAD - QK_NOPE - QK_ROPE,), w_rope.dtype)], axis=-1)


def _block_diag(w):
    eye = jnp.eye(LRU_BLOCKS, dtype=w.dtype)
    return jnp.einsum("lncd,nm->lncmd", w, eye).reshape(w.shape[0], LRU_WIDTH, LRU_WIDTH)


def _stacked_params(norm_mix, w_in, conv_w, conv_b, lru_wa, lru_ba, lru_wx, lru_bx, lru_lambda,
                    q_norm, w_uq, kv_norm, w_ukv, w_out, norm_ffn):
    L = norm_mix.shape[0]
    scale = math.log2(math.e) / math.sqrt(QK_NOPE + QK_ROPE)
    w_kr = w_in[:, :, C_CKV + KV_LORA:]
    w_in_pad = jnp.concatenate([w_in[:, :, :C_KR], _rope_block(w_kr, False), _rope_block(w_kr, True)],
                               axis=2).astype(BF16)
    wq = (w_uq * scale).reshape(L, Q_LORA, N_HEADS, QK_NOPE + QK_ROPE)
    zpad = jnp.zeros((L, Q_LORA, N_HEADS, HEAD_PAD - QK_NOPE - QK_ROPE), F32)
    zn = jnp.zeros((L, Q_LORA, N_HEADS, QK_NOPE), F32)
    half = QK_ROPE // 2
    wq_a = jnp.concatenate([wq, zpad], axis=3).reshape(L, Q_LORA, N_HEADS * HEAD_PAD)
    wq_sw = jnp.concatenate([wq[..., QK_NOPE + half:], wq[..., QK_NOPE:QK_NOPE + half]], axis=3)
    wq_b = jnp.concatenate([zn, wq_sw, zpad], axis=3).reshape(L, Q_LORA, N_HEADS * HEAD_PAD)
    wkv = w_ukv.reshape(L, KV_LORA, N_HEADS, QK_NOPE + V_HEAD)
    wk = jnp.concatenate([wkv[..., :QK_NOPE], jnp.zeros((L, KV_LORA, N_HEADS, HEAD_PAD - QK_NOPE), F32)],
                         axis=3).reshape(L, KV_LORA, N_HEADS * HEAD_PAD)
    zv = jnp.zeros((L, KV_LORA, N_HEADS // 2, V_HEAD), F32)
    wv4 = wkv[..., QK_NOPE:].reshape(L, KV_LORA, N_HEADS // 2, 2, V_HEAD)
    wv = jnp.stack([wv4[:, :, :, 0], zv, zv, wv4[:, :, :, 1]], axis=3).reshape(L, KV_LORA, N_HEADS * HEAD_PAD)
    ones_pat = jnp.tile(jnp.concatenate([jnp.zeros((V_HEAD,), F32), jnp.ones((2 * V_HEAD,), F32),
                                         jnp.zeros((V_HEAD,), F32)]), N_HEADS // 2)[None, :]
    row = lambda a: a.reshape(L, 1, -1)
    return {
        "g_mix": row(norm_mix),
        "w_in": w_in_pad,
        "conv_w": conv_w,
        "conv_b": row(conv_b),
        "w_gate": jnp.concatenate([_block_diag(lru_wa), _block_diag(lru_wx)], axis=2).astype(BF16),
        "b_gate": jnp.concatenate([row(lru_ba), row(lru_bx)], axis=2),
        "lam": row(lru_lambda),
        "g_q": row(q_norm),
        "w_q": jnp.concatenate([wq_a, wq_b], axis=2).astype(BF16),
        "g_kv": row(kv_norm),
        "w_kv": jnp.concatenate([wk, wv], axis=2).astype(BF16),
        "v_ones": ones_pat,
        "w_out": w_out.astype(BF16),
        "g_ffn": row(norm_ffn),
    }


def _layer_spec(arr, l, **kw):
    return pl.BlockSpec((None,) + arr.shape[1:], lambda *_: (l, 0, 0), **kw)


def kernel(x, positions, norm_mix, w_in, conv_w, conv_b, lru_wa, lru_ba, lru_wx, lru_bx, lru_lambda,
           q_norm, w_uq, kv_norm, w_ukv, w_out, norm_ffn, dense_w_gate, dense_w_up, dense_w_down,
           router_w, expert_w_gate, expert_w_up, expert_w_down, norm_final):
    B, S, _ = x.shape
    T = B * S
    depth = norm_mix.shape[0]
    ts = min(S, 512)
    tq = min(S, 512)
    tm = min(T, 512)
    ctab, stab = _rope_tables(positions)
    x2 = x.reshape(T, D_MODEL)
    experts_bf16 = {}
    stacked = _stacked_params(norm_mix, w_in, conv_w, conv_b, lru_wa, lru_ba, lru_wx, lru_bx,
                              lru_lambda, q_norm, w_uq, kv_norm, w_ukv, w_out, norm_ffn)
    for l in range(depth):
        p = dict(stacked)
        lru, q, k, v = _mix_in(x2, B, S, p, l, ctab, stab, ts, nb=2 if B % 2 == 0 else 1)
        j = l // 2
        last = l == depth - 1
        dense = l % 2 == 0
        hpb = N_HEADS
        attn_steps = B * (S // tq) * (N_HEADS // hpb)
        cast_dense = dense and attn_steps >= 2 and D_MODEL % (attn_steps * SEG_ALIGN) == 0
        dense_f32 = [dense_w_gate[j], dense_w_up[j]] if cast_dense else []
        att, dense_bf16 = _attention(q, k, v, B, S, tq, hpb, dense_f32)
        if dense:
            if cast_dense:
                p["w_gate_d"], p["w_up_d"] = dense_bf16
            else:
                p["w_gate_d"] = dense_w_gate[j].astype(BF16)
                p["w_up_d"] = dense_w_up[j].astype(BF16)
            p["w_down_d"] = dense_w_down[j].astype(BF16)
            to_cast = []
            if l + 1 < depth and T // tm >= 2:
                jn = (l + 1) // 2
                to_cast = [expert_w_gate[jn].reshape(-1, D_FF_EXPERT), expert_w_up[jn].reshape(-1, D_FF_EXPERT),
                           expert_w_down[jn].reshape(-1, D_MODEL)]
            x2, cast = _ffn_dense(x2, lru, att, p, l, tm, to_cast)
            if cast:
                experts_bf16[l + 1] = (cast[0].reshape(N_EXPERTS, D_MODEL, D_FF_EXPERT),
                                       cast[1].reshape(N_EXPERTS, D_MODEL, D_FF_EXPERT),
                                       cast[2].reshape(N_EXPERTS, D_FF_EXPERT, D_MODEL))
        else:
            wr = jnp.pad(router_w[j].T, ((0, E_PAD - N_EXPERTS), (0, 0)))
            p["w_router_hi"] = wr.astype(BF16)
            p["w_router_lo"] = (wr - p["w_router_hi"].astype(F32)).astype(BF16)
            xn, xs, tab, seg, cnt = _mix_out_route(x2, lru, att, p, l, tm)
            if l in experts_bf16:
                p["w_gate_e"], p["w_up_e"], p["w_down_e"] = experts_bf16[l]
            else:
                p["w_gate_e"] = expert_w_gate[j].astype(BF16)
                p["w_up_e"] = expert_w_up[j].astype(BF16)
                p["w_down_e"] = expert_w_down[j].astype(BF16)
            seg = seg[:, :N_EXPERTS, 0].reshape(-1)
            cnt = cnt[:, :N_EXPERTS, 0].reshape(-1)
            y = _moe_experts(xs, seg, cnt, p, gj=min(8, T // tm), big=512, ch=128)
            x2 = _moe_combine(xn, y, tab, norm_final[None, :], tm, final_norm=last)
    return x2.reshape(B, S, D_MODEL)
```

```python
import functools
import math

import jax
import jax.numpy as jnp
from jax import lax
from jax.experimental import pallas as pl
from jax.experimental.pallas import tpu as pltpu

D_MODEL = 1024
EPS = 1e-6
LRU_WIDTH = 512
LRU_BLOCKS = 8
LRU_BLOCK_W = 64
LRU_C = 8.0
CONV_W = 4
N_HEADS = 8
QK_NOPE = 64
QK_ROPE = 32
V_HEAD = 64
Q_LORA = 256
KV_LORA = 128
ROPE_THETA = 10000.0
N_EXPERTS = 8
D_FF_DENSE = 2816
D_FF_EXPERT = 1792

LANES = 128
HEAD_PAD = 128
C_XLRU = 0
C_GATE = LRU_WIDTH
C_CQ = 2 * LRU_WIDTH
C_CKV = C_CQ + Q_LORA
C_KR = C_CKV + KV_LORA
C_KRS = C_KR + HEAD_PAD
IN_COLS_PAD = C_KRS + HEAD_PAD

E_PAD = 16
SEG_ALIGN = 16
XS_COLS = D_MODEL + LANES

VMEM_LIMIT = 56 * 1024 * 1024

F32 = jnp.float32
BF16 = jnp.bfloat16


def _cparams(*sem):
    return pltpu.CompilerParams(dimension_semantics=sem, vmem_limit_bytes=VMEM_LIMIT)


def _rms(x, g):
    return x * lax.rsqrt(jnp.mean(x * x, axis=-1, keepdims=True) + EPS) * g


def _full(shape):
    nd = len(shape)
    return pl.BlockSpec(shape, lambda *_: (0,) * nd)


def _rope_kernel(pos_ref, inv_ref, sgn_ref, c_ref, s_ref):
    ang = pos_ref[...].astype(F32) * inv_ref[...]
    lane = lax.broadcasted_iota(jnp.int32, ang.shape, 1)
    c_ref[...] = jnp.where(lane < QK_NOPE, 1.0,
                           jnp.where(lane < QK_NOPE + QK_ROPE, jnp.cos(ang), 0.0))
    s_ref[...] = jnp.sin(ang) * sgn_ref[...]


def _rope_tables(positions):
    T = positions.size
    tt = min(T, 2048)
    half = QK_ROPE // 2
    inv = 1.0 / (ROPE_THETA ** (jnp.arange(half, dtype=F32) / half))
    zeros = jnp.zeros((QK_NOPE,), F32)
    inv128 = jnp.concatenate([zeros, inv, inv, jnp.zeros((32,), F32)])[None, :]
    sgn128 = jnp.concatenate([zeros, -jnp.ones((half,), F32), jnp.ones((half,), F32),
                              jnp.zeros((32,), F32)])[None, :]
    return pl.pallas_call(
        _rope_kernel,
        grid=(T // tt,),
        in_specs=[pl.BlockSpec((tt, 1), lambda i: (i, 0)), _full((1, LANES)), _full((1, LANES))],
        out_specs=[pl.BlockSpec((tt, LANES), lambda i: (i, 0))] * 2,
        out_shape=[jax.ShapeDtypeStruct((T, LANES), F32)] * 2,
        compiler_params=_cparams("arbitrary"),
        name="rope_tables",
    )(positions.reshape(T, 1), inv128, sgn128)


def _mix_in_proj(x_ref, gmix_ref, win_ref):
    h = _rms(x_ref[...], gmix_ref[...]).astype(BF16)
    return jnp.dot(h, win_ref[...], preferred_element_type=F32)


NLB = LRU_WIDTH // LANES


def _lru_conv(z, convw_ref, convb_ref, halo_ref, *, ts):
    nlb = NLB
    ng = ts // 8
    for c in range(nlb):
        halo_ref[c, 8:, :] = z[:, C_XLRU + c * LANES:C_XLRU + (c + 1) * LANES]

    def strided_rows(first):
        return jnp.concatenate([halo_ref[c, pl.ds(first, 8, stride=ng), :] for c in range(nlb)], axis=1)

    xc_parts = []
    for g in range(ng):
        acc = convb_ref[...]
        for kk in range(CONV_W):
            acc = acc + convw_ref[kk:kk + 1, :] * strided_rows(8 + g - (CONV_W - 1 - kk))
        xc_parts.append(acc)
    xc = jnp.concatenate(xc_parts, axis=0)
    halo_ref[:, 0:8, :] = halo_ref[:, ts:ts + 8, :]
    return xc


def _lru_scan(z, xc, gates, lam_ref, lru_ref, hcarry_ref, hbuf_ref, ab_ref, *, ts):
    si = pl.program_id(1)
    nlb = NLB
    ng = ts // 8
    nlam = -lam_ref[...]
    softplus = jnp.maximum(nlam, 0.0) + jnp.log1p(jnp.exp(-jnp.abs(nlam)))

    hrun = jnp.zeros((8, LRU_WIDTH), F32)
    prun = jnp.ones((8, LRU_WIDTH), F32)
    rb = 64
    for r0 in range(0, ts, rb):
        gts = gates[r0:r0 + rb, :]
        r = jax.nn.sigmoid(gts[:, :LRU_WIDTH])
        ig = jax.nn.sigmoid(gts[:, LRU_WIDTH:])
        log_a = -LRU_C * r * softplus
        a = jnp.exp(log_a)
        mult = jnp.sqrt(-jnp.tanh(log_a) * (a * a + 1.0))
        if r0 == 0:
            row = lax.broadcasted_iota(jnp.int32, (rb, LRU_WIDTH), 0)
            mult = jnp.where(row + si * ts == 0, 1.0, mult)
        b = mult * (ig * xc[r0:r0 + rb, :])
        ab_ref[0, r0:r0 + rb, :] = a
        ab_ref[1, r0:r0 + rb, :] = b
        for g in range(rb // 8):
            ag = a[g * 8:(g + 1) * 8, :]
            hrun = ag * hrun + b[g * 8:(g + 1) * 8, :]
            prun = ag * prun
    sub = lax.broadcasted_iota(jnp.int32, (8, LRU_WIDTH), 0)
    d = 1
    while d < 8:
        keep = sub >= d
        hrun = prun * jnp.where(keep, pltpu.roll(hrun, d, 0), 0.0) + hrun
        prun = prun * jnp.where(keep, pltpu.roll(prun, d, 0), 1.0)
        d *= 2
    block_end = hrun + prun * hcarry_ref[...]
    carry_in = jnp.where(sub >= 1, pltpu.roll(block_end, 1, 0), hcarry_ref[...])
    hcarry_ref[...] = block_end[7:8, :]
    hg = carry_in
    for g in range(ng):
        hg = ab_ref[0, g * 8:(g + 1) * 8, :] * hg + ab_ref[1, g * 8:(g + 1) * 8, :]
        for c in range(nlb):
            hbuf_ref[c, pl.ds(g, 8, stride=ng), :] = hg[:, c * LANES:(c + 1) * LANES]
    hseq = jnp.concatenate([hbuf_ref[c] for c in range(nlb)], axis=1)
    lru_ref[...] = (hseq * jax.nn.gelu(z[:, C_GATE:C_GATE + LRU_WIDTH])).astype(BF16)


def _qkv_project(z, gq_ref, wq_ref, gkv_ref, wkv_ref):
    hq = _rms(z[:, C_CQ:C_CQ + Q_LORA], gq_ref[...]).astype(BF16)
    qq = jnp.dot(hq, wq_ref[...], preferred_element_type=F32)
    hkv = _rms(z[:, C_CKV:C_CKV + KV_LORA], gkv_ref[...]).astype(BF16)
    kv = jnp.dot(hkv, wkv_ref[...], preferred_element_type=F32)
    return qq, kv


def _qkv_rotary_store(z, qq, kv, vones_ref, c_ref, s_ref, q_ref, k_ref, v_ref):
    cmul = c_ref[...]
    smul = s_ref[...]
    kr = z[:, C_KR:C_KR + HEAD_PAD] * cmul + z[:, C_KRS:C_KRS + HEAD_PAD] * smul
    nq = N_HEADS * HEAD_PAD
    for hh in range(N_HEADS):
        lo = hh * HEAD_PAD
        q_ref[:, lo:lo + HEAD_PAD] = (qq[:, lo:lo + HEAD_PAD] * cmul
                                      + qq[:, nq + lo:nq + lo + HEAD_PAD] * smul).astype(BF16)
        k_ref[lo:lo + HEAD_PAD, :] = (kv[:, lo:lo + HEAD_PAD] + kr).T.astype(BF16)
    v_ref[...] = (kv[:, nq:] + vones_ref[...]).astype(BF16)


def _mix_in_kernel(x_ref, gmix_ref, win_ref, convw_ref, convb_ref, wgate_ref, bgate_ref,
                   lam_ref, gq_ref, wq_ref, gkv_ref, wkv_ref, vones_ref, c_ref, s_ref,
                   lru_ref, q_ref, k_ref, v_ref, halo_ref, hcarry_ref, hbuf_ref, ab_ref, *, ts, nb):
    @pl.when(pl.program_id(1) == 0)
    def _():
        halo_ref[:, :, 0:8, :] = jnp.zeros((nb, halo_ref.shape[1], 8, LANES), F32)
        hcarry_ref[...] = jnp.zeros_like(hcarry_ref)

    seqs = range(nb)
    zs = [_mix_in_proj(x_ref.at[bb], gmix_ref, win_ref) for bb in seqs]
    xcs = [_lru_conv(zs[bb], convw_ref, convb_ref, halo_ref.at[bb], ts=ts) for bb in seqs]
    gates = [jnp.dot(xcs[bb].astype(BF16), wgate_ref[...], preferred_element_type=F32) + bgate_ref[...]
             for bb in seqs]
    qkvs = [_qkv_project(zs[bb], gq_ref, wq_ref, gkv_ref, wkv_ref) for bb in seqs]
    for bb in seqs:
        _lru_scan(zs[bb], xcs[bb], gates[bb], lam_ref, lru_ref.at[bb], hcarry_ref.at[bb],
                  hbuf_ref.at[bb], ab_ref.at[bb], ts=ts)
    for bb in seqs:
        _qkv_rotary_store(zs[bb], *qkvs[bb], vones_ref, c_ref.at[bb], s_ref.at[bb],
                          q_ref.at[bb], k_ref.at[bb], v_ref.at[bb])


def _mix_in(x2, B, S, p, l, ctab, stab, ts, nb):
    T = B * S
    ns = S // ts
    blk = lambda b, s: (b, s, 0)
    kern = functools.partial(_mix_in_kernel, ts=ts, nb=nb)
    nq = N_HEADS * HEAD_PAD
    nlb = LRU_WIDTH // LANES
    layer_names = ("g_mix", "w_in", "conv_w", "conv_b", "w_gate", "b_gate", "lam", "g_q", "w_q", "g_kv", "w_kv")
    outs = pl.pallas_call(
        kern,
        grid=(B // nb, ns),
        in_specs=[pl.BlockSpec((nb, ts, D_MODEL), blk)]
        + [_layer_spec(p[name], l) for name in layer_names]
        + [_full((1, nq)), pl.BlockSpec((nb, ts, LANES), blk), pl.BlockSpec((nb, ts, LANES), blk)],
        out_specs=[
            pl.BlockSpec((nb, ts, LRU_WIDTH), blk),
            pl.BlockSpec((nb, ts, nq), blk),
            pl.BlockSpec((nb, nq, ts), lambda b, s: (b, 0, s)),
            pl.BlockSpec((nb, ts, nq), blk),
        ],
        out_shape=[
            jax.ShapeDtypeStruct((B, S, LRU_WIDTH), BF16),
            jax.ShapeDtypeStruct((B, S, nq), BF16),
            jax.ShapeDtypeStruct((B, nq, S), BF16),
            jax.ShapeDtypeStruct((B, S, nq), BF16),
        ],
        scratch_shapes=[pltpu.VMEM((nb, nlb, ts + 8, LANES), F32),
                        pltpu.VMEM((nb, 1, LRU_WIDTH), F32),
                        pltpu.VMEM((nb, nlb, ts, LANES), F32),
                        pltpu.VMEM((nb, 2, ts, LRU_WIDTH), F32)],
        compiler_params=_cparams("arbitrary", "arbitrary"),
        name="mix_in",
    )(x2.reshape(B, S, D_MODEL), *[p[name] for name in layer_names], p["v_ones"],
      ctab.reshape(B, S, LANES), stab.reshape(B, S, LANES))
    lru, q, kt, v = outs
    return lru.reshape(T, LRU_WIDTH), q.reshape(T, nq), kt, v.reshape(T, nq)


def _attn_kernel(*refs, tq, hpb, n_cast):
    q_ref, k_ref, v_ref = refs[:3]
    srcs = refs[3:3 + n_cast]
    o_ref = refs[3 + n_cast]
    dsts = refs[4 + n_cast:4 + 2 * n_cast]
    p_ref, m_ref, al_ref, acc_ref = refs[4 + 2 * n_cast:8 + 2 * n_cast]
    in_bufs = refs[8 + 2 * n_cast:8 + 3 * n_cast]
    out_bufs = refs[8 + 3 * n_cast:8 + 4 * n_cast]
    in_sem, out_sem = refs[8 + 4 * n_cast:]
    grid_step = ((pl.program_id(0) * pl.num_programs(1) + pl.program_id(1)) * pl.num_programs(2)
                 + pl.program_id(2))
    grid_steps = pl.num_programs(0) * pl.num_programs(1) * pl.num_programs(2)
    finish_cast = _side_cast(grid_step, grid_steps, srcs, dsts, in_bufs, out_bufs, in_sem, out_sem)
    i = pl.program_id(2)
    row = lax.broadcasted_iota(jnp.int32, (tq, tq), 0)
    col = lax.broadcasted_iota(jnp.int32, (tq, tq), 1)

    def scores(j, slot, masked):
        start = pl.multiple_of(j * tq, tq)
        for hh in range(hpb):
            lo = hh * HEAD_PAD
            s = jnp.dot(q_ref[:, lo:lo + HEAD_PAD], k_ref[lo:lo + HEAD_PAD, pl.ds(start, tq)],
                        preferred_element_type=F32)
            if masked:
                s = jnp.where(col <= row, s, -jnp.inf)
            m_old = m_ref[hh]
            m_new = jnp.maximum(m_old, jnp.max(s, axis=1, keepdims=True))
            al_ref[slot, hh] = jnp.exp2(m_old - m_new)
            m_ref[hh] = m_new
            p_ref[slot, hh] = jnp.exp2((s - jnp.tile(m_new, (1, tq // LANES))).astype(BF16))

    def weigh(j, slot):
        start = pl.multiple_of(j * tq, tq)
        for hh in range(hpb):
            acc_ref[hh] = al_ref[slot, hh] * acc_ref[hh] + jnp.dot(
                p_ref[slot, hh], v_ref[pl.ds(start, tq), hh * HEAD_PAD:(hh + 1) * HEAD_PAD],
                preferred_element_type=F32)

    m_ref[...] = jnp.full(m_ref.shape, -jnp.inf, F32)
    acc_ref[...] = jnp.zeros_like(acc_ref)
    odd = lax.rem(i, 2)
    first_pending = jnp.where(odd == 1, 0, i)

    @pl.when(odd == 1)
    def _():
        scores(i, 1, True)
        scores(0, 0, False)
        weigh(i, 1)

    @pl.when(odd == 0)
    def _():
        scores(i, 0, True)

    def pair(t, carry):
        a = 2 * t + odd
        scores(a, 1, False)
        weigh(jnp.where(t == 0, first_pending, a - 1), 0)
        scores(a + 1, 0, False)
        weigh(a, 1)
        return carry

    lax.fori_loop(0, i // 2, pair, 0)
    weigh(jnp.where(i < 2, first_pending, i - 1), 0)

    lane = lax.broadcasted_iota(jnp.int32, (tq, HEAD_PAD), 1)
    for pr in range(hpb // 2):
        even, odd = acc_ref[2 * pr], acc_ref[2 * pr + 1]
        out = jnp.where(lane < V_HEAD, even / pltpu.roll(even, V_HEAD, 1),
                        odd / pltpu.roll(odd, V_HEAD, 1))
        o_ref[:, pr * HEAD_PAD:(pr + 1) * HEAD_PAD] = out.astype(BF16)
    finish_cast()


def _attention(q, k, v, B, S, tq, hpb, to_cast):
    T = B * S
    nq = S // tq
    nsteps = B * (N_HEADS // hpb) * nq
    cast_in, cast_out, cast_shapes, cast_scratch = _side_cast_specs(to_cast, nsteps)
    outs = pl.pallas_call(
        functools.partial(_attn_kernel, tq=tq, hpb=hpb, n_cast=len(to_cast)),
        grid=(B, N_HEADS // hpb, nq),
        in_specs=[
            pl.BlockSpec((tq, hpb * HEAD_PAD), lambda b, h, i: (b * nq + i, h)),
            pl.BlockSpec((None, hpb * HEAD_PAD, S), lambda b, h, i: (b, h, 0)),
            pl.BlockSpec((S, hpb * HEAD_PAD), lambda b, h, i: (b, h)),
        ] + cast_in,
        out_specs=[pl.BlockSpec((tq, hpb * V_HEAD), lambda b, h, i: (b * nq + i, h))] + cast_out,
        out_shape=[jax.ShapeDtypeStruct((T, N_HEADS * V_HEAD), BF16)] + cast_shapes,
        scratch_shapes=[pltpu.VMEM((2, hpb, tq, tq), BF16),
                        pltpu.VMEM((hpb, tq, LANES), F32), pltpu.VMEM((2, hpb, tq, LANES), F32),
                        pltpu.VMEM((hpb, tq, HEAD_PAD), F32)] + cast_scratch,
        compiler_params=_cparams("arbitrary", "arbitrary", "arbitrary"),
        name="mla_attention",
    )(q, k, v, *to_cast)
    return outs[0], outs[1:]


def _split3(w):
    hi = w.astype(BF16)
    r1 = w - hi.astype(F32)
    mid = r1.astype(BF16)
    lo = (r1 - mid.astype(F32)).astype(BF16)
    return hi, mid, lo


def _mix_out_route_kernel(x_ref, lru_ref, att_ref, wo_ref, gffn_ref, wrh_ref, wrl_ref,
                          xo_ref, xs_ref, tab_ref, seg_ref, cnt_ref, *, tm):
    mixed = jnp.concatenate([lru_ref[...], att_ref[...]], axis=1)
    xn = x_ref[...] + jnp.dot(mixed, wo_ref[...], preferred_element_type=F32)
    xo_ref[...] = xn
    h2 = _rms(xn, gffn_ref[...])
    h2_hi = h2.astype(BF16)
    h2_lo = (h2 - h2_hi.astype(F32)).astype(BF16)

    nt = (((1,), (1,)), ((), ()))
    logits = (lax.dot_general(wrh_ref[...], h2_hi, nt, preferred_element_type=F32)
              + lax.dot_general(wrh_ref[...], h2_lo, nt, preferred_element_type=F32)
              + lax.dot_general(wrl_ref[...], h2_hi, nt, preferred_element_type=F32))
    eidx = lax.broadcasted_iota(jnp.int32, (E_PAD, tm), 0)
    logits = jnp.where(eidx < N_EXPERTS, logits, -jnp.inf)
    m1 = jnp.max(logits, axis=0, keepdims=True)
    i1 = jnp.min(jnp.where(logits == m1, eidx, E_PAD), axis=0, keepdims=True)
    is0 = eidx == i1
    rest = jnp.where(is0, -jnp.inf, logits)
    m2 = jnp.max(rest, axis=0, keepdims=True)
    i2 = jnp.min(jnp.where(rest == m2, eidx, E_PAD), axis=0, keepdims=True)
    is1 = eidx == i2
    e2 = jnp.exp(m2 - m1)
    den = 1.0 + e2
    w0 = 1.0 / den
    w1 = e2 / den

    sel = jnp.where(is0, 1.0, jnp.where(is1, 1.0, 0.0))
    tr = lax.broadcasted_iota(jnp.int32, (tm, tm), 0)
    tc = lax.broadcasted_iota(jnp.int32, (tm, tm), 1)
    before = jnp.where(tr < tc, 1.0, 0.0).astype(BF16)
    rank = jnp.dot(sel.astype(BF16), before, preferred_element_type=F32).astype(jnp.int32)
    cnt = jnp.sum(sel, axis=1, keepdims=True).astype(jnp.int32)
    cpad = jnp.broadcast_to(((cnt + (SEG_ALIGN - 1)) // SEG_ALIGN) * SEG_ALIGN, (E_PAD, tm))
    inc = cpad
    d = 1
    while d < E_PAD:
        inc = inc + jnp.where(eidx >= d, pltpu.roll(inc, d, 0), 0)
        d *= 2
    segstart = inc - cpad
    dest = segstart + rank
    d0 = jnp.sum(jnp.where(is0, dest, 0), axis=0, keepdims=True)
    d1 = jnp.sum(jnp.where(is1, dest, 0), axis=0, keepdims=True)
    seg_ref[0] = segstart[:, :LANES]
    cnt_ref[0] = cpad[:, :LANES]

    srow = lax.broadcasted_iota(jnp.int32, (LANES, tm), 0)
    stack = jnp.where(srow == 0, w0, jnp.where(srow == 1, w1, jnp.where(
        srow == 2, d0.astype(F32), jnp.where(srow == 3, d1.astype(F32), 0.0))))
    tab = stack.T
    tab_ref[...] = tab

    rr = lax.broadcasted_iota(jnp.int32, (xs_ref.shape[1], tm), 0)
    p0 = jnp.where(rr == d0, 1.0, 0.0).astype(BF16)
    p1 = jnp.where(rr == d1, 1.0, 0.0).astype(BF16)
    xs_ref[0, :, :D_MODEL] = jnp.dot(p0 + p1, h2_hi, preferred_element_type=F32).astype(BF16)
    gate = jnp.sum(jnp.where(rr == d0, w0, jnp.where(rr == d1, w1, 0.0)), axis=1, keepdims=True)
    nrows = xs_ref.shape[1]
    hi, mid, lo = (v.astype(F32) for v in _split3(jnp.broadcast_to(gate, (nrows, LANES))))
    lane = lax.broadcasted_iota(jnp.int32, (nrows, LANES), 1)
    xs_ref[0, :, D_MODEL:] = jnp.where(lane == 0, hi, jnp.where(lane == 1, mid, jnp.where(
        lane == 2, lo, 0.0))).astype(BF16)


def _mix_out_route(x2, lru, att, p, l, tm):
    T = x2.shape[0]
    nt = T // tm
    rows = 2 * tm + N_EXPERTS * SEG_ALIGN
    row = lambda i: (i, 0)
    blk3 = lambda i: (i, 0, 0)
    return pl.pallas_call(
        functools.partial(_mix_out_route_kernel, tm=tm),
        grid=(nt,),
        in_specs=[
            pl.BlockSpec((tm, D_MODEL), row),
            pl.BlockSpec((tm, LRU_WIDTH), row),
            pl.BlockSpec((tm, N_HEADS * V_HEAD), row),
            _layer_spec(p["w_out"], l), _layer_spec(p["g_ffn"], l),
            _full((E_PAD, D_MODEL)), _full((E_PAD, D_MODEL)),
        ],
        out_specs=[
            pl.BlockSpec((tm, D_MODEL), row),
            pl.BlockSpec((1, rows, XS_COLS), blk3),
            pl.BlockSpec((tm, LANES), row),
            pl.BlockSpec((1, E_PAD, LANES), blk3),
            pl.BlockSpec((1, E_PAD, LANES), blk3),
        ],
        out_shape=[
            jax.ShapeDtypeStruct((T, D_MODEL), F32),
            jax.ShapeDtypeStruct((nt, rows, XS_COLS), BF16),
            jax.ShapeDtypeStruct((T, LANES), F32),
            jax.ShapeDtypeStruct((nt, E_PAD, LANES), jnp.int32),
            jax.ShapeDtypeStruct((nt, E_PAD, LANES), jnp.int32),
        ],
        compiler_params=_cparams("arbitrary"),
        name="mix_out_route",
    )(x2, lru, att, p["w_out"], p["g_ffn"], p["w_router_hi"], p["w_router_lo"])


def _side_cast(step, nsteps, srcs, dsts, in_bufs, out_bufs, in_sem, out_sem):
    n_cast = len(srcs)
    slot = lax.rem(step, 2)

    def chunk_rows(c, kk):
        r = in_bufs[kk].shape[1]
        return pl.ds(pl.multiple_of(c * r, r), r)

    def in_copy(c, sl, kk):
        return pltpu.make_async_copy(srcs[kk].at[chunk_rows(c, kk), :], in_bufs[kk].at[sl], in_sem.at[sl])

    def out_copy(c, sl, kk):
        return pltpu.make_async_copy(out_bufs[kk].at[sl], dsts[kk].at[chunk_rows(c, kk), :], out_sem.at[sl])

    def finish():
        for kk in range(n_cast):
            out_copy(step, slot, kk).start()

        @pl.when(step == nsteps - 1)
        def _():
            for kk in range(n_cast):
                out_copy(step - 1, 1 - slot, kk).wait()
                out_copy(step, slot, kk).wait()

    if n_cast:
        @pl.when(step == 0)
        def _():
            for kk in range(n_cast):
                in_copy(step, slot, kk).start()

        for kk in range(n_cast):
            in_copy(step, slot, kk).wait()

        @pl.when(step + 1 < nsteps)
        def _():
            for kk in range(n_cast):
                in_copy(step + 1, 1 - slot, kk).start()

        @pl.when(step >= 2)
        def _():
            for kk in range(n_cast):
                out_copy(step - 2, slot, kk).wait()

        for kk in range(n_cast):
            out_bufs[kk][slot] = in_bufs[kk][slot].astype(BF16)
    return finish


def _side_cast_specs(to_cast, nsteps):
    assert nsteps >= 2 or not to_cast
    assert all(a.shape[0] % (nsteps * SEG_ALIGN) == 0 for a in to_cast)
    chunks = [(a.shape[0] // nsteps, a.shape[1]) for a in to_cast]
    anyspec = pl.BlockSpec(memory_space=pl.ANY)
    scratch = ([pltpu.VMEM((2,) + c, F32) for c in chunks] + [pltpu.VMEM((2,) + c, BF16) for c in chunks]
               + [pltpu.SemaphoreType.DMA((2,)), pltpu.SemaphoreType.DMA((2,))])
    return ([anyspec] * len(to_cast), [anyspec] * len(to_cast),
            [jax.ShapeDtypeStruct(a.shape, BF16) for a in to_cast], scratch)


def _ffn_dense_kernel(*refs, fc, n_cast):
    x_ref, lru_ref, att_ref, wo_ref, gffn_ref, wg_ref, wu_ref, wd_ref = refs[:8]
    srcs = refs[8:8 + n_cast]
    o_ref = refs[8 + n_cast]
    dsts = refs[9 + n_cast:9 + 2 * n_cast]
    in_bufs = refs[9 + 2 * n_cast:9 + 3 * n_cast]
    out_bufs = refs[9 + 3 * n_cast:9 + 4 * n_cast]
    in_sem, out_sem = refs[9 + 4 * n_cast:]
    finish_cast = _side_cast(pl.program_id(0), pl.num_programs(0), srcs, dsts, in_bufs, out_bufs,
                             in_sem, out_sem)

    mixed = jnp.concatenate([lru_ref[...], att_ref[...]], axis=1)
    acc = x_ref[...] + jnp.dot(mixed, wo_ref[...], preferred_element_type=F32)
    h2 = _rms(acc, gffn_ref[...]).astype(BF16)
    for c in range(D_FF_DENSE // fc):
        g = jnp.dot(h2, wg_ref[:, c * fc:(c + 1) * fc], preferred_element_type=F32)
        u = jnp.dot(h2, wu_ref[:, c * fc:(c + 1) * fc], preferred_element_type=F32)
        act = (g * jax.nn.sigmoid(g) * u).astype(BF16)
        acc = acc + jnp.dot(act, wd_ref[c * fc:(c + 1) * fc, :], preferred_element_type=F32)
    o_ref[...] = acc
    finish_cast()


def _ffn_dense(x2, lru, att, p, l, tm, to_cast):
    T = x2.shape[0]
    nsteps = T // tm
    row = lambda i: (i, 0)
    once = pl.Buffered(1)
    n_cast = len(to_cast)
    cast_in, cast_out, cast_shapes, cast_scratch = _side_cast_specs(to_cast, nsteps)
    outs = pl.pallas_call(
        functools.partial(_ffn_dense_kernel, fc=256, n_cast=n_cast),
        grid=(nsteps,),
        in_specs=[
            pl.BlockSpec((tm, D_MODEL), row),
            pl.BlockSpec((tm, LRU_WIDTH), row),
            pl.BlockSpec((tm, N_HEADS * V_HEAD), row),
            _layer_spec(p["w_out"], l, pipeline_mode=once),
            _layer_spec(p["g_ffn"], l),
            pl.BlockSpec((D_MODEL, D_FF_DENSE), lambda i: (0, 0), pipeline_mode=once),
            pl.BlockSpec((D_MODEL, D_FF_DENSE), lambda i: (0, 0), pipeline_mode=once),
            pl.BlockSpec((D_FF_DENSE, D_MODEL), lambda i: (0, 0), pipeline_mode=once),
        ] + cast_in,
        out_specs=[pl.BlockSpec((tm, D_MODEL), row)] + cast_out,
        out_shape=[jax.ShapeDtypeStruct((T, D_MODEL), F32)] + cast_shapes,
        scratch_shapes=cast_scratch,
        compiler_params=_cparams("arbitrary"),
        name="ffn_dense",
    )(x2, lru, att, p["w_out"], p["g_ffn"], p["w_gate_d"], p["w_up_d"], p["w_down_d"], *to_cast)
    return outs[0], outs[1:]


SEG_BITS = (512, 256, 128, 64, 32, 16)
TAIL_BITS = (128, 64, 32, 16)


def _moe_expert_kernel(seg_ref, cnt_ref, xs_hbm, wg_ref, wu_ref, wd_ref, y_hbm,
                       lhs_ref, ost_ref, zero_ref, in_sem, out_sem, zero_sem, *, gj, big, ch, fc, n_groups):
    k = pl.program_id(0) * n_groups + pl.program_id(1)
    nsteps = N_EXPERTS * n_groups
    slot = lax.rem(k, 2)

    def for_each_piece(step, fn):
        e_s = step // n_groups
        g_s = lax.rem(step, n_groups)
        off = jnp.int32(0)
        for s in range(gj):
            j = g_s * gj + s
            start = seg_ref[j * N_EXPERTS + e_s]
            c = cnt_ref[j * N_EXPERTS + e_s]
            pos = jnp.int32(0)
            for b in SEG_BITS:
                has = (c & b) != 0

                @pl.when(has)
                def _(j=j, src=start + pos, dst=off + pos, b=b):
                    fn(j, pl.multiple_of(src, SEG_ALIGN), pl.multiple_of(dst, SEG_ALIGN), b)

                pos = pos + jnp.where(has, b, 0)
            off = off + c
        return off

    def in_copy(sl):
        def fn(j, src, dst, b):
            return pltpu.make_async_copy(xs_hbm.at[j, pl.ds(src, b), :],
                                         lhs_ref.at[sl, pl.ds(dst, b), :], in_sem.at[sl])
        return fn

    def out_copy(j, src, dst, b):
        return pltpu.make_async_copy(ost_ref.at[pl.ds(dst, b), :],
                                     y_hbm.at[j, pl.ds(src, b), :], out_sem.at[0])

    def start_all(step, mk):
        return for_each_piece(step, lambda *a: mk(*a).start())

    def wait_all(step, mk):
        return for_each_piece(step, lambda *a: mk(*a).wait())

    def for_each_tail_piece(step, fn):
        g_s = lax.rem(step, n_groups)
        for s in range(gj):
            j = g_s * gj + s
            last = j * N_EXPERTS + N_EXPERTS - 1
            tail = seg_ref[last] + cnt_ref[last]
            c = y_hbm.shape[1] - tail
            pos = jnp.int32(0)
            for b in TAIL_BITS:
                has = (c & b) != 0

                @pl.when(has)
                def _(j=j, dst=tail + pos, b=b):
                    fn(j, pl.multiple_of(dst, SEG_ALIGN), b)

                pos = pos + jnp.where(has, b, 0)

    def zero_copy(j, dst, b):
        return pltpu.make_async_copy(zero_ref.at[pl.ds(0, b), :], y_hbm.at[j, pl.ds(dst, b), :],
                                     zero_sem.at[0])

    @pl.when(k == 0)
    def _():
        lhs_ref[...] = jnp.zeros_like(lhs_ref)
        zero_ref[...] = jnp.zeros_like(zero_ref)
        start_all(k, in_copy(slot))

    @pl.when(k < n_groups)
    def _():
        for_each_tail_piece(k, lambda *a: zero_copy(*a).start())

    n_rows = wait_all(k, in_copy(slot))

    @pl.when(k + 1 < nsteps)
    def _():
        start_all(k + 1, in_copy(1 - slot))

    @pl.when(k >= 1)
    def _():
        wait_all(k - 1, out_copy)

    def chunk(r0, size, fc):
        rows = lhs_ref[slot, pl.ds(r0, size), :]
        xrow = rows[:, :D_MODEL]
        gate = (rows[:, D_MODEL:D_MODEL + 1].astype(F32) + rows[:, D_MODEL + 1:D_MODEL + 2].astype(F32)
                + rows[:, D_MODEL + 2:D_MODEL + 3].astype(F32))
        y = jnp.zeros((size, D_MODEL), F32)
        for c in range(D_FF_EXPERT // fc):
            g = jnp.dot(xrow, wg_ref[0, :, c * fc:(c + 1) * fc], preferred_element_type=F32)
            u = jnp.dot(xrow, wu_ref[0, :, c * fc:(c + 1) * fc], preferred_element_type=F32)
            act = (g * jax.nn.sigmoid(g) * u).astype(BF16)
            y = y + jnp.dot(act, wd_ref[0, c * fc:(c + 1) * fc, :], preferred_element_type=F32)
        ost_ref[pl.ds(r0, size), :] = (y * gate).astype(BF16)

    n_big = n_rows // big

    def big_chunk(ci, carry):
        chunk(pl.multiple_of(ci * big, big), big, fc)
        return carry

    def small_chunk(ci, carry):
        chunk(pl.multiple_of(n_big * big + ci * ch, ch), ch, D_FF_EXPERT)
        return carry

    lax.fori_loop(0, n_big, big_chunk, 0)
    lax.fori_loop(0, (n_rows - n_big * big + ch - 1) // ch, small_chunk, 0)
    start_all(k, out_copy)

    @pl.when(k < n_groups)
    def _():
        for_each_tail_piece(k, lambda *a: zero_copy(*a).wait())

    @pl.when(k == nsteps - 1)
    def _():
        wait_all(k, out_copy)


def _moe_experts(xs, seg, cnt, p, gj, big, ch):
    nt, rows, _ = xs.shape
    n_groups = nt // gj
    cap = gj * (rows - N_EXPERTS * SEG_ALIGN) // 2 + ch
    wmap = lambda e, g, seg_r, cnt_r: (e, 0, 0)
    grid_spec = pltpu.PrefetchScalarGridSpec(
        num_scalar_prefetch=2,
        grid=(N_EXPERTS, n_groups),
        in_specs=[
            pl.BlockSpec(memory_space=pl.ANY),
            pl.BlockSpec((1, D_MODEL, D_FF_EXPERT), wmap),
            pl.BlockSpec((1, D_MODEL, D_FF_EXPERT), wmap),
            pl.BlockSpec((1, D_FF_EXPERT, D_MODEL), wmap),
        ],
        out_specs=pl.BlockSpec(memory_space=pl.ANY),
        scratch_shapes=[
            pltpu.VMEM((2, cap, XS_COLS), BF16),
            pltpu.VMEM((cap, D_MODEL), BF16),
            pltpu.VMEM((sum(TAIL_BITS), D_MODEL), BF16),
            pltpu.SemaphoreType.DMA((2,)),
            pltpu.SemaphoreType.DMA((1,)),
            pltpu.SemaphoreType.DMA((1,)),
        ],
    )
    assert N_EXPERTS * SEG_ALIGN <= sum(TAIL_BITS)
    return pl.pallas_call(
        functools.partial(_moe_expert_kernel, gj=gj, big=big, ch=ch, fc=256, n_groups=n_groups),
        grid_spec=grid_spec,
        out_shape=jax.ShapeDtypeStruct((nt, rows, D_MODEL), BF16),
        compiler_params=_cparams("arbitrary", "arbitrary"),
        name="moe_experts",
    )(seg, cnt, xs, p["w_gate_e"], p["w_up_e"], p["w_down_e"])


def _moe_combine_kernel(x_ref, y_ref, tab_ref, gfin_ref, o_ref, *, final_norm):
    tm = x_ref.shape[0]
    rows = y_ref.shape[1]
    d0 = tab_ref[:, 2:3].astype(jnp.int32)
    d1 = tab_ref[:, 3:4].astype(jnp.int32)
    rr = lax.broadcasted_iota(jnp.int32, (tm, rows), 1)
    pt = jnp.where(rr == d0, 1.0, jnp.where(rr == d1, 1.0, 0.0)).astype(BF16)
    out = x_ref[...] + jnp.dot(pt, y_ref[0], preferred_element_type=F32)
    if final_norm:
        out = _rms(out, gfin_ref[...])
    o_ref[...] = out


def _moe_combine(xn, y, tab, g_final, tm, final_norm):
    T = xn.shape[0]
    rows = y.shape[1]
    row = lambda i: (i, 0)
    return pl.pallas_call(
        functools.partial(_moe_combine_kernel, final_norm=final_norm),
        grid=(T // tm,),
        in_specs=[
            pl.BlockSpec((tm, D_MODEL), row),
            pl.BlockSpec((1, rows, D_MODEL), lambda i: (i, 0, 0)),
            pl.BlockSpec((tm, LANES), row),
            _full((1, D_MODEL)),
        ],
        out_specs=pl.BlockSpec((tm, D_MODEL), row),
        out_shape=jax.ShapeDtypeStruct((T, D_MODEL), F32),
        compiler_params=_cparams("arbitrary"),
        name="moe_combine",
    )(xn, y, tab, g_final)


def _rope_block(w_rope, swap):
    half = QK_ROPE // 2
    if swap:
        w_rope = jnp.concatenate([w_rope[..., half:], w_rope[..., :half]], axis=-1)
    lead = w_rope.shape[:-1]
    return jnp.concatenate([jnp.zeros(lead + (QK_NOPE,), w_rope.dtype), w_rope,
                            jnp.zeros(lead + (HEAD_PAD - QK_NOPE - QK_ROPE,), w_rope.dtype)], axis=-1)


def _block_diag(w):
    eye = jnp.eye(LRU_BLOCKS, dtype=w.dtype)
    return jnp.einsum("lncd,nm->lncmd", w, eye).reshape(w.shape[0], LRU_WIDTH, LRU_WIDTH)


def _stacked_params(norm_mix, w_in, conv_w, conv_b, lru_wa, lru_ba, lru_wx, lru_bx, lru_lambda,
                    q_norm, w_uq, kv_norm, w_ukv, w_out, norm_ffn):
    L = norm_mix.shape[0]
    scale = math.log2(math.e) / math.sqrt(QK_NOPE + QK_ROPE)
    w_kr = w_in[:, :, C_CKV + KV_LORA:]
    w_in_pad = jnp.concatenate([w_in[:, :, :C_KR], _rope_block(w_kr, False), _rope_block(w_kr, True)],
                               axis=2).astype(BF16)
    wq = (w_uq * scale).reshape(L, Q_LORA, N_HEADS, QK_NOPE + QK_ROPE)
    zpad = jnp.zeros((L, Q_LORA, N_HEADS, HEAD_PAD - QK_NOPE - QK_ROPE), F32)
    zn = jnp.zeros((L, Q_LORA, N_HEADS, QK_NOPE), F32)
    half = QK_ROPE // 2
    wq_a = jnp.concatenate([wq, zpad], axis=3).reshape(L, Q_LORA, N_HEADS * HEAD_PAD)
    wq_sw = jnp.concatenate([wq[..., QK_NOPE + half:], wq[..., QK_NOPE:QK_NOPE + half]], axis=3)
    wq_b = jnp.concatenate([zn, wq_sw, zpad], axis=3).reshape(L, Q_LORA, N_HEADS * HEAD_PAD)
    wkv = w_ukv.reshape(L, KV_LORA, N_HEADS, QK_NOPE + V_HEAD)
    wk = jnp.concatenate([wkv[..., :QK_NOPE], jnp.zeros((L, KV_LORA, N_HEADS, HEAD_PAD - QK_NOPE), F32)],
                         axis=3).reshape(L, KV_LORA, N_HEADS * HEAD_PAD)
    zv = jnp.zeros((L, KV_LORA, N_HEADS // 2, V_HEAD), F32)
    wv4 = wkv[..., QK_NOPE:].reshape(L, KV_LORA, N_HEADS // 2, 2, V_HEAD)
    wv = jnp.stack([wv4[:, :, :, 0], zv, zv, wv4[:, :, :, 1]], axis=3).reshape(L, KV_LORA, N_HEADS * HEAD_PAD)
    ones_pat = jnp.tile(jnp.concatenate([jnp.zeros((V_HEAD,), F32), jnp.ones((2 * V_HEAD,), F32),
                                         jnp.zeros((V_HEAD,), F32)]), N_HEADS // 2)[None, :]
    row = lambda a: a.reshape(L, 1, -1)
    return {
        "g_mix": row(norm_mix),
        "w_in": w_in_pad,
        "conv_w": conv_w,
        "conv_b": row(conv_b),
        "w_gate": jnp.concatenate([_block_diag(lru_wa), _block_diag(lru_wx)], axis=2).astype(BF16),
        "b_gate": jnp.concatenate([row(lru_ba), row(lru_bx)], axis=2),
        "lam": row(lru_lambda),
        "g_q": row(q_norm),
        "w_q": jnp.concatenate([wq_a, wq_b], axis=2).astype(BF16),
        "g_kv": row(kv_norm),
        "w_kv": jnp.concatenate([wk, wv], axis=2).astype(BF16),
        "v_ones": ones_pat,
        "w_out": w_out.astype(BF16),
        "g_ffn": row(norm_ffn),
    }


def _layer_spec(arr, l, **kw):
    return pl.BlockSpec((None,) + arr.shape[1:], lambda *_: (l, 0, 0), **kw)


def kernel(x, positions, norm_mix, w_in, conv_w, conv_b, lru_wa, lru_ba, lru_wx, lru_bx, lru_lambda,
           q_norm, w_uq, kv_norm, w_ukv, w_out, norm_ffn, dense_w_gate, dense_w_up, dense_w_down,
           router_w, expert_w_gate, expert_w_up, expert_w_down, norm_final):
    B, S, _ = x.shape
    T = B * S
    depth = norm_mix.shape[0]
    ts = min(S, 512)
    tq = min(S, 512)
    tm = min(T, 512)
    ctab, stab = _rope_tables(positions)
    x2 = x.reshape(T, D_MODEL)
    experts_bf16 = {}
    stacked = _stacked_params(norm_mix, w_in, conv_w, conv_b, lru_wa, lru_ba, lru_wx, lru_bx,
                              lru_lambda, q_norm, w_uq, kv_norm, w_ukv, w_out, norm_ffn)
    for l in range(depth):
        p = dict(stacked)
        lru, q, k, v = _mix_in(x2, B, S, p, l, ctab, stab, ts, nb=2 if B % 2 == 0 else 1)
        j = l // 2
        last = l == depth - 1
        dense = l % 2 == 0
        hpb = N_HEADS
        attn_steps = B * (S // tq) * (N_HEADS // hpb)
        cast_dense = dense and attn_steps >= 2 and D_MODEL % (attn_steps * SEG_ALIGN) == 0
        dense_f32 = [dense_w_gate[j], dense_w_up[j]] if cast_dense else []
        att, dense_bf16 = _attention(q, k, v, B, S, tq, hpb, dense_f32)
        if dense:
            if cast_dense:
                p["w_gate_d"], p["w_up_d"] = dense_bf16
            else:
                p["w_gate_d"] = dense_w_gate[j].astype(BF16)
                p["w_up_d"] = dense_w_up[j].astype(BF16)
            p["w_down_d"] = dense_w_down[j].astype(BF16)
            to_cast = []
            if l + 1 < depth and T // tm >= 2:
                jn = (l + 1) // 2
                to_cast = [expert_w_gate[jn].reshape(-1, D_FF_EXPERT), expert_w_up[jn].reshape(-1, D_FF_EXPERT),
                           expert_w_down[jn].reshape(-1, D_MODEL)]
            x2, cast = _ffn_dense(x2, lru, att, p, l, tm, to_cast)
            if cast:
                experts_bf16[l + 1] = (cast[0].reshape(N_EXPERTS, D_MODEL, D_FF_EXPERT),
                                       cast[1].reshape(N_EXPERTS, D_MODEL, D_FF_EXPERT),
                                       cast[2].reshape(N_EXPERTS, D_FF_EXPERT, D_MODEL))
        else:
            wr = jnp.pad(router_w[j].T, ((0, E_PAD - N_EXPERTS), (0, 0)))
            p["w_router_hi"] = wr.astype(BF16)
            p["w_router_lo"] = (wr - p["w_router_hi"].astype(F32)).astype(BF16)
            xn, xs, tab, seg, cnt = _mix_out_route(x2, lru, att, p, l, tm)
            if l in experts_bf16:
                p["w_gate_e"], p["w_up_e"], p["w_down_e"] = experts_bf16[l]
            else:
                p["w_gate_e"] = expert_w_gate[j].astype(BF16)
                p["w_up_e"] = expert_w_up[j].astype(BF16)
                p["w_down_e"] = expert_w_down[j].astype(BF16)
            seg = seg[:, :N_EXPERTS, 0].reshape(-1)
            cnt = cnt[:, :N_EXPERTS, 0].reshape(-1)
            y = _moe_experts(xs, seg, cnt, p, gj=min(8, T // tm), big=512, ch=128)
            x2 = _moe_combine(xn, y, tab, norm_final[None, :], tm, final_norm=last)
    return x2.reshape(B, S, D_MODEL)
```
